```python
import jax, jax.numpy as jnp
from jax import lax
import numpy as np

D_MODEL = 1024
BATCH = 8
SEQ = 2048
DEPTH = 4

CHUNK = 64
QBLOCK = 128
ROPE_THETA = 10000.0
EPS = 1e-6
N_MIXERS = 3
N_A = (DEPTH + 2) // 3
N_B = (DEPTH + 1) // 3
N_C = DEPTH // 3

A_WIDTH = D_MODEL
CONV_WIDTH = 3

B_HEADS = 16
B_HEAD_DIM = 64
B_WIDTH = B_HEADS * B_HEAD_DIM
IDX_HEADS = 8
IDX_DIM = 64
IDX_ROPE_DIM = 32
TOPK_MAX = 256
B_IN_COLS = 4 * B_WIDTH + IDX_HEADS * IDX_DIM + IDX_DIM + IDX_HEADS

C_HEADS = 16
C_Q_LORA = 384
C_KV_LORA = 256
C_NOPE = 64
C_ROPE = 32
C_V = 64
C_QK = C_NOPE + C_ROPE
C_WIDTH = C_HEADS * C_V
C_IN_COLS = C_Q_LORA + C_KV_LORA + C_ROPE + C_WIDTH

kernel_name = "hybrid_conv_dsa_mla_stream_encoder"


def rmsnorm(x, g):
    xf = x.astype(jnp.float32)
    y = xf * lax.rsqrt(jnp.mean(xf * xf, axis=-1, keepdims=True) + EPS)
    return (y * g.astype(jnp.float32)).astype(x.dtype)


def rope(x, pos):
    d = x.shape[-1]
    half = d // 2
    inv = ROPE_THETA ** (-jnp.arange(half, dtype=jnp.float32) / half)
    ang = pos.astype(jnp.float32)[..., None] * inv
    cos = jnp.cos(ang)[:, :, None, :]
    sin = jnp.sin(ang)[:, :, None, :]
    xf = x.astype(jnp.float32)
    x1, x2 = xf[..., :half], xf[..., half:]
    out = jnp.concatenate([x1 * cos - x2 * sin, x2 * cos + x1 * sin], axis=-1)
    return out.astype(x.dtype)


def chunk_visible(q_pos, k_pos):
    return (k_pos // CHUNK)[None, :] <= (q_pos // CHUNK)[:, None]


def short_conv_mixer(xn, w_in, conv_w, conv_b, w_out):
    bg, cg, hv, z = jnp.split(xn @ w_in, 4, axis=-1)
    u = cg * hv
    y = lax.conv_general_dilated(
        u, conv_w[:, None, :].astype(u.dtype), window_strides=(1,),
        padding=[(CONV_WIDTH - 1, 0)], dimension_numbers=("NWC", "WIO", "NWC"),
        feature_group_count=A_WIDTH) + conv_b
    return (bg * y * jax.nn.silu(z)) @ w_out


def dsa_mixer(xn, positions, w_in, q_norm, k_norm, w_out):
    bsz, seq, _ = xn.shape
    cuts = np.cumsum([B_WIDTH, B_WIDTH, B_WIDTH, B_WIDTH, IDX_HEADS * IDX_DIM, IDX_DIM]).tolist()
    q, k, v, z, qi, ki, wi = jnp.split(xn @ w_in, cuts, axis=-1)
    q = rope(rmsnorm(q.reshape(bsz, seq, B_HEADS, B_HEAD_DIM), q_norm), positions)
    k = rope(rmsnorm(k.reshape(bsz, seq, B_HEADS, B_HEAD_DIM), k_norm), positions)
    v = v.reshape(bsz, seq, B_HEADS, B_HEAD_DIM)
    qi = qi.reshape(bsz, seq, IDX_HEADS, IDX_DIM)
    qi = jnp.concatenate([rope(qi[..., :IDX_ROPE_DIM], positions), qi[..., IDX_ROPE_DIM:]], axis=-1)
    ki = ki[:, :, None, :]
    ki = jnp.concatenate([rope(ki[..., :IDX_ROPE_DIM], positions), ki[..., IDX_ROPE_DIM:]], axis=-1)[:, :, 0, :]
    wi = wi * (IDX_HEADS ** -0.5 * IDX_DIM ** -0.5)

    topk = min(TOPK_MAX, seq // 4)
    nb = seq // QBLOCK
    scale = B_HEAD_DIM ** -0.5

    def blocks(t):
        return t.reshape((bsz * nb, QBLOCK) + t.shape[2:])

    bidx = jnp.repeat(jnp.arange(bsz, dtype=jnp.int32), nb)
    qstart = jnp.tile(jnp.arange(nb, dtype=jnp.int32) * QBLOCK, bsz)
    k_pos = jnp.arange(seq, dtype=jnp.int32)

    def attend(args):
        q_b, qi_b, w_b, b, q0 = args
        k_b, v_b, ki_b = k[b], v[b], ki[b]
        rel = jax.nn.relu(jnp.einsum("qhd,sd->qhs", qi_b, ki_b).astype(jnp.float32))
        score = jnp.einsum("qh,qhs->qs", w_b.astype(jnp.float32), rel)
        q_pos = q0 + jnp.arange(QBLOCK, dtype=jnp.int32)
        score = jnp.where(chunk_visible(q_pos, k_pos), score, -jnp.inf)
        top_val, top_idx = lax.top_k(score, topk)
        valid = jnp.isfinite(top_val)
        k_sel = k_b[top_idx]
        v_sel = v_b[top_idx]
        logits = jnp.einsum("qhd,qkhd->qhk", q_b, k_sel).astype(jnp.float32) * scale
        logits = jnp.where(valid[:, None, :], logits, -jnp.inf)
        p = jax.nn.softmax(logits, axis=-1).astype(v_sel.dtype)
        return jnp.einsum("qhk,qkhd->qhd", p, v_sel)

    o = lax.map(attend, (blocks(q), blocks(qi), blocks(wi), bidx, qstart))
    o = o.reshape(bsz, seq, B_WIDTH)
    return (o * jax.nn.silu(z)) @ w_out


def dense_chunk_attention(q, k, v, scale):
    bsz, seq, heads, dqk = q.shape
    nb = seq // QBLOCK
    qb = q.reshape(bsz, nb, QBLOCK, heads, dqk).transpose(1, 0, 2, 3, 4)
    starts = jnp.arange(nb, dtype=jnp.int32) * QBLOCK
    k_pos = jnp.arange(seq, dtype=jnp.int32)

    def attend(args):
        q_b, q0 = args
        logits = jnp.einsum("bqhd,bshd->bhqs", q_b, k).astype(jnp.float32) * scale
        mask = chunk_visible(q0 + jnp.arange(QBLOCK, dtype=jnp.int32), k_pos)
        logits = jnp.where(mask[None, None], logits, -jnp.inf)
        p = jax.nn.softmax(logits, axis=-1).astype(v.dtype)
        return jnp.einsum("bhqs,bshd->bqhd", p, v)

    o = lax.map(attend, (qb, starts))
    return o.transpose(1, 0, 2, 3, 4).reshape(bsz, seq, heads, v.shape[-1])


def mla_mixer(xn, positions, w_in, q_lat_norm, kv_lat_norm, w_uq, w_ukv, q_norm, k_norm, w_out):
    bsz, seq, _ = xn.shape
    cuts = np.cumsum([C_Q_LORA, C_KV_LORA, C_ROPE]).tolist()
    cq, ckv, kr, z = jnp.split(xn @ w_in, cuts, axis=-1)
    q = (rmsnorm(cq, q_lat_norm) @ w_uq).reshape(bsz, seq, C_HEADS, C_QK)
    kv = (rmsnorm(ckv, kv_lat_norm) @ w_ukv).reshape(bsz, seq, C_HEADS, C_NOPE + C_V)
    k_nope, v = kv[..., :C_NOPE], kv[..., C_NOPE:]
    k = jnp.concatenate([k_nope, jnp.broadcast_to(kr[:, :, None, :], (bsz, seq, C_HEADS, C_ROPE))], axis=-1)
    q = rmsnorm(q, q_norm)
    k = rmsnorm(k, k_norm)
    q = jnp.concatenate([q[..., :C_NOPE], rope(q[..., C_NOPE:], positions)], axis=-1)
    k = jnp.concatenate([k[..., :C_NOPE], rope(k[..., C_NOPE:], positions)], axis=-1)
    o = dense_chunk_attention(q, k, v, C_QK ** -0.5).reshape(bsz, seq, C_WIDTH)
    return (o * jax.nn.silu(z)) @ w_out


def setup_inputs(seed: int = 0) -> dict:
    key = jax.random.key(seed)
    ks = iter(jax.random.split(key, 32))
    f32 = jnp.float32

    def dense(shape, fan_in):
        return jax.random.normal(next(ks), shape, f32) * fan_in ** -0.5

    def gain(shape):
        return 1.0 + 0.02 * jax.random.normal(next(ks), shape, f32)

    x = jax.random.normal(next(ks), (BATCH, SEQ, D_MODEL), f32)
    offset = jax.random.randint(next(ks), (BATCH,), 0, 64, dtype=jnp.int32) * CHUNK
    positions = offset[:, None] + jnp.arange(SEQ, dtype=jnp.int32)[None, :]
    return {
        "x": x,
        "positions": positions,
        "a_norm": gain((N_A, D_MODEL)),
        "a_w_in": dense((N_A, D_MODEL, 4 * A_WIDTH), D_MODEL),
        "a_conv_w": dense((N_A, CONV_WIDTH, A_WIDTH), CONV_WIDTH),
        "a_conv_b": 0.02 * jax.random.normal(next(ks), (N_A, A_WIDTH), f32),
        "a_w_out": dense((N_A, A_WIDTH, D_MODEL), A_WIDTH),
        "b_norm": gain((N_B, D_MODEL)),
        "b_w_in": dense((N_B, D_MODEL, B_IN_COLS), D_MODEL),
        "b_q_norm": gain((N_B, B_HEAD_DIM)),
        "b_k_norm": gain((N_B, B_HEAD_DIM)),
        "b_w_out": dense((N_B, B_WIDTH, D_MODEL), B_WIDTH),
        "c_norm": gain((N_C, D_MODEL)),
        "c_w_in": dense((N_C, D_MODEL, C_IN_COLS), D_MODEL),
        "c_q_lat_norm": gain((N_C, C_Q_LORA)),
        "c_kv_lat_norm": gain((N_C, C_KV_LORA)),
        "c_w_uq": dense((N_C, C_Q_LORA, C_HEADS * C_QK), C_Q_LORA),
        "c_w_ukv": dense((N_C, C_KV_LORA, C_HEADS * (C_NOPE + C_V)), C_KV_LORA),
        "c_q_norm": gain((N_C, C_QK)),
        "c_k_norm": gain((N_C, C_QK)),
        "c_w_out": dense((N_C, C_WIDTH, D_MODEL), C_WIDTH),
    }


def reference(x, positions, a_norm, a_w_in, a_conv_w, a_conv_b, a_w_out,
              b_norm, b_w_in, b_q_norm, b_k_norm, b_w_out,
              c_norm, c_w_in, c_q_lat_norm, c_kv_lat_norm, c_w_uq, c_w_ukv, c_q_norm, c_k_norm, c_w_out):
    for i in range(DEPTH):
        kind, j = i % N_MIXERS, i // N_MIXERS
        if kind == 0:
            y = short_conv_mixer(rmsnorm(x, a_norm[j]), a_w_in[j], a_conv_w[j], a_conv_b[j], a_w_out[j])
        elif kind == 1:
            y = dsa_mixer(rmsnorm(x, b_norm[j]), positions, b_w_in[j], b_q_norm[j], b_k_norm[j], b_w_out[j])
        else:
            y = mla_mixer(rmsnorm(x, c_norm[j]), positions, c_w_in[j], c_q_lat_norm[j], c_kv_lat_norm[j],
                          c_w_uq[j], c_w_ukv[j], c_q_norm[j], c_k_norm[j], c_w_out[j])
        x = x + y
    return x
```

```python
import functools
import math

import jax
import jax.numpy as jnp
from jax import lax
from jax.experimental import pallas as pl
from jax.experimental.pallas import tpu as pltpu

EPS = 1e-6
ROPE_THETA = 10000.0
CHUNK_SHIFT = 6
TOPK_MAX = 256
LANES = 128
NEG_BIG = -1e30
INT_MIN = -(2 ** 31)
F32_MAX = 3.4028234663852886e38
LOG2E = math.log2(math.e)
VMEM_LIMIT = 56 * 1024 * 1024

BF16 = jnp.bfloat16
F32 = jnp.float32


def _dot(a, b):
    return jnp.dot(a, b, preferred_element_type=F32)


def _dot_nt(a, b):
    return lax.dot_general(a, b, (((1,), (1,)), ((), ())), preferred_element_type=F32)


def _rms(x, n):
    return x * lax.rsqrt(jnp.sum(x * x, axis=-1, keepdims=True) * (1.0 / n) + EPS)


def _silu(z):
    return z * jax.nn.sigmoid(z)


def _rope_tile(t, cos_t, sin_t, first_mask, half):
    partner = jnp.where(first_mask, pltpu.roll(t, LANES - half, 1), pltpu.roll(t, half, 1))
    return t * cos_t + partner * sin_t


def _params(sem):
    return pltpu.CompilerParams(dimension_semantics=sem, vmem_limit_bytes=VMEM_LIMIT)


def _const_spec(shape):
    nd = len(shape)
    return pl.BlockSpec(shape, lambda *_: (0,) * nd)


def _single(shape, index_map):
    return pl.BlockSpec(shape, index_map, pipeline_mode=pl.Buffered(1))


def _diag_visible(tq):
    r = lax.broadcasted_iota(jnp.int32, (tq, tq), 0)
    c = lax.broadcasted_iota(jnp.int32, (tq, tq), 1)
    return (c >> CHUNK_SHIFT) <= (r >> CHUNK_SHIFT)


def _softmax_pv(logit, v):
    p = jnp.exp2(logit - jnp.max(logit, axis=-1, keepdims=True))
    den = jnp.sum(p, axis=-1, keepdims=True)
    return _dot(p.astype(BF16), v) / den


def _per_block_variants(qb, n_blocks, body):
    for i in range(n_blocks):
        pl.when(qb == i)(functools.partial(body, i))


def _conv_kernel(x_ref, g_ref, w_in_ref, cw_ref, cb_ref, w_out_ref, o_ref, u_scr, *, tm, width):
    j = pl.program_id(1)
    x = x_ref[0]
    d = x.shape[-1]
    xb = (_rms(x, d) * g_ref[...]).astype(BF16)
    bg = _dot(xb, w_in_ref[:, 0 * width:1 * width])
    cg = _dot(xb, w_in_ref[:, 1 * width:2 * width])
    hv = _dot(xb, w_in_ref[:, 2 * width:3 * width])
    z = _dot(xb, w_in_ref[:, 3 * width:4 * width])
    u = cg * hv

    @pl.when(j == 0)
    def _():
        u_scr[0:8, :] = jnp.zeros((8, width), F32)

    @pl.when(j > 0)
    def _():
        u_scr[0:8, :] = u_scr[tm:tm + 8, :]

    u_scr[8:tm + 8, :] = u
    y = (cw_ref[2:3, :] * u + cw_ref[1:2, :] * u_scr[7:7 + tm, :]
         + cw_ref[0:1, :] * u_scr[6:6 + tm, :] + cb_ref[...])
    g = (bg * y * _silu(z)).astype(BF16)
    o_ref[0] = x + _dot(g, w_out_ref[...])


def _conv_layer(x, g, w_in, cw, cb, w_out, *, tm=256):
    b, s, d = x.shape
    width = w_out.shape[0]
    return pl.pallas_call(
        functools.partial(_conv_kernel, tm=tm, width=width),
        grid=(b, s // tm),
        in_specs=[
            pl.BlockSpec((1, tm, d), lambda bi, i: (bi, i, 0)),
            _const_spec((1, d)),
            _const_spec((d, 4 * width)),
            _const_spec((3, width)),
            _const_spec((1, width)),
            _const_spec((width, d)),
        ],
        out_specs=pl.BlockSpec((1, tm, d), lambda bi, i: (bi, i, 0)),
        out_shape=jax.ShapeDtypeStruct((b, s, d), F32),
        scratch_shapes=[pltpu.VMEM((tm + 8, width), F32)],
        compiler_params=_params(("arbitrary", "arbitrary")),
        name="conv_mixer",
    )(x, g.reshape(1, d), w_in.astype(BF16), cw, cb.reshape(1, width), w_out.astype(BF16))


def _dsa_proj_kernel(x_ref, pos_ref, g_ref, w_main_ref, w_qi_ref, w_kiwi_ref, qg_ref, kg_ref,
                     inv_ref, sgn_ref,
                     q_ref, k_ref, v_ref, z_ref, qi_ref, ki_ref, wi_ref,
                     *, width, n_pair, n_qi_tile, scale, wi_scale):
    x = x_ref[0]
    d = x.shape[-1]
    tm = x.shape[0]
    xb = (_rms(x, d) * g_ref[...]).astype(BF16)
    pos = pos_ref[0].astype(F32)
    ang_a = pos * inv_ref[0:1, :]
    ang_b = pos * inv_ref[1:2, :]
    cos_a, sin_a = jnp.cos(ang_a), jnp.sin(ang_a) * sgn_ref[0:1, :]
    cos_b, sin_b = jnp.cos(ang_b), jnp.sin(ang_b) * sgn_ref[1:2, :]
    lane = lax.broadcasted_iota(jnp.int32, (tm, LANES), 1)
    lo = lane < 64
    l64 = lane & 63
    first_a = l64 < 32
    first_b = l64 < 16

    q = _dot(xb, w_main_ref[:, 0 * width:1 * width])
    k = _dot(xb, w_main_ref[:, 1 * width:2 * width])
    v = _dot(xb, w_main_ref[:, 2 * width:3 * width])
    z = _dot(xb, w_main_ref[:, 3 * width:4 * width])
    z_ref[0] = z.astype(BF16)

    def head_norm_rope(t, gain):
        sq = t * t
        s_lo = jnp.sum(jnp.where(lo, sq, 0.0), axis=-1, keepdims=True)
        s_hi = jnp.sum(jnp.where(lo, 0.0, sq), axis=-1, keepdims=True)
        ms = jnp.where(lo, s_lo, s_hi) * (1.0 / 64)
        t = t * lax.rsqrt(ms + EPS) * gain
        return _rope_tile(t, cos_a, sin_a, first_a, 32)

    for j in range(n_pair):
        sl = slice(j * LANES, (j + 1) * LANES)
        qt = head_norm_rope(q[:, sl], qg_ref[...]) * scale
        q_ref[0, 2 * j] = jnp.where(lo, qt, 0.0).astype(BF16)
        q_ref[0, 2 * j + 1] = jnp.where(lo, 0.0, qt).astype(BF16)
        k_ref[0, j] = head_norm_rope(k[:, sl], kg_ref[...]).astype(BF16)
        v_ref[0, j] = v[:, sl].astype(BF16)

    qi = _dot(xb, w_qi_ref[...])
    for j in range(n_qi_tile):
        t = _rope_tile(qi[:, j * LANES:(j + 1) * LANES], cos_b, sin_b, first_b, 16)
        qi_ref[0, 2 * j] = jnp.where(lo, t, 0.0).astype(BF16)
        qi_ref[0, 2 * j + 1] = jnp.where(lo, 0.0, t).astype(BF16)

    kiwi = _dot(xb, w_kiwi_ref[...])
    ki_ref[0] = _rope_tile(kiwi[:, 0:LANES], cos_b, sin_b, first_b, 16).astype(BF16)
    wi_ref[0] = kiwi[:, LANES:2 * LANES] * wi_scale


def _key_to_float(key):
    return pltpu.bitcast(jnp.where(key >= 0, key, key ^ 0x7FFFFFFF), F32)


def _dsa_select_bias(score_scr, bias_scr, nvis, tq, topk):
    topk_f = float(topk)

    def count_ge(cand):
        return jnp.sum(jnp.where(score_scr[:, 0:nvis] >= cand, 1.0, 0.0), axis=1, keepdims=True)

    base = jnp.where(count_ge(jnp.zeros((tq, 1), F32)) >= topk_f, 0, INT_MIN).astype(jnp.int32)

    def bit_step(i, base):
        cand = base | jnp.left_shift(jnp.int32(1), 30 - i)
        return jnp.where(count_ge(_key_to_float(cand)) >= topk_f, cand, base)

    thr = _key_to_float(lax.fori_loop(0, 31, bit_step, base))
    lowest = jnp.full((tq, 1), -F32_MAX, F32)
    thr = jnp.where(count_ge(lowest) < topk_f, lowest, thr)
    excess = jnp.max(count_ge(thr)) > topk_f

    def plain(_):
        bias_scr[:, 0:nvis] = jnp.where(score_scr[:, 0:nvis] >= thr, 0.0, NEG_BIG)
        return 0

    def index_ordered_ties(_):
        need = topk_f - jnp.sum(jnp.where(score_scr[:, 0:nvis] > thr, 1.0, 0.0), axis=1, keepdims=True)
        r = lax.broadcasted_iota(jnp.int32, (LANES, LANES), 0)
        c = lax.broadcasted_iota(jnp.int32, (LANES, LANES), 1)
        upper = jnp.where(r < c, 1.0, 0.0).astype(BF16)
        carry = jnp.zeros((tq, 1), F32)
        for t in range(nvis // LANES):
            sl = slice(t * LANES, (t + 1) * LANES)
            s = score_scr[:, sl]
            tie_f = jnp.where(s == thr, 1.0, 0.0)
            before = _dot(tie_f.astype(BF16), upper) + carry
            carry = carry + jnp.sum(tie_f, axis=1, keepdims=True)
            tie_bias = jnp.where(s == thr, jnp.where(before < need, 0.0, NEG_BIG), NEG_BIG)
            bias_scr[:, sl] = jnp.where(s > thr, 0.0, tie_bias)
        return 0

    lax.cond(excess, index_ordered_ties, plain, 0)


def _dsa_attn_kernel(q_ref, qi_ref, wi_ref, k_ref, v_ref, ki_ref, z_ref, x_ref, w_out_ref,
                     o_ref, score_scr, bias_scr, o_scr, *, tq, n_blocks, topk, n_idx_heads, n_pair):
    qb = pl.program_id(1)
    lo = lax.broadcasted_iota(jnp.int32, (tq, LANES), 1) < 64

    def body(i):
        nvis = (i + 1) * tq
        diag_vis = _diag_visible(tq)
        if nvis <= topk:
            bias_scr[:, 0:nvis] = jnp.where(diag_vis, 0.0, NEG_BIG)
        else:
            ki = ki_ref[0, 0:nvis, :]
            wi = wi_ref[0]
            score = None
            for h in range(n_idx_heads):
                term = wi[:, h:h + 1] * jnp.maximum(_dot_nt(qi_ref[0, h], ki), 0.0)
                score = term if score is None else score + term
            if nvis > tq:
                score_scr[:, 0:nvis - tq] = score[:, 0:nvis - tq]
            score_scr[:, nvis - tq:nvis] = jnp.where(diag_vis, score[:, nvis - tq:nvis], -jnp.inf)
            _dsa_select_bias(score_scr, bias_scr, nvis, tq, topk)

        def pair_step(j, _):
            kt = k_ref[0, j, 0:nvis, :]
            vt = v_ref[0, j, 0:nvis, :]
            qq = q_ref[0, pl.ds(2 * j, 2)].reshape(2 * tq, LANES)
            logit = _dot_nt(qq, kt)
            bias = bias_scr[:, 0:nvis]
            o_even = _softmax_pv(logit[0:tq] + bias, vt)
            o_odd = _softmax_pv(logit[tq:2 * tq] + bias, vt)
            o_scr[j] = jnp.where(lo, o_even, o_odd)
            return 0

        lax.fori_loop(0, n_pair, pair_step, 0)
        o = jnp.concatenate([o_scr[j] for j in range(n_pair)], axis=1)
        g = (o * _silu(z_ref[0].astype(F32))).astype(BF16)
        o_ref[0] = x_ref[0] + _dot(g, w_out_ref[...])

    _per_block_variants(qb, n_blocks, body)


def _rope_inv(half):
    return ROPE_THETA ** (-jnp.arange(half, dtype=F32) / half)


def _dsa_layer(x, pos3, g, w_in, q_norm, k_norm, w_out, *, tm=256, tq=256):
    b, s, d = x.shape
    width = w_out.shape[0]
    head_dim = q_norm.shape[0]
    assert head_dim == 64 and width % LANES == 0
    n_pair = width // LANES
    idx_dim = 64
    n_idx_heads = (w_in.shape[1] - 4 * width - idx_dim) // (idx_dim + 1)
    assert 4 * width + n_idx_heads * idx_dim + idx_dim + n_idx_heads == w_in.shape[1]
    n_qi_tile = n_idx_heads * idx_dim // LANES
    c0 = 4 * width
    c1 = c0 + n_idx_heads * idx_dim
    c2 = c1 + idx_dim
    w_main = w_in[:, :c0].astype(BF16)
    w_qi = w_in[:, c0:c1].astype(BF16)
    w_ki = w_in[:, c1:c2]
    w_wi = jnp.pad(w_in[:, c2:], ((0, 0), (0, LANES - n_idx_heads)))
    w_kiwi = jnp.concatenate([w_ki, w_ki, w_wi], axis=1).astype(BF16)

    inv32 = _rope_inv(32)
    inv16 = _rope_inv(16)
    zeros32 = jnp.zeros((32,), F32)
    inv_a = jnp.tile(inv32, 4)
    inv_b = jnp.tile(jnp.concatenate([inv16, inv16, zeros32]), 2)
    inv = jnp.stack([inv_a, inv_b])
    sgn_a = jnp.tile(jnp.concatenate([-jnp.ones((32,), F32), jnp.ones((32,), F32)]), 2)
    sgn_b = jnp.tile(jnp.concatenate([-jnp.ones((16,), F32), jnp.ones((48,), F32)]), 2)
    sgn = jnp.stack([sgn_a, sgn_b])
    qg = jnp.tile(q_norm, 2).reshape(1, LANES)
    kg = jnp.tile(k_norm, 2).reshape(1, LANES)

    topk = min(TOPK_MAX, s // 4)
    assert s % tq == 0 and tq % (1 << CHUNK_SHIFT) == 0
    n_heads = 2 * n_pair
    row = lambda bi, i: (bi, i, 0)
    head_row = lambda bi, i: (bi, 0, i, 0)
    q, k, v, z, qi, ki, wi = pl.pallas_call(
        functools.partial(_dsa_proj_kernel, width=width, n_pair=n_pair, n_qi_tile=n_qi_tile,
                          scale=head_dim ** -0.5 * LOG2E,
                          wi_scale=n_idx_heads ** -0.5 * idx_dim ** -0.5),
        grid=(b, s // tm),
        in_specs=[
            pl.BlockSpec((1, tm, d), row),
            pl.BlockSpec((1, tm, 1), row),
            _const_spec((1, d)),
            _const_spec((d, c0)),
            _const_spec((d, c1 - c0)),
            _const_spec((d, 3 * LANES)),
            _const_spec((1, LANES)),
            _const_spec((1, LANES)),
            _const_spec((2, LANES)),
            _const_spec((2, LANES)),
        ],
        out_specs=[
            pl.BlockSpec((1, n_heads, tm, LANES), head_row),
            pl.BlockSpec((1, n_pair, tm, LANES), head_row),
            pl.BlockSpec((1, n_pair, tm, LANES), head_row),
            pl.BlockSpec((1, tm, width), row),
            pl.BlockSpec((1, n_idx_heads, tm, LANES), head_row),
            pl.BlockSpec((1, tm, LANES), row),
            pl.BlockSpec((1, tm, LANES), row),
        ],
        out_shape=[
            jax.ShapeDtypeStruct((b, n_heads, s, LANES), BF16),
            jax.ShapeDtypeStruct((b, n_pair, s, LANES), BF16),
            jax.ShapeDtypeStruct((b, n_pair, s, LANES), BF16),
            jax.ShapeDtypeStruct((b, s, width), BF16),
            jax.ShapeDtypeStruct((b, n_idx_heads, s, LANES), BF16),
            jax.ShapeDtypeStruct((b, s, LANES), BF16),
            jax.ShapeDtypeStruct((b, s, LANES), F32),
        ],
        compiler_params=_params(("arbitrary", "arbitrary")),
        name="dsa_proj",
    )(x, pos3, g.reshape(1, d), w_main, w_qi, w_kiwi, qg, kg, inv, sgn)

    qrow = lambda bi, i: (bi, i, 0)
    qhead = lambda bi, i: (bi, 0, i, 0)
    full = lambda bi, i: (bi, 0, 0, 0)
    return pl.pallas_call(
        functools.partial(_dsa_attn_kernel, tq=tq, n_blocks=s // tq, topk=topk,
                          n_idx_heads=n_idx_heads, n_pair=n_pair),
        grid=(b, s // tq),
        in_specs=[
            pl.BlockSpec((1, n_heads, tq, LANES), qhead),
            pl.BlockSpec((1, n_idx_heads, tq, LANES), qhead),
            pl.BlockSpec((1, tq, LANES), qrow),
            _single((1, n_pair, s, LANES), full),
            _single((1, n_pair, s, LANES), full),
            _single((1, s, LANES), lambda bi, i: (bi, 0, 0)),
            pl.BlockSpec((1, tq, width), qrow),
            pl.BlockSpec((1, tq, d), qrow),
            _single((width, d), lambda bi, i: (0, 0)),
        ],
        out_specs=pl.BlockSpec((1, tq, d), qrow),
        out_shape=jax.ShapeDtypeStruct((b, s, d), F32),
        scratch_shapes=[
            pltpu.VMEM((tq, s), F32),
            pltpu.VMEM((tq, s), F32),
            pltpu.VMEM((n_pair, tq, LANES), F32),
        ],
        compiler_params=_params(("arbitrary", "arbitrary")),
        name="dsa_attn",
    )(q, qi, wi, k, v, ki, z, x, w_out.astype(BF16))


def _mla_proj_kernel(x_ref, pos_ref, g_ref, w_in_ref, qlat_g_ref, kvlat_g_ref, w_uq_ref, w_uk_ref,
                     w_uv_ref, qg_ref, kg_ref, inv_ref, sgn_ref,
                     q_ref, k_ref, v_ref, z_ref,
                     *, q_lora, kv_lora, n_heads, qk_dim, rope_half, scale):
    x = x_ref[0]
    d = x.shape[-1]
    tm = x.shape[0]
    xb = (_rms(x, d) * g_ref[...]).astype(BF16)
    pos = pos_ref[0].astype(F32)
    ang = pos * inv_ref[...]
    cos_t, sin_t = jnp.cos(ang), jnp.sin(ang) * sgn_ref[...]
    lane = lax.broadcasted_iota(jnp.int32, (tm, LANES), 1)
    first = (lane >= 64) & (lane < 64 + rope_half)

    c1 = q_lora + kv_lora
    cq = _dot(xb, w_in_ref[:, 0:q_lora])
    ckv = _dot(xb, w_in_ref[:, q_lora:c1])
    kr = _dot(xb, w_in_ref[:, c1:c1 + LANES])
    z = _dot(xb, w_in_ref[:, c1 + LANES:])
    z_ref[0] = z.astype(BF16)

    q = _dot((_rms(cq, q_lora) * qlat_g_ref[...]).astype(BF16), w_uq_ref[...])
    ckv_b = (_rms(ckv, kv_lora) * kvlat_g_ref[...]).astype(BF16)
    kn = _dot(ckv_b, w_uk_ref[...])
    v = _dot(ckv_b, w_uv_ref[...])

    for h in range(n_heads):
        sl = slice(h * LANES, (h + 1) * LANES)
        qt = _rms(q[:, sl], qk_dim) * qg_ref[...]
        q_ref[0, h] = (_rope_tile(qt, cos_t, sin_t, first, rope_half) * scale).astype(BF16)
        kt = _rms(kn[:, sl] + kr, qk_dim) * kg_ref[...]
        k_ref[0, h] = _rope_tile(kt, cos_t, sin_t, first, rope_half).astype(BF16)
    for j in range(n_heads // 2):
        v_ref[0, j] = v[:, j * LANES:(j + 1) * LANES].astype(BF16)


def _mla_attn_kernel(q_ref, k_ref, v_ref, z_ref, x_ref, w_out_ref, o_ref, o_scr,
                     *, tq, n_blocks, n_pair):
    qb = pl.program_id(1)
    lo = lax.broadcasted_iota(jnp.int32, (tq, LANES), 1) < 64

    def body(i):
        nvis = (i + 1) * tq
        diag_bias = jnp.where(_diag_visible(tq), 0.0, NEG_BIG)

        def head_out(h, vt):
            logit = _dot_nt(q_ref[0, h], k_ref[0, h, 0:nvis, :])
            last = logit[:, nvis - tq:nvis] + diag_bias
            if nvis > tq:
                logit = jnp.concatenate([logit[:, 0:nvis - tq], last], axis=1)
            else:
                logit = last
            return _softmax_pv(logit, vt)

        def pair_step(j, _):
            vt = v_ref[0, j, 0:nvis, :]
            o_scr[j] = jnp.where(lo, head_out(2 * j, vt), head_out(2 * j + 1, vt))
            return 0

        lax.fori_loop(0, n_pair, pair_step, 0)
        o = jnp.concatenate([o_scr[j] for j in range(n_pair)], axis=1)
        g = (o * _silu(z_ref[0].astype(F32))).astype(BF16)
        o_ref[0] = x_ref[0] + _dot(g, w_out_ref[...])

    _per_block_variants(qb, n_blocks, body)


def _mla_layer(x, pos3, g, w_in, q_lat_norm, kv_lat_norm, w_uq, w_ukv, q_norm, k_norm, w_out,
               *, tm=256, tq=256):
    b, s, d = x.shape
    width = w_out.shape[0]
    q_lora = q_lat_norm.shape[0]
    kv_lora = kv_lat_norm.shape[0]
    qk_dim = q_norm.shape[0]
    v_dim = 64
    nope = 64
    rope_dim = qk_dim - nope
    n_heads = width // v_dim
    assert nope + rope_dim <= LANES and w_ukv.shape[1] == n_heads * (nope + v_dim)
    assert w_in.shape[1] == q_lora + kv_lora + rope_dim + width
    assert s % tq == 0 and tq % (1 << CHUNK_SHIFT) == 0
    pad = LANES - qk_dim

    c1 = q_lora + kv_lora
    w_kr = jnp.pad(w_in[:, c1:c1 + rope_dim], ((0, 0), (nope, LANES - nope - rope_dim)))
    w_in_p = jnp.concatenate([w_in[:, :c1], w_kr, w_in[:, c1 + rope_dim:]], axis=1).astype(BF16)
    w_uq_p = jnp.pad(w_uq.reshape(q_lora, n_heads, qk_dim), ((0, 0), (0, 0), (0, pad)))
    w_uq_p = w_uq_p.reshape(q_lora, n_heads * LANES).astype(BF16)
    w_ukv3 = w_ukv.reshape(kv_lora, n_heads, nope + v_dim)
    w_uk_p = jnp.pad(w_ukv3[:, :, :nope], ((0, 0), (0, 0), (0, LANES - nope)))
    w_uk_p = w_uk_p.reshape(kv_lora, n_heads * LANES).astype(BF16)
    w_uv = w_ukv3[:, :, nope:].reshape(kv_lora, n_heads * v_dim).astype(BF16)
    qg = jnp.pad(q_norm, (0, pad)).reshape(1, LANES)
    kg = jnp.pad(k_norm, (0, pad)).reshape(1, LANES)

    rope_half = rope_dim // 2
    inv_h = _rope_inv(rope_half)
    inv = jnp.concatenate([jnp.zeros((nope,), F32), inv_h, inv_h,
                           jnp.zeros((LANES - nope - rope_dim,), F32)]).reshape(1, LANES)
    sgn = jnp.concatenate([jnp.ones((nope,), F32), -jnp.ones((rope_half,), F32),
                           jnp.ones((LANES - nope - rope_half,), F32)]).reshape(1, LANES)

    n_pair = n_heads // 2
    row = lambda bi, i: (bi, i, 0)
    head_row = lambda bi, i: (bi, 0, i, 0)
    n_in = w_in_p.shape[1]
    q, k, v, z = pl.pallas_call(
        functools.partial(_mla_proj_kernel, q_lora=q_lora, kv_lora=kv_lora, n_heads=n_heads,
                          qk_dim=qk_dim, rope_half=rope_half, scale=qk_dim ** -0.5 * LOG2E),
        grid=(b, s // tm),
        in_specs=[
            pl.BlockSpec((1, tm, d), row),
            pl.BlockSpec((1, tm, 1), row),
            _const_spec((1, d)),
            _const_spec((d, n_in)),
            _const_spec((1, q_lora)),
            _const_spec((1, kv_lora)),
            _const_spec((q_lora, n_heads * LANES)),
            _const_spec((kv_lora, n_heads * LANES)),
            _const_spec((kv_lora, n_heads * v_dim)),
            _const_spec((1, LANES)),
            _const_spec((1, LANES)),
            _const_spec((1, LANES)),
            _const_spec((1, LANES)),
        ],
        out_specs=[
            pl.BlockSpec((1, n_heads, tm, LANES), head_row),
            pl.BlockSpec((1, n_heads, tm, LANES), head_row),
            pl.BlockSpec((1, n_pair, tm, LANES), head_row),
            pl.BlockSpec((1, tm, width), row),
        ],
        out_shape=[
            jax.ShapeDtypeStruct((b, n_heads, s, LANES), BF16),
            jax.ShapeDtypeStruct((b, n_heads, s, LANES), BF16),
            jax.ShapeDtypeStruct((b, n_pair, s, LANES), BF16),
            jax.ShapeDtypeStruct((b, s, width), BF16),
        ],
        compiler_params=_params(("arbitrary", "arbitrary")),
        name="mla_proj",
    )(x, pos3, g.reshape(1, d), w_in_p, q_lat_norm.reshape(1, q_lora), kv_lat_norm.reshape(1, kv_lora),
      w_uq_p, w_uk_p, w_uv, qg, kg, inv, sgn)

    qrow = lambda bi, i: (bi, i, 0)
    qhead = lambda bi, i: (bi, 0, i, 0)
    full = lambda bi, i: (bi, 0, 0, 0)
    return pl.pallas_call(
        functools.partial(_mla_attn_kernel, tq=tq, n_blocks=s // tq, n_pair=n_pair),
        grid=(b, s // tq),
        in_specs=[
            pl.BlockSpec((1, n_heads, tq, LANES), qhead),
            _single((1, n_heads, s, LANES), full),
            _single((1, n_pair, s, LANES), full),
            pl.BlockSpec((1, tq, width), qrow),
            pl.BlockSpec((1, tq, d), qrow),
            _single((width, d), lambda bi, i: (0, 0)),
        ],
        out_specs=pl.BlockSpec((1, tq, d), qrow),
        out_shape=jax.ShapeDtypeStruct((b, s, d), F32),
        scratch_shapes=[
            pltpu.VMEM((n_pair, tq, LANES), F32),
        ],
        compiler_params=_params(("arbitrary", "arbitrary")),
        name="mla_attn",
    )(q, k, v, z, x, w_out.astype(BF16))


def kernel(x, positions, a_norm, a_w_in, a_conv_w, a_conv_b, a_w_out, b_norm, b_w_in, b_q_norm, b_k_norm, b_w_out, c_norm, c_w_in, c_q_lat_norm, c_kv_lat_norm, c_w_uq, c_w_ukv, c_q_norm, c_k_norm, c_w_out):
    depth = a_norm.shape[0] + b_norm.shape[0] + c_norm.shape[0]
    pos3 = positions.reshape(positions.shape + (1,))
    for i in range(depth):
        kind, j = i % 3, i // 3
        if kind == 0:
            x = _conv_layer(x, a_norm[j], a_w_in[j], a_conv_w[j], a_conv_b[j], a_w_out[j])
        elif kind == 1:
            x = _dsa_layer(x, pos3, b_norm[j], b_w_in[j], b_q_norm[j], b_k_norm[j], b_w_out[j])
        else:
            x = _mla_layer(x, pos3, c_norm[j], c_w_in[j], c_q_lat_norm[j], c_kv_lat_norm[j],
                           c_w_uq[j], c_w_ukv[j], c_q_norm[j], c_k_norm[j], c_w_out[j])
    return x
```

```python
import functools
import math

import jax
import jax.numpy as jnp
from jax import lax
from jax.experimental import pallas as pl
from jax.experimental.pallas import tpu as pltpu

EPS = 1e-6
ROPE_THETA = 10000.0
CHUNK_SHIFT = 6
TOPK_MAX = 256
LANES = 128
NEG_BIG = -1e30
INT_MIN = -(2 ** 31)
F32_MAX = 3.4028234663852886e38
LOG2E = math.log2(math.e)
VMEM_LIMIT = 56 * 1024 * 1024

BF16 = jnp.bfloat16
F32 = jnp.float32


def _dot(a, b):
    return jnp.dot(a, b, preferred_element_type=F32)


def _dot_nt(a, b):
    return lax.dot_general(a, b, (((1,), (1,)), ((), ())), preferred_element_type=F32)


def _rms(x, n):
    return x * lax.rsqrt(jnp.sum(x * x, axis=-1, keepdims=True) * (1.0 / n) + EPS)


def _silu(z):
    return z * jax.nn.sigmoid(z)


def _rope_tile(t, cos_t, sin_t, first_mask, half):
    partner = jnp.where(first_mask, pltpu.roll(t, LANES - half, 1), pltpu.roll(t, half, 1))
    return t * cos_t + partner * sin_t


def _params(sem):
    return pltpu.CompilerParams(dimension_semantics=sem, vmem_limit_bytes=VMEM_LIMIT)


def _const_spec(shape):
    nd = len(shape)
    return pl.BlockSpec(shape, lambda *_: (0,) * nd)


def _single(shape, index_map):
    return pl.BlockSpec(shape, index_map, pipeline_mode=pl.Buffered(1))


def _diag_visible(tq):
    r = lax.broadcasted_iota(jnp.int32, (tq, tq), 0)
    c = lax.broadcasted_iota(jnp.int32, (tq, tq), 1)
    return (c >> CHUNK_SHIFT) <= (r >> CHUNK_SHIFT)


def _softmax_pv(logit, v):
    p = jnp.exp2(logit - jnp.max(logit, axis=-1, keepdims=True))
    den = jnp.sum(p, axis=-1, keepdims=True)
    return _dot(p.astype(BF16), v) / den


def _per_block_variants(qb, n_blocks, body):
    for i in range(n_blocks):
        pl.when(qb == i)(functools.partial(body, i))


def _conv_kernel(x_ref, g_ref, w_in_ref, cw_ref, cb_ref, w_out_ref, o_ref, u_scr, *, tm, width):
    j = pl.program_id(1)
    x = x_ref[0]
    d = x.shape[-1]
    xb = (_rms(x, d) * g_ref[...]).astype(BF16)
    bg = _dot(xb, w_in_ref[:, 0 * width:1 * width])
    cg = _dot(xb, w_in_ref[:, 1 * width:2 * width])
    hv = _dot(xb, w_in_ref[:, 2 * width:3 * width])
    z = _dot(xb, w_in_ref[:, 3 * width:4 * width])
    u = cg * hv

    @pl.when(j == 0)
    def _():
        u_scr[0:8, :] = jnp.zeros((8, width), F32)

    @pl.when(j > 0)
    def _():
        u_scr[0:8, :] = u_scr[tm:tm + 8, :]

    u_scr[8:tm + 8, :] = u
    y = (cw_ref[2:3, :] * u + cw_ref[1:2, :] * u_scr[7:7 + tm, :]
         + cw_ref[0:1, :] * u_scr[6:6 + tm, :] + cb_ref[...])
    g = (bg * y * _silu(z)).astype(BF16)
    o_ref[0] = x + _dot(g, w_out_ref[...])


def _conv_layer(x, g, w_in, cw, cb, w_out, *, tm=256):
    b, s, d = x.shape
    width = w_out.shape[0]
    return pl.pallas_call(
        functools.partial(_conv_kernel, tm=tm, width=width),
        grid=(b, s // tm),
        in_specs=[
            pl.BlockSpec((1, tm, d), lambda bi, i: (bi, i, 0)),
            _const_spec((1, d)),
            _const_spec((d, 4 * width)),
            _const_spec((3, width)),
            _const_spec((1, width)),
            _const_spec((width, d)),
        ],
        out_specs=pl.BlockSpec((1, tm, d), lambda bi, i: (bi, i, 0)),
        out_shape=jax.ShapeDtypeStruct((b, s, d), F32),
        scratch_shapes=[pltpu.VMEM((tm + 8, width), F32)],
        compiler_params=_params(("arbitrary", "arbitrary")),
        name="conv_mixer",
    )(x, g.reshape(1, d), w_in.astype(BF16), cw, cb.reshape(1, width), w_out.astype(BF16))


def _dsa_proj_kernel(x_ref, pos_ref, g_ref, w_main_ref, w_qi_ref, w_kiwi_ref, qg_ref, kg_ref,
                     inv_ref, sgn_ref,
                     q_ref, k_ref, v_ref, z_ref, qi_ref, ki_ref, wi_ref,
                     *, width, n_pair, n_qi_tile, scale, wi_scale):
    x = x_ref[0]
    d = x.shape[-1]
    tm = x.shape[0]
    xb = (_rms(x, d) * g_ref[...]).astype(BF16)
    pos = pos_ref[0].astype(F32)
    ang_a = pos * inv_ref[0:1, :]
    ang_b = pos * inv_ref[1:2, :]
    cos_a, sin_a = jnp.cos(ang_a), jnp.sin(ang_a) * sgn_ref[0:1, :]
    cos_b, sin_b = jnp.cos(ang_b), jnp.sin(ang_b) * sgn_ref[1:2, :]
    lane = lax.broadcasted_iota(jnp.int32, (tm, LANES), 1)
    lo = lane < 64
    first_b = (lane & 63) < 16
    head0 = (lane & 32) == 0

    q = _dot(xb, w_main_ref[:, 0 * width:1 * width])
    k = _dot(xb, w_main_ref[:, 1 * width:2 * width])
    v = _dot(xb, w_main_ref[:, 2 * width:3 * width])
    z = _dot(xb, w_main_ref[:, 3 * width:4 * width])
    z_ref[0] = z.astype(BF16)

    def head_norm_rope(t, gain):
        sq = t * t
        s0 = jnp.sum(jnp.where(head0, sq, 0.0), axis=-1, keepdims=True)
        s1 = jnp.sum(jnp.where(head0, 0.0, sq), axis=-1, keepdims=True)
        ms = jnp.where(head0, s0, s1) * (1.0 / 64)
        t = t * lax.rsqrt(ms + EPS) * gain
        return t * cos_a + pltpu.roll(t, 64, 1) * sin_a

    for j in range(n_pair):
        sl = slice(j * LANES, (j + 1) * LANES)
        qt = head_norm_rope(q[:, sl], qg_ref[...]) * scale
        q_ref[0, 2 * j] = jnp.where(head0, qt, 0.0).astype(BF16)
        q_ref[0, 2 * j + 1] = jnp.where(head0, 0.0, qt).astype(BF16)
        k_ref[0, j] = head_norm_rope(k[:, sl], kg_ref[...]).astype(BF16)
        v_ref[0, j] = v[:, sl].astype(BF16)

    qi = _dot(xb, w_qi_ref[...])
    for j in range(n_qi_tile):
        t = _rope_tile(qi[:, j * LANES:(j + 1) * LANES], cos_b, sin_b, first_b, 16)
        qi_ref[0, 2 * j] = jnp.where(lo, t, 0.0).astype(BF16)
        qi_ref[0, 2 * j + 1] = jnp.where(lo, 0.0, t).astype(BF16)

    kiwi = _dot(xb, w_kiwi_ref[...])
    ki_ref[0] = _rope_tile(kiwi[:, 0:LANES], cos_b, sin_b, first_b, 16).astype(BF16)
    wi_ref[0] = kiwi[:, LANES:2 * LANES] * wi_scale


def _key_to_float(key):
    return pltpu.bitcast(jnp.where(key >= 0, key, key ^ 0x7FFFFFFF), F32)


def _dsa_select_bias(score_scr, bias_scr, nvis, tq, topk):
    topk_f = float(topk)

    def count_ge(cand):
        return jnp.sum(jnp.where(score_scr[:, 0:nvis] >= cand, 1.0, 0.0), axis=1, keepdims=True)

    base = jnp.where(count_ge(jnp.zeros((tq, 1), F32)) >= topk_f, 0, INT_MIN).astype(jnp.int32)

    def bit_step(i, base):
        cand = base | jnp.left_shift(jnp.int32(1), 30 - i)
        return jnp.where(count_ge(_key_to_float(cand)) >= topk_f, cand, base)

    thr = _key_to_float(lax.fori_loop(0, 31, bit_step, base))
    lowest = jnp.full((tq, 1), -F32_MAX, F32)
    thr = jnp.where(count_ge(lowest) < topk_f, lowest, thr)
    excess = jnp.max(count_ge(thr)) > topk_f

    def plain(_):
        bias_scr[:, 0:nvis] = jnp.where(score_scr[:, 0:nvis] >= thr, 0.0, NEG_BIG)
        return 0

    def index_ordered_ties(_):
        need = topk_f - jnp.sum(jnp.where(score_scr[:, 0:nvis] > thr, 1.0, 0.0), axis=1, keepdims=True)
        r = lax.broadcasted_iota(jnp.int32, (LANES, LANES), 0)
        c = lax.broadcasted_iota(jnp.int32, (LANES, LANES), 1)
        upper = jnp.where(r < c, 1.0, 0.0).astype(BF16)
        carry = jnp.zeros((tq, 1), F32)
        for t in range(nvis // LANES):
            sl = slice(t * LANES, (t + 1) * LANES)
            s = score_scr[:, sl]
            tie_f = jnp.where(s == thr, 1.0, 0.0)
            before = _dot(tie_f.astype(BF16), upper) + carry
            carry = carry + jnp.sum(tie_f, axis=1, keepdims=True)
            tie_bias = jnp.where(s == thr, jnp.where(before < need, 0.0, NEG_BIG), NEG_BIG)
            bias_scr[:, sl] = jnp.where(s > thr, 0.0, tie_bias)
        return 0

    lax.cond(excess, index_ordered_ties, plain, 0)


def _dsa_attn_kernel(q_ref, qi_ref, wi_ref, k_ref, v_ref, ki_ref, z_ref, x_ref, w_out_ref,
                     o_ref, score_scr, bias_scr, o_scr, *, tq, n_blocks, topk, n_idx_heads, n_pair):
    qb = pl.program_id(1)
    lo = lax.broadcasted_iota(jnp.int32, (tq, LANES), 1) < 64

    def body(i):
        nvis = (i + 1) * tq
        diag_vis = _diag_visible(tq)
        if nvis <= topk:
            bias_scr[:, 0:nvis] = jnp.where(diag_vis, 0.0, NEG_BIG)
        else:
            ki = ki_ref[0, 0:nvis, :]
            wi = wi_ref[0]
            score = None
            for h in range(n_idx_heads):
                term = wi[:, h:h + 1] * jnp.maximum(_dot_nt(qi_ref[0, h], ki), 0.0)
                score = term if score is None else score + term
            if nvis > tq:
                score_scr[:, 0:nvis - tq] = score[:, 0:nvis - tq]
            score_scr[:, nvis - tq:nvis] = jnp.where(diag_vis, score[:, nvis - tq:nvis], -jnp.inf)
            _dsa_select_bias(score_scr, bias_scr, nvis, tq, topk)

        def pair_step(j, _):
            kt = k_ref[0, j, 0:nvis, :]
            vt = v_ref[0, j, 0:nvis, :]
            qq = q_ref[0, pl.ds(2 * j, 2)].reshape(2 * tq, LANES)
            logit = _dot_nt(qq, kt)
            bias = bias_scr[:, 0:nvis]
            o_even = _softmax_pv(logit[0:tq] + bias, vt)
            o_odd = _softmax_pv(logit[tq:2 * tq] + bias, vt)
            o_scr[j] = jnp.where(lo, o_even, o_odd)
            return 0

        lax.fori_loop(0, n_pair, pair_step, 0, unroll=4)
        o = jnp.concatenate([o_scr[j] for j in range(n_pair)], axis=1)
        g = (o * _silu(z_ref[0].astype(F32))).astype(BF16)
        o_ref[0] = x_ref[0] + _dot(g, w_out_ref[...])

    _per_block_variants(qb, n_blocks, body)


def _rope_inv(half):
    return ROPE_THETA ** (-jnp.arange(half, dtype=F32) / half)


def _dsa_layer(x, pos3, g, w_in, q_norm, k_norm, w_out, *, tm=256, tq=256):
    b, s, d = x.shape
    width = w_out.shape[0]
    head_dim = q_norm.shape[0]
    assert head_dim == 64 and width % LANES == 0
    n_pair = width // LANES
    idx_dim = 64
    n_idx_heads = (w_in.shape[1] - 4 * width - idx_dim) // (idx_dim + 1)
    assert 4 * width + n_idx_heads * idx_dim + idx_dim + n_idx_heads == w_in.shape[1]
    n_qi_tile = n_idx_heads * idx_dim // LANES
    c0 = 4 * width
    c1 = c0 + n_idx_heads * idx_dim
    c2 = c1 + idx_dim
    half = head_dim // 2

    def pair_split(t):
        t4 = t.reshape(t.shape[:-1] + (n_pair, 2, 2, half))
        return jnp.swapaxes(t4, -3, -2).reshape(t.shape)

    w_main = jnp.concatenate([pair_split(w_in[:, :width]), pair_split(w_in[:, width:2 * width]),
                              w_in[:, 2 * width:c0]], axis=1).astype(BF16)
    w_qi = w_in[:, c0:c1].astype(BF16)
    w_ki = w_in[:, c1:c2]
    w_wi = jnp.pad(w_in[:, c2:], ((0, 0), (0, LANES - n_idx_heads)))
    w_kiwi = jnp.concatenate([w_ki, w_ki, w_wi], axis=1).astype(BF16)

    inv32 = _rope_inv(32)
    inv16 = _rope_inv(16)
    zeros32 = jnp.zeros((32,), F32)
    inv_a = jnp.tile(inv32, 4)
    inv_b = jnp.tile(jnp.concatenate([inv16, inv16, zeros32]), 2)
    inv = jnp.stack([inv_a, inv_b])
    sgn_a = jnp.concatenate([-jnp.ones((64,), F32), jnp.ones((64,), F32)])
    sgn_b = jnp.tile(jnp.concatenate([-jnp.ones((16,), F32), jnp.ones((48,), F32)]), 2)
    sgn = jnp.stack([sgn_a, sgn_b])

    def pair_gain(gn):
        return jnp.concatenate([gn[:half], gn[:half], gn[half:], gn[half:]]).reshape(1, LANES)

    qg = pair_gain(q_norm)
    kg = pair_gain(k_norm)

    topk = min(TOPK_MAX, s // 4)
    assert s % tq == 0 and tq % (1 << CHUNK_SHIFT) == 0
    n_heads = 2 * n_pair
    row = lambda bi, i: (bi, i, 0)
    head_row = lambda bi, i: (bi, 0, i, 0)
    q, k, v, z, qi, ki, wi = pl.pallas_call(
        functools.partial(_dsa_proj_kernel, width=width, n_pair=n_pair, n_qi_tile=n_qi_tile,
                          scale=head_dim ** -0.5 * LOG2E,
                          wi_scale=n_idx_heads ** -0.5 * idx_dim ** -0.5),
        grid=(b, s // tm),
        in_specs=[
            pl.BlockSpec((1, tm, d), row),
            pl.BlockSpec((1, tm, 1), row),
            _const_spec((1, d)),
            _const_spec((d, c0)),
            _const_spec((d, c1 - c0)),
            _const_spec((d, 3 * LANES)),
            _const_spec((1, LANES)),
            _const_spec((1, LANES)),
            _const_spec((2, LANES)),
            _const_spec((2, LANES)),
        ],
        out_specs=[
            pl.BlockSpec((1, n_heads, tm, LANES), head_row),
            pl.BlockSpec((1, n_pair, tm, LANES), head_row),
            pl.BlockSpec((1, n_pair, tm, LANES), head_row),
            pl.BlockSpec((1, tm, width), row),
            pl.BlockSpec((1, n_idx_heads, tm, LANES), head_row),
            pl.BlockSpec((1, tm, LANES), row),
            pl.BlockSpec((1, tm, LANES), row),
        ],
        out_shape=[
            jax.ShapeDtypeStruct((b, n_heads, s, LANES), BF16),
            jax.ShapeDtypeStruct((b, n_pair, s, LANES), BF16),
            jax.ShapeDtypeStruct((b, n_pair, s, LANES), BF16),
            jax.ShapeDtypeStruct((b, s, width), BF16),
            jax.ShapeDtypeStruct((b, n_idx_heads, s, LANES), BF16),
            jax.ShapeDtypeStruct((b, s, LANES), BF16),
            jax.ShapeDtypeStruct((b, s, LANES), F32),
        ],
        compiler_params=_params(("arbitrary", "arbitrary")),
        name="dsa_proj",
    )(x, pos3, g.reshape(1, d), w_main, w_qi, w_kiwi, qg, kg, inv, sgn)

    qrow = lambda bi, i: (bi, i, 0)
    qhead = lambda bi, i: (bi, 0, i, 0)
    full = lambda bi, i: (bi, 0, 0, 0)
    return pl.pallas_call(
        functools.partial(_dsa_attn_kernel, tq=tq, n_blocks=s // tq, topk=topk,
                          n_idx_heads=n_idx_heads, n_pair=n_pair),
        grid=(b, s // tq),
        in_specs=[
            pl.BlockSpec((1, n_heads, tq, LANES), qhead),
            pl.BlockSpec((1, n_idx_heads, tq, LANES), qhead),
            pl.BlockSpec((1, tq, LANES), qrow),
            _single((1, n_pair, s, LANES), full),
            _single((1, n_pair, s, LANES), full),
            _single((1, s, LANES), lambda bi, i: (bi, 0, 0)),
            pl.BlockSpec((1, tq, width), qrow),
            pl.BlockSpec((1, tq, d), qrow),
            _single((width, d), lambda bi, i: (0, 0)),
        ],
        out_specs=pl.BlockSpec((1, tq, d), qrow),
        out_shape=jax.ShapeDtypeStruct((b, s, d), F32),
        scratch_shapes=[
            pltpu.VMEM((tq, s), F32),
            pltpu.VMEM((tq, s), F32),
            pltpu.VMEM((n_pair, tq, LANES), F32),
        ],
        compiler_params=_params(("arbitrary", "arbitrary")),
        name="dsa_attn",
    )(q, qi, wi, k, v, ki, z, x, w_out.astype(BF16))


def _mla_proj_kernel(x_ref, pos_ref, g_ref, w_in_ref, qlat_g_ref, kvlat_g_ref, w_uq_ref, w_uq_rot_ref,
                     w_uk_ref, w_uv_ref, gains_ref, inv_ref, sgn_ref,
                     q_ref, k_ref, v_ref, z_ref,
                     *, q_lora, kv_lora, n_heads, qk_dim, scale):
    x = x_ref[0]
    d = x.shape[-1]
    xb = (_rms(x, d) * g_ref[...]).astype(BF16)
    pos = pos_ref[0].astype(F32)
    ang = pos * inv_ref[...]
    cos_t, sin_t = jnp.cos(ang), jnp.sin(ang) * sgn_ref[...]
    cos_q = cos_t * (gains_ref[0:1, :] * scale)
    sin_q = sin_t * (gains_ref[1:2, :] * scale)
    kg = gains_ref[2:3, :]

    c1 = q_lora + kv_lora
    cq = _dot(xb, w_in_ref[:, 0:q_lora])
    ckv = _dot(xb, w_in_ref[:, q_lora:c1])
    kr = _dot(xb, w_in_ref[:, c1:c1 + LANES])
    kr_rot = _dot(xb, w_in_ref[:, c1 + LANES:c1 + 2 * LANES])
    z = _dot(xb, w_in_ref[:, c1 + 2 * LANES:])
    z_ref[0] = z.astype(BF16)

    cq_b = (_rms(cq, q_lora) * qlat_g_ref[...]).astype(BF16)
    q = _dot(cq_b, w_uq_ref[...])
    q_rot = _dot(cq_b, w_uq_rot_ref[...])
    ckv_b = (_rms(ckv, kv_lora) * kvlat_g_ref[...]).astype(BF16)
    kn = _dot(ckv_b, w_uk_ref[...])
    v = _dot(ckv_b, w_uv_ref[...])

    k_rope = kr * (kg * cos_t) + kr_rot * (gains_ref[3:4, :] * sin_t)
    kr_ss = jnp.sum(kr * kr, axis=-1, keepdims=True)
    inv_n = 1.0 / qk_dim
    for h in range(n_heads):
        sl = slice(h * LANES, (h + 1) * LANES)
        qh = q[:, sl]
        q_s = lax.rsqrt(jnp.sum(qh * qh, axis=-1, keepdims=True) * inv_n + EPS)
        q_ref[0, h] = (q_s * (qh * cos_q + q_rot[:, sl] * sin_q)).astype(BF16)
        kh = kn[:, sl]
        k_s = lax.rsqrt((jnp.sum(kh * kh, axis=-1, keepdims=True) + kr_ss) * inv_n + EPS)
        k_ref[0, h] = (k_s * (kh * kg + k_rope)).astype(BF16)
    for j in range(n_heads // 2):
        v_ref[0, j] = v[:, j * LANES:(j + 1) * LANES].astype(BF16)


def _mla_attn_kernel(q_ref, k_ref, v_ref, z_ref, x_ref, w_out_ref, o_ref, o_scr,
                     *, tq, n_blocks, n_pair):
    qb = pl.program_id(1)
    lo = lax.broadcasted_iota(jnp.int32, (tq, LANES), 1) < 64

    def body(i):
        nvis = (i + 1) * tq
        diag_bias = jnp.where(_diag_visible(tq), 0.0, NEG_BIG)

        def head_out(h, vt):
            logit = _dot_nt(q_ref[0, h], k_ref[0, h, 0:nvis, :])
            last = logit[:, nvis - tq:nvis] + diag_bias
            if nvis > tq:
                logit = jnp.concatenate([logit[:, 0:nvis - tq], last], axis=1)
            else:
                logit = last
            return _softmax_pv(logit, vt)

        def pair_step(j, _):
            vt = v_ref[0, j, 0:nvis, :]
            o_scr[j] = jnp.where(lo, head_out(2 * j, vt), head_out(2 * j + 1, vt))
            return 0

        lax.fori_loop(0, n_pair, pair_step, 0, unroll=4)
        o = jnp.concatenate([o_scr[j] for j in range(n_pair)], axis=1)
        g = (o * _silu(z_ref[0].astype(F32))).astype(BF16)
        o_ref[0] = x_ref[0] + _dot(g, w_out_ref[...])

    _per_block_variants(qb, n_blocks, body)


def _mla_layer(x, pos3, g, w_in, q_lat_norm, kv_lat_norm, w_uq, w_ukv, q_norm, k_norm, w_out,
               *, tm=256, tq=256):
    b, s, d = x.shape
    width = w_out.shape[0]
    q_lora = q_lat_norm.shape[0]
    kv_lora = kv_lat_norm.shape[0]
    qk_dim = q_norm.shape[0]
    v_dim = 64
    nope = 64
    rope_dim = qk_dim - nope
    n_heads = width // v_dim
    assert nope + rope_dim <= LANES and w_ukv.shape[1] == n_heads * (nope + v_dim)
    assert w_in.shape[1] == q_lora + kv_lora + rope_dim + width
    assert s % tq == 0 and tq % (1 << CHUNK_SHIFT) == 0
    pad = LANES - qk_dim

    rope_half = rope_dim // 2

    def rope_lanes(t):
        return jnp.pad(t, [(0, 0)] * (t.ndim - 1) + [(nope, LANES - nope - rope_dim)])

    def partner(t):
        return jnp.concatenate([t[..., rope_half:], t[..., :rope_half]], axis=-1)

    c1 = q_lora + kv_lora
    w_kr = w_in[:, c1:c1 + rope_dim]
    w_in_p = jnp.concatenate([w_in[:, :c1], rope_lanes(w_kr), rope_lanes(partner(w_kr)),
                              w_in[:, c1 + rope_dim:]], axis=1).astype(BF16)
    w_uq3 = w_uq.reshape(q_lora, n_heads, qk_dim)
    w_uq_p = jnp.pad(w_uq3, ((0, 0), (0, 0), (0, pad))).reshape(q_lora, n_heads * LANES).astype(BF16)
    w_uq_rot = rope_lanes(partner(w_uq3[:, :, nope:])).reshape(q_lora, n_heads * LANES).astype(BF16)
    w_ukv3 = w_ukv.reshape(kv_lora, n_heads, nope + v_dim)
    w_uk_p = jnp.pad(w_ukv3[:, :, :nope], ((0, 0), (0, 0), (0, LANES - nope)))
    w_uk_p = w_uk_p.reshape(kv_lora, n_heads * LANES).astype(BF16)
    w_uv = w_ukv3[:, :, nope:].reshape(kv_lora, n_heads * v_dim).astype(BF16)
    gains = jnp.stack([jnp.pad(q_norm, (0, pad)), rope_lanes(partner(q_norm[nope:])),
                       jnp.pad(k_norm, (0, pad)), rope_lanes(partner(k_norm[nope:]))])

    inv_h = _rope_inv(rope_half)
    inv = jnp.concatenate([jnp.zeros((nope,), F32), inv_h, inv_h,
                           jnp.zeros((LANES - nope - rope_dim,), F32)]).reshape(1, LANES)
    sgn = jnp.concatenate([jnp.ones((nope,), F32), -jnp.ones((rope_half,), F32),
                           jnp.ones((LANES - nope - rope_half,), F32)]).reshape(1, LANES)

    n_pair = n_heads // 2
    row = lambda bi, i: (bi, i, 0)
    head_row = lambda bi, i: (bi, 0, i, 0)
    n_in = w_in_p.shape[1]
    q, k, v, z = pl.pallas_call(
        functools.partial(_mla_proj_kernel, q_lora=q_lora, kv_lora=kv_lora, n_heads=n_heads,
                          qk_dim=qk_dim, scale=qk_dim ** -0.5 * LOG2E),
        grid=(b, s // tm),
        in_specs=[
            pl.BlockSpec((1, tm, d), row),
            pl.BlockSpec((1, tm, 1), row),
            _const_spec((1, d)),
            _const_spec((d, n_in)),
            _const_spec((1, q_lora)),
            _const_spec((1, kv_lora)),
            _const_spec((q_lora, n_heads * LANES)),
            _const_spec((q_lora, n_heads * LANES)),
            _const_spec((kv_lora, n_heads * LANES)),
            _const_spec((kv_lora, n_heads * v_dim)),
            _const_spec((4, LANES)),
            _const_spec((1, LANES)),
            _const_spec((1, LANES)),
        ],
        out_specs=[
            pl.BlockSpec((1, n_heads, tm, LANES), head_row),
            pl.BlockSpec((1, n_heads, tm, LANES), head_row),
            pl.BlockSpec((1, n_pair, tm, LANES), head_row),
            pl.BlockSpec((1, tm, width), row),
        ],
        out_shape=[
            jax.ShapeDtypeStruct((b, n_heads, s, LANES), BF16),
            jax.ShapeDtypeStruct((b, n_heads, s, LANES), BF16),
            jax.ShapeDtypeStruct((b, n_pair, s, LANES), BF16),
            jax.ShapeDtypeStruct((b, s, width), BF16),
        ],
        compiler_params=_params(("arbitrary", "arbitrary")),
        name="mla_proj",
    )(x, pos3, g.reshape(1, d), w_in_p, q_lat_norm.reshape(1, q_lora), kv_lat_norm.reshape(1, kv_lora),
      w_uq_p, w_uq_rot, w_uk_p, w_uv, gains, inv, sgn)

    qrow = lambda bi, i: (bi, i, 0)
    qhead = lambda bi, i: (bi, 0, i, 0)
    full = lambda bi, i: (bi, 0, 0, 0)
    return pl.pallas_call(
        functools.partial(_mla_attn_kernel, tq=tq, n_blocks=s // tq, n_pair=n_pair),
        grid=(b, s // tq),
        in_specs=[
            pl.BlockSpec((1, n_heads, tq, LANES), qhead),
            _single((1, n_heads, s, LANES), full),
            _single((1, n_pair, s, LANES), full),
            pl.BlockSpec((1, tq, width), qrow),
            pl.BlockSpec((1, tq, d), qrow),
            _single((width, d), lambda bi, i: (0, 0)),
        ],
        out_specs=pl.BlockSpec((1, tq, d), qrow),
        out_shape=jax.ShapeDtypeStruct((b, s, d), F32),
        scratch_shapes=[
            pltpu.VMEM((n_pair, tq, LANES), F32),
        ],
        compiler_params=_params(("arbitrary", "arbitrary")),
        name="mla_attn",
    )(q, k, v, z, x, w_out.astype(BF16))


def kernel(x, positions, a_norm, a_w_in, a_conv_w, a_conv_b, a_w_out, b_norm, b_w_in, b_q_norm, b_k_norm, b_w_out, c_norm, c_w_in, c_q_lat_norm, c_kv_lat_norm, c_w_uq, c_w_ukv, c_q_norm, c_k_norm, c_w_out):
    depth = a_norm.shape[0] + b_norm.shape[0] + c_norm.shape[0]
    pos3 = positions.reshape(positions.shape + (1,))
    for i in range(depth):
        kind, j = i % 3, i // 3
        if kind == 0:
            x = _conv_layer(x, a_norm[j], a_w_in[j], a_conv_w[j], a_conv_b[j], a_w_out[j])
        elif kind == 1:
            x = _dsa_layer(x, pos3, b_norm[j], b_w_in[j], b_q_norm[j], b_k_norm[j], b_w_out[j])
        else:
            x = _mla_layer(x, pos3, c_norm[j], c_w_in[j], c_q_lat_norm[j], c_kv_lat_norm[j],
                           c_w_uq[j], c_w_ukv[j], c_q_norm[j], c_k_norm[j], c_w_out[j])
    return x
```

```python
import functools
import math

import jax
import jax.numpy as jnp
from jax import lax
from jax.experimental import pallas as pl
from jax.experimental.pallas import tpu as pltpu

EPS = 1e-6
ROPE_THETA = 10000.0
CHUNK_SHIFT = 6
TOPK_MAX = 256
LANES = 128
NEG_BIG = -1e30
INT_MIN = -(2 ** 31)
F32_MAX = 3.4028234663852886e38
LOG2E = math.log2(math.e)
VMEM_LIMIT = 56 * 1024 * 1024

BF16 = jnp.bfloat16
F32 = jnp.float32


def _dot(a, b):
    return jnp.dot(a, b, preferred_element_type=F32)


def _dot_nt(a, b):
    return lax.dot_general(a, b, (((1,), (1,)), ((), ())), preferred_element_type=F32)


def _rms(x, n):
    return x * lax.rsqrt(jnp.sum(x * x, axis=-1, keepdims=True) * (1.0 / n) + EPS)


def _silu(z):
    return z * jax.nn.sigmoid(z)


def _rope_tile(t, cos_t, sin_t, first_mask, half):
    partner = jnp.where(first_mask, pltpu.roll(t, LANES - half, 1), pltpu.roll(t, half, 1))
    return t * cos_t + partner * sin_t


def _params(sem):
    return pltpu.CompilerParams(dimension_semantics=sem, vmem_limit_bytes=VMEM_LIMIT)


def _const_spec(shape):
    nd = len(shape)
    return pl.BlockSpec(shape, lambda *_: (0,) * nd)


def _single(shape, index_map):
    return pl.BlockSpec(shape, index_map, pipeline_mode=pl.Buffered(1))


def _diag_visible(tq):
    r = lax.broadcasted_iota(jnp.int32, (tq, tq), 0)
    c = lax.broadcasted_iota(jnp.int32, (tq, tq), 1)
    return (c >> CHUNK_SHIFT) <= (r >> CHUNK_SHIFT)


def _softmax_pv(logit, v):
    p = jnp.exp2(logit - jnp.max(logit, axis=-1, keepdims=True))
    den = jnp.sum(p, axis=-1, keepdims=True)
    return _dot(p.astype(BF16), v) / den


def _per_block_variants(qb, n_blocks, body):
    for i in range(n_blocks):
        pl.when(qb == i)(functools.partial(body, i))


def _conv_kernel(x_ref, g_ref, w_in_ref, cw_ref, cb_ref, w_out_ref, o_ref, u_scr, *, tm, width):
    j = pl.program_id(1)
    x = x_ref[0]
    d = x.shape[-1]
    xb = (_rms(x, d) * g_ref[...]).astype(BF16)
    bg = _dot(xb, w_in_ref[:, 0 * width:1 * width])
    cg = _dot(xb, w_in_ref[:, 1 * width:2 * width])
    hv = _dot(xb, w_in_ref[:, 2 * width:3 * width])
    z = _dot(xb, w_in_ref[:, 3 * width:4 * width])
    u = cg * hv

    @pl.when(j == 0)
    def _():
        u_scr[0:8, :] = jnp.zeros((8, width), F32)

    @pl.when(j > 0)
    def _():
        u_scr[0:8, :] = u_scr[tm:tm + 8, :]

    u_scr[8:tm + 8, :] = u
    y = (cw_ref[2:3, :] * u + cw_ref[1:2, :] * u_scr[7:7 + tm, :]
         + cw_ref[0:1, :] * u_scr[6:6 + tm, :] + cb_ref[...])
    g = (bg * y * _silu(z)).astype(BF16)
    o_ref[0] = x + _dot(g, w_out_ref[...])


def _conv_layer(x, g, w_in, cw, cb, w_out, *, tm=256):
    b, s, d = x.shape
    width = w_out.shape[0]
    return pl.pallas_call(
        functools.partial(_conv_kernel, tm=tm, width=width),
        grid=(b, s // tm),
        in_specs=[
            pl.BlockSpec((1, tm, d), lambda bi, i: (bi, i, 0)),
            _const_spec((1, d)),
            _const_spec((d, 4 * width)),
            _const_spec((3, width)),
            _const_spec((1, width)),
            _const_spec((width, d)),
        ],
        out_specs=pl.BlockSpec((1, tm, d), lambda bi, i: (bi, i, 0)),
        out_shape=jax.ShapeDtypeStruct((b, s, d), F32),
        scratch_shapes=[pltpu.VMEM((tm + 8, width), F32)],
        compiler_params=_params(("arbitrary", "arbitrary")),
        name="conv_mixer",
    )(x, g.reshape(1, d), w_in.astype(BF16), cw, cb.reshape(1, width), w_out.astype(BF16))


def _dsa_proj_kernel(x_ref, pos_ref, g_ref, w_main_ref, w_qi_ref, w_kiwi_ref, qg_ref, kg_ref,
                     inv_ref, sgn_ref,
                     q_ref, k_ref, v_ref, z_ref, qi_ref, ki_ref, wi_ref,
                     *, width, n_pair, n_qi_tile, scale, wi_scale):
    x = x_ref[0]
    d = x.shape[-1]
    tm = x.shape[0]
    xb = (_rms(x, d) * g_ref[...]).astype(BF16)
    pos = pos_ref[0].astype(F32)
    ang_a = pos * inv_ref[0:1, :]
    ang_b = pos * inv_ref[1:2, :]
    cos_a, sin_a = jnp.cos(ang_a), jnp.sin(ang_a) * sgn_ref[0:1, :]
    cos_b, sin_b = jnp.cos(ang_b), jnp.sin(ang_b) * sgn_ref[1:2, :]
    lane = lax.broadcasted_iota(jnp.int32, (tm, LANES), 1)
    lo = lane < 64
    first_b = (lane & 63) < 16
    head0 = (lane & 32) == 0

    q = _dot(xb, w_main_ref[:, 0 * width:1 * width])
    k = _dot(xb, w_main_ref[:, 1 * width:2 * width])
    v = _dot(xb, w_main_ref[:, 2 * width:3 * width])
    z = _dot(xb, w_main_ref[:, 3 * width:4 * width])
    z_ref[0] = z.astype(BF16)

    def head_norm_rope(t, gain):
        sq = t * t
        s0 = jnp.sum(jnp.where(head0, sq, 0.0), axis=-1, keepdims=True)
        s1 = jnp.sum(jnp.where(head0, 0.0, sq), axis=-1, keepdims=True)
        ms = jnp.where(head0, s0, s1) * (1.0 / 64)
        t = t * lax.rsqrt(ms + EPS) * gain
        return t * cos_a + pltpu.roll(t, 64, 1) * sin_a

    for j in range(n_pair):
        sl = slice(j * LANES, (j + 1) * LANES)
        qt = head_norm_rope(q[:, sl], qg_ref[...]) * scale
        q_ref[0, 2 * j] = jnp.where(head0, qt, 0.0).astype(BF16)
        q_ref[0, 2 * j + 1] = jnp.where(head0, 0.0, qt).astype(BF16)
        k_ref[0, j] = head_norm_rope(k[:, sl], kg_ref[...]).astype(BF16)
        v_ref[0, j] = v[:, sl].astype(BF16)

    qi = _dot(xb, w_qi_ref[...])
    for j in range(n_qi_tile):
        t = _rope_tile(qi[:, j * LANES:(j + 1) * LANES], cos_b, sin_b, first_b, 16)
        qi_ref[0, 2 * j] = jnp.where(lo, t, 0.0).astype(BF16)
        qi_ref[0, 2 * j + 1] = jnp.where(lo, 0.0, t).astype(BF16)

    kiwi = _dot(xb, w_kiwi_ref[...])
    ki_ref[0] = _rope_tile(kiwi[:, 0:LANES], cos_b, sin_b, first_b, 16).astype(BF16)
    wi_ref[0] = kiwi[:, LANES:2 * LANES] * wi_scale


def _key_to_float(key):
    return pltpu.bitcast(jnp.where(key >= 0, key, key ^ 0x7FFFFFFF), F32)


def _dsa_select_bias(score_scr, bias_scr, nvis, tq, topk):
    topk_f = float(topk)

    def count_ge(cand):
        return jnp.sum(jnp.where(score_scr[:, 0:nvis] >= cand, 1.0, 0.0), axis=1, keepdims=True)

    base = jnp.where(count_ge(jnp.zeros((tq, 1), F32)) >= topk_f, 0, INT_MIN).astype(jnp.int32)

    def bit_step(i, base):
        cand = base | jnp.left_shift(jnp.int32(1), 30 - i)
        return jnp.where(count_ge(_key_to_float(cand)) >= topk_f, cand, base)

    thr = _key_to_float(lax.fori_loop(0, 31, bit_step, base))
    lowest = jnp.full((tq, 1), -F32_MAX, F32)
    thr = jnp.where(count_ge(lowest) < topk_f, lowest, thr)
    excess = jnp.max(count_ge(thr)) > topk_f

    def plain(_):
        bias_scr[:, 0:nvis] = jnp.where(score_scr[:, 0:nvis] >= thr, 0.0, NEG_BIG)
        return 0

    def index_ordered_ties(_):
        need = topk_f - jnp.sum(jnp.where(score_scr[:, 0:nvis] > thr, 1.0, 0.0), axis=1, keepdims=True)
        r = lax.broadcasted_iota(jnp.int32, (LANES, LANES), 0)
        c = lax.broadcasted_iota(jnp.int32, (LANES, LANES), 1)
        upper = jnp.where(r < c, 1.0, 0.0).astype(BF16)
        carry = jnp.zeros((tq, 1), F32)
        for t in range(nvis // LANES):
            sl = slice(t * LANES, (t + 1) * LANES)
            s = score_scr[:, sl]
            tie_f = jnp.where(s == thr, 1.0, 0.0)
            before = _dot(tie_f.astype(BF16), upper) + carry
            carry = carry + jnp.sum(tie_f, axis=1, keepdims=True)
            tie_bias = jnp.where(s == thr, jnp.where(before < need, 0.0, NEG_BIG), NEG_BIG)
            bias_scr[:, sl] = jnp.where(s > thr, 0.0, tie_bias)
        return 0

    lax.cond(excess, index_ordered_ties, plain, 0)


def _dsa_attn_kernel(q_ref, qi_ref, wi_ref, k_ref, v_ref, ki_ref, z_ref, x_ref, w_out_ref,
                     o_ref, score_scr, bias_scr, o_scr, *, tq, n_blocks, topk, n_idx_heads, n_pair):
    qb = pl.program_id(0)
    lo = lax.broadcasted_iota(jnp.int32, (tq, LANES), 1) < 64

    def body(i):
        nvis = (i + 1) * tq
        diag_vis = _diag_visible(tq)
        if nvis <= topk:
            bias_scr[:, 0:nvis] = jnp.where(diag_vis, 0.0, NEG_BIG)
        else:
            ki = ki_ref[0, 0:nvis, :]
            wi = wi_ref[0]
            score = None
            for h in range(n_idx_heads):
                term = wi[:, h:h + 1] * jnp.maximum(_dot_nt(qi_ref[0, h], ki), 0.0)
                score = term if score is None else score + term
            if nvis > tq:
                score_scr[:, 0:nvis - tq] = score[:, 0:nvis - tq]
            score_scr[:, nvis - tq:nvis] = jnp.where(diag_vis, score[:, nvis - tq:nvis], -jnp.inf)
            _dsa_select_bias(score_scr, bias_scr, nvis, tq, topk)

        def pair_step(j, _):
            kt = k_ref[0, j, 0:nvis, :]
            vt = v_ref[0, j, 0:nvis, :]
            qq = q_ref[0, pl.ds(2 * j, 2)].reshape(2 * tq, LANES)
            logit = _dot_nt(qq, kt)
            bias = bias_scr[:, 0:nvis]
            o_even = _softmax_pv(logit[0:tq] + bias, vt)
            o_odd = _softmax_pv(logit[tq:2 * tq] + bias, vt)
            o_scr[j] = jnp.where(lo, o_even, o_odd)
            return 0

        lax.fori_loop(0, n_pair, pair_step, 0, unroll=4)
        o = jnp.concatenate([o_scr[j] for j in range(n_pair)], axis=1)
        g = (o * _silu(z_ref[0].astype(F32))).astype(BF16)
        o_ref[0] = x_ref[0] + _dot(g, w_out_ref[...])

    _per_block_variants(qb, n_blocks, body)


def _rope_inv(half):
    return ROPE_THETA ** (-jnp.arange(half, dtype=F32) / half)


def _dsa_layer(x, pos3, g, w_in, q_norm, k_norm, w_out, *, tm=256, tq=256):
    b, s, d = x.shape
    width = w_out.shape[0]
    head_dim = q_norm.shape[0]
    assert head_dim == 64 and width % LANES == 0
    n_pair = width // LANES
    idx_dim = 64
    n_idx_heads = (w_in.shape[1] - 4 * width - idx_dim) // (idx_dim + 1)
    assert 4 * width + n_idx_heads * idx_dim + idx_dim + n_idx_heads == w_in.shape[1]
    n_qi_tile = n_idx_heads * idx_dim // LANES
    c0 = 4 * width
    c1 = c0 + n_idx_heads * idx_dim
    c2 = c1 + idx_dim
    half = head_dim // 2

    def pair_split(t):
        t4 = t.reshape(t.shape[:-1] + (n_pair, 2, 2, half))
        return jnp.swapaxes(t4, -3, -2).reshape(t.shape)

    w_main = jnp.concatenate([pair_split(w_in[:, :width]), pair_split(w_in[:, width:2 * width]),
                              w_in[:, 2 * width:c0]], axis=1).astype(BF16)
    w_qi = w_in[:, c0:c1].astype(BF16)
    w_ki = w_in[:, c1:c2]
    w_wi = jnp.pad(w_in[:, c2:], ((0, 0), (0, LANES - n_idx_heads)))
    w_kiwi = jnp.concatenate([w_ki, w_ki, w_wi], axis=1).astype(BF16)

    inv32 = _rope_inv(32)
    inv16 = _rope_inv(16)
    zeros32 = jnp.zeros((32,), F32)
    inv_a = jnp.tile(inv32, 4)
    inv_b = jnp.tile(jnp.concatenate([inv16, inv16, zeros32]), 2)
    inv = jnp.stack([inv_a, inv_b])
    sgn_a = jnp.concatenate([-jnp.ones((64,), F32), jnp.ones((64,), F32)])
    sgn_b = jnp.tile(jnp.concatenate([-jnp.ones((16,), F32), jnp.ones((48,), F32)]), 2)
    sgn = jnp.stack([sgn_a, sgn_b])

    def pair_gain(gn):
        return jnp.concatenate([gn[:half], gn[:half], gn[half:], gn[half:]]).reshape(1, LANES)

    qg = pair_gain(q_norm)
    kg = pair_gain(k_norm)

    topk = min(TOPK_MAX, s // 4)
    assert s % tq == 0 and tq % (1 << CHUNK_SHIFT) == 0
    n_heads = 2 * n_pair
    row = lambda bi, i: (bi, i, 0)
    head_row = lambda bi, i: (bi, 0, i, 0)
    q, k, v, z, qi, ki, wi = pl.pallas_call(
        functools.partial(_dsa_proj_kernel, width=width, n_pair=n_pair, n_qi_tile=n_qi_tile,
                          scale=head_dim ** -0.5 * LOG2E,
                          wi_scale=n_idx_heads ** -0.5 * idx_dim ** -0.5),
        grid=(b, s // tm),
        in_specs=[
            pl.BlockSpec((1, tm, d), row),
            pl.BlockSpec((1, tm, 1), row),
            _const_spec((1, d)),
            _const_spec((d, c0)),
            _const_spec((d, c1 - c0)),
            _const_spec((d, 3 * LANES)),
            _const_spec((1, LANES)),
            _const_spec((1, LANES)),
            _const_spec((2, LANES)),
            _const_spec((2, LANES)),
        ],
        out_specs=[
            pl.BlockSpec((1, n_heads, tm, LANES), head_row),
            pl.BlockSpec((1, n_pair, tm, LANES), head_row),
            pl.BlockSpec((1, n_pair, tm, LANES), head_row),
            pl.BlockSpec((1, tm, width), row),
            pl.BlockSpec((1, n_idx_heads, tm, LANES), head_row),
            pl.BlockSpec((1, tm, LANES), row),
            pl.BlockSpec((1, tm, LANES), row),
        ],
        out_shape=[
            jax.ShapeDtypeStruct((b, n_heads, s, LANES), BF16),
            jax.ShapeDtypeStruct((b, n_pair, s, LANES), BF16),
            jax.ShapeDtypeStruct((b, n_pair, s, LANES), BF16),
            jax.ShapeDtypeStruct((b, s, width), BF16),
            jax.ShapeDtypeStruct((b, n_idx_heads, s, LANES), BF16),
            jax.ShapeDtypeStruct((b, s, LANES), BF16),
            jax.ShapeDtypeStruct((b, s, LANES), F32),
        ],
        compiler_params=_params(("arbitrary", "arbitrary")),
        name="dsa_proj",
    )(x, pos3, g.reshape(1, d), w_main, w_qi, w_kiwi, qg, kg, inv, sgn)

    qrow = lambda i, bi: (bi, i, 0)
    qhead = lambda i, bi: (bi, 0, i, 0)
    full = lambda i, bi: (bi, 0, 0, 0)
    return pl.pallas_call(
        functools.partial(_dsa_attn_kernel, tq=tq, n_blocks=s // tq, topk=topk,
                          n_idx_heads=n_idx_heads, n_pair=n_pair),
        grid=(s // tq, b),
        in_specs=[
            pl.BlockSpec((1, n_heads, tq, LANES), qhead),
            pl.BlockSpec((1, n_idx_heads, tq, LANES), qhead),
            pl.BlockSpec((1, tq, LANES), qrow),
            pl.BlockSpec((1, n_pair, s, LANES), full),
            pl.BlockSpec((1, n_pair, s, LANES), full),
            pl.BlockSpec((1, s, LANES), lambda i, bi: (bi, 0, 0)),
            pl.BlockSpec((1, tq, width), qrow),
            pl.BlockSpec((1, tq, d), qrow),
            _single((width, d), lambda i, bi: (0, 0)),
        ],
        out_specs=pl.BlockSpec((1, tq, d), qrow),
        out_shape=jax.ShapeDtypeStruct((b, s, d), F32),
        scratch_shapes=[
            pltpu.VMEM((tq, s), F32),
            pltpu.VMEM((tq, s), F32),
            pltpu.VMEM((n_pair, tq, LANES), F32),
        ],
        compiler_params=_params(("arbitrary", "arbitrary")),
        name="dsa_attn",
    )(q, qi, wi, k, v, ki, z, x, w_out.astype(BF16))


def _mla_proj_kernel(x_ref, pos_ref, g_ref, w_in_ref, qlat_g_ref, kvlat_g_ref, w_uq_ref, w_uq_rot_ref,
                     w_uk_ref, w_uv_ref, gains_ref, inv_ref, sgn_ref,
                     q_ref, k_ref, v_ref, z_ref,
                     *, q_lora, kv_lora, n_heads, qk_dim, scale):
    x = x_ref[0]
    d = x.shape[-1]
    xb = (_rms(x, d) * g_ref[...]).astype(BF16)
    pos = pos_ref[0].astype(F32)
    ang = pos * inv_ref[...]
    cos_t, sin_t = jnp.cos(ang), jnp.sin(ang) * sgn_ref[...]
    cos_q = cos_t * (gains_ref[0:1, :] * scale)
    sin_q = sin_t * (gains_ref[1:2, :] * scale)
    kg = gains_ref[2:3, :]

    c1 = q_lora + kv_lora
    cq = _dot(xb, w_in_ref[:, 0:q_lora])
    ckv = _dot(xb, w_in_ref[:, q_lora:c1])
    kr = _dot(xb, w_in_ref[:, c1:c1 + LANES])
    kr_rot = _dot(xb, w_in_ref[:, c1 + LANES:c1 + 2 * LANES])
    z = _dot(xb, w_in_ref[:, c1 + 2 * LANES:])
    z_ref[0] = z.astype(BF16)

    cq_b = (_rms(cq, q_lora) * qlat_g_ref[...]).astype(BF16)
    q = _dot(cq_b, w_uq_ref[...])
    q_rot = _dot(cq_b, w_uq_rot_ref[...])
    ckv_b = (_rms(ckv, kv_lora) * kvlat_g_ref[...]).astype(BF16)
    kn = _dot(ckv_b, w_uk_ref[...])
    v = _dot(ckv_b, w_uv_ref[...])

    k_rope = kr * (kg * cos_t) + kr_rot * (gains_ref[3:4, :] * sin_t)
    kr_ss = jnp.sum(kr * kr, axis=-1, keepdims=True)
    inv_n = 1.0 / qk_dim
    for h in range(n_heads):
        sl = slice(h * LANES, (h + 1) * LANES)
        qh = q[:, sl]
        q_s = lax.rsqrt(jnp.sum(qh * qh, axis=-1, keepdims=True) * inv_n + EPS)
        q_ref[0, h] = (q_s * (qh * cos_q + q_rot[:, sl] * sin_q)).astype(BF16)
        kh = kn[:, sl]
        k_s = lax.rsqrt((jnp.sum(kh * kh, axis=-1, keepdims=True) + kr_ss) * inv_n + EPS)
        k_ref[0, h] = (k_s * (kh * kg + k_rope)).astype(BF16)
    for j in range(n_heads // 2):
        v_ref[0, j] = v[:, j * LANES:(j + 1) * LANES].astype(BF16)


def _mla_attn_kernel(q_ref, k_ref, v_ref, z_ref, x_ref, w_out_ref, o_ref, o_scr,
                     *, tq, n_blocks, n_pair):
    qb = pl.program_id(0)
    lo = lax.broadcasted_iota(jnp.int32, (tq, LANES), 1) < 64

    def body(i):
        nvis = (i + 1) * tq
        diag_bias = jnp.where(_diag_visible(tq), 0.0, NEG_BIG)

        def head_out(h, vt):
            logit = _dot_nt(q_ref[0, h], k_ref[0, h, 0:nvis, :])
            last = logit[:, nvis - tq:nvis] + diag_bias
            if nvis > tq:
                logit = jnp.concatenate([logit[:, 0:nvis - tq], last], axis=1)
            else:
                logit = last
            return _softmax_pv(logit, vt)

        def pair_step(j, _):
            vt = v_ref[0, j, 0:nvis, :]
            o_scr[j] = jnp.where(lo, head_out(2 * j, vt), head_out(2 * j + 1, vt))
            return 0

        lax.fori_loop(0, n_pair, pair_step, 0, unroll=4)
        o = jnp.concatenate([o_scr[j] for j in range(n_pair)], axis=1)
        g = (o * _silu(z_ref[0].astype(F32))).astype(BF16)
        o_ref[0] = x_ref[0] + _dot(g, w_out_ref[...])

    _per_block_variants(qb, n_blocks, body)


def _mla_layer(x, pos3, g, w_in, q_lat_norm, kv_lat_norm, w_uq, w_ukv, q_norm, k_norm, w_out,
               *, tm=256, tq=256):
    b, s, d = x.shape
    width = w_out.shape[0]
    q_lora = q_lat_norm.shape[0]
    kv_lora = kv_lat_norm.shape[0]
    qk_dim = q_norm.shape[0]
    v_dim = 64
    nope = 64
    rope_dim = qk_dim - nope
    n_heads = width // v_dim
    assert nope + rope_dim <= LANES and w_ukv.shape[1] == n_heads * (nope + v_dim)
    assert w_in.shape[1] == q_lora + kv_lora + rope_dim + width
    assert s % tq == 0 and tq % (1 << CHUNK_SHIFT) == 0
    pad = LANES - qk_dim

    rope_half = rope_dim // 2

    def rope_lanes(t):
        return jnp.pad(t, [(0, 0)] * (t.ndim - 1) + [(nope, LANES - nope - rope_dim)])

    def partner(t):
        return jnp.concatenate([t[..., rope_half:], t[..., :rope_half]], axis=-1)

    c1 = q_lora + kv_lora
    w_kr = w_in[:, c1:c1 + rope_dim]
    w_in_p = jnp.concatenate([w_in[:, :c1], rope_lanes(w_kr), rope_lanes(partner(w_kr)),
                              w_in[:, c1 + rope_dim:]], axis=1).astype(BF16)
    w_uq3 = w_uq.reshape(q_lora, n_heads, qk_dim)
    w_uq_p = jnp.pad(w_uq3, ((0, 0), (0, 0), (0, pad))).reshape(q_lora, n_heads * LANES).astype(BF16)
    w_uq_rot = rope_lanes(partner(w_uq3[:, :, nope:])).reshape(q_lora, n_heads * LANES).astype(BF16)
    w_ukv3 = w_ukv.reshape(kv_lora, n_heads, nope + v_dim)
    w_uk_p = jnp.pad(w_ukv3[:, :, :nope], ((0, 0), (0, 0), (0, LANES - nope)))
    w_uk_p = w_uk_p.reshape(kv_lora, n_heads * LANES).astype(BF16)
    w_uv = w_ukv3[:, :, nope:].reshape(kv_lora, n_heads * v_dim).astype(BF16)
    gains = jnp.stack([jnp.pad(q_norm, (0, pad)), rope_lanes(partner(q_norm[nope:])),
                       jnp.pad(k_norm, (0, pad)), rope_lanes(partner(k_norm[nope:]))])

    inv_h = _rope_inv(rope_half)
    inv = jnp.concatenate([jnp.zeros((nope,), F32), inv_h, inv_h,
                           jnp.zeros((LANES - nope - rope_dim,), F32)]).reshape(1, LANES)
    sgn = jnp.concatenate([jnp.ones((nope,), F32), -jnp.ones((rope_half,), F32),
                           jnp.ones((LANES - nope - rope_half,), F32)]).reshape(1, LANES)

    n_pair = n_heads // 2
    row = lambda bi, i: (bi, i, 0)
    head_row = lambda bi, i: (bi, 0, i, 0)
    n_in = w_in_p.shape[1]
    q, k, v, z = pl.pallas_call(
        functools.partial(_mla_proj_kernel, q_lora=q_lora, kv_lora=kv_lora, n_heads=n_heads,
                          qk_dim=qk_dim, scale=qk_dim ** -0.5 * LOG2E),
        grid=(b, s // tm),
        in_specs=[
            pl.BlockSpec((1, tm, d), row),
            pl.BlockSpec((1, tm, 1), row),
            _const_spec((1, d)),
            _const_spec((d, n_in)),
            _const_spec((1, q_lora)),
            _const_spec((1, kv_lora)),
            _const_spec((q_lora, n_heads * LANES)),
            _const_spec((q_lora, n_heads * LANES)),
            _const_spec((kv_lora, n_heads * LANES)),
            _const_spec((kv_lora, n_heads * v_dim)),
            _const_spec((4, LANES)),
            _const_spec((1, LANES)),
            _const_spec((1, LANES)),
        ],
        out_specs=[
            pl.BlockSpec((1, n_heads, tm, LANES), head_row),
            pl.BlockSpec((1, n_heads, tm, LANES), head_row),
            pl.BlockSpec((1, n_pair, tm, LANES), head_row),
            pl.BlockSpec((1, tm, width), row),
        ],
        out_shape=[
            jax.ShapeDtypeStruct((b, n_heads, s, LANES), BF16),
            jax.ShapeDtypeStruct((b, n_heads, s, LANES), BF16),
            jax.ShapeDtypeStruct((b, n_pair, s, LANES), BF16),
            jax.ShapeDtypeStruct((b, s, width), BF16),
        ],
        compiler_params=_params(("arbitrary", "arbitrary")),
        name="mla_proj",
    )(x, pos3, g.reshape(1, d), w_in_p, q_lat_norm.reshape(1, q_lora), kv_lat_norm.reshape(1, kv_lora),
      w_uq_p, w_uq_rot, w_uk_p, w_uv, gains, inv, sgn)

    qrow = lambda i, bi: (bi, i, 0)
    qhead = lambda i, bi: (bi, 0, i, 0)
    full = lambda i, bi: (bi, 0, 0, 0)
    return pl.pallas_call(
        functools.partial(_mla_attn_kernel, tq=tq, n_blocks=s // tq, n_pair=n_pair),
        grid=(s // tq, b),
        in_specs=[
            pl.BlockSpec((1, n_heads, tq, LANES), qhead),
            pl.BlockSpec((1, n_heads, s, LANES), full),
            pl.BlockSpec((1, n_pair, s, LANES), full),
            pl.BlockSpec((1, tq, width), qrow),
            pl.BlockSpec((1, tq, d), qrow),
            _single((width, d), lambda i, bi: (0, 0)),
        ],
        out_specs=pl.BlockSpec((1, tq, d), qrow),
        out_shape=jax.ShapeDtypeStruct((b, s, d), F32),
        scratch_shapes=[
            pltpu.VMEM((n_pair, tq, LANES), F32),
        ],
        compiler_params=_params(("arbitrary", "arbitrary")),
        name="mla_attn",
    )(q, k, v, z, x, w_out.astype(BF16))


def kernel(x, positions, a_norm, a_w_in, a_conv_w, a_conv_b, a_w_out, b_norm, b_w_in, b_q_norm, b_k_norm, b_w_out, c_norm, c_w_in, c_q_lat_norm, c_kv_lat_norm, c_w_uq, c_w_ukv, c_q_norm, c_k_norm, c_w_out):
    depth = a_norm.shape[0] + b_norm.shape[0] + c_norm.shape[0]
    pos3 = positions.reshape(positions.shape + (1,))
    for i in range(depth):
        kind, j = i % 3, i // 3
        if kind == 0:
            x = _conv_layer(x, a_norm[j], a_w_in[j], a_conv_w[j], a_conv_b[j], a_w_out[j])
        elif kind == 1:
            x = _dsa_layer(x, pos3, b_norm[j], b_w_in[j], b_q_norm[j], b_k_norm[j], b_w_out[j])
        else:
            x = _mla_layer(x, pos3, c_norm[j], c_w_in[j], c_q_lat_norm[j], c_kv_lat_norm[j],
                           c_w_uq[j], c_w_ukv[j], c_q_norm[j], c_k_norm[j], c_w_out[j])
    return x
```

```python
import functools
import math

import jax
import jax.numpy as jnp
from jax import lax
from jax.experimental import pallas as pl
from jax.experimental.pallas import tpu as pltpu

EPS = 1e-6
ROPE_THETA = 10000.0
CHUNK_SHIFT = 6
TOPK_MAX = 256
LANES = 128
NEG_BIG = -1e30
INT_MIN = -(2 ** 31)
F32_MAX = 3.4028234663852886e38
LOG2E = math.log2(math.e)
VMEM_LIMIT = 56 * 1024 * 1024

BF16 = jnp.bfloat16
F32 = jnp.float32


def _dot(a, b):
    return jnp.dot(a, b, preferred_element_type=F32)


def _dot_nt(a, b):
    return lax.dot_general(a, b, (((1,), (1,)), ((), ())), preferred_element_type=F32)


def _rms(x, n):
    return x * lax.rsqrt(jnp.sum(x * x, axis=-1, keepdims=True) * (1.0 / n) + EPS)


def _silu(z):
    return z * jax.nn.sigmoid(z)


def _rope_tile(t, cos_t, sin_t, first_mask, half):
    partner = jnp.where(first_mask, pltpu.roll(t, LANES - half, 1), pltpu.roll(t, half, 1))
    return t * cos_t + partner * sin_t


def _params(sem):
    return pltpu.CompilerParams(dimension_semantics=sem, vmem_limit_bytes=VMEM_LIMIT)


def _const_spec(shape):
    nd = len(shape)
    return pl.BlockSpec(shape, lambda *_: (0,) * nd)


def _single(shape, index_map):
    return pl.BlockSpec(shape, index_map, pipeline_mode=pl.Buffered(1))


def _diag_visible(tq):
    r = lax.broadcasted_iota(jnp.int32, (tq, tq), 0)
    c = lax.broadcasted_iota(jnp.int32, (tq, tq), 1)
    return (c >> CHUNK_SHIFT) <= (r >> CHUNK_SHIFT)


def _softmax_pv(logit, v):
    p = jnp.exp2(logit - jnp.max(logit, axis=-1, keepdims=True))
    den = jnp.sum(p, axis=-1, keepdims=True)
    return _dot(p.astype(BF16), v) / den


def _per_block_variants(qb, n_blocks, body):
    for i in range(n_blocks):
        pl.when(qb == i)(functools.partial(body, i))


def _conv_kernel(x_ref, g_ref, w_in_ref, cw_ref, cb_ref, w_out_ref, o_ref, u_scr, *, tm, width):
    j = pl.program_id(1)
    x = x_ref[0]
    d = x.shape[-1]
    xb = (_rms(x, d) * g_ref[...]).astype(BF16)

    @pl.when(j == 0)
    def _():
        u_scr[0:8, :] = jnp.zeros((8, width), F32)

    @pl.when(j > 0)
    def _():
        u_scr[0:8, :] = u_scr[tm:tm + 8, :]

    bg = _dot(xb, w_in_ref[:, 0 * width:1 * width])
    cg = _dot(xb, w_in_ref[:, 1 * width:2 * width])
    hv = _dot(xb, w_in_ref[:, 2 * width:3 * width])
    z = _dot(xb, w_in_ref[:, 3 * width:4 * width])
    u = cg * hv
    u_scr[8:tm + 8, :] = u
    y = (cw_ref[2:3, :] * u + cw_ref[1:2, :] * u_scr[7:7 + tm, :]
         + cw_ref[0:1, :] * u_scr[6:6 + tm, :] + cb_ref[...])
    g = (bg * y * _silu(z)).astype(BF16)
    o_ref[0] = x + _dot(g, w_out_ref[...])


def _conv_layer(x, g, w_in, cw, cb, w_out, *, tm=256):
    b, s, d = x.shape
    width = w_out.shape[0]
    return pl.pallas_call(
        functools.partial(_conv_kernel, tm=tm, width=width),
        grid=(b, s // tm),
        in_specs=[
            pl.BlockSpec((1, tm, d), lambda bi, i: (bi, i, 0)),
            _const_spec((1, d)),
            _const_spec((d, 4 * width)),
            _const_spec((3, width)),
            _const_spec((1, width)),
            _const_spec((width, d)),
        ],
        out_specs=pl.BlockSpec((1, tm, d), lambda bi, i: (bi, i, 0)),
        out_shape=jax.ShapeDtypeStruct((b, s, d), F32),
        scratch_shapes=[pltpu.VMEM((tm + 8, width), F32)],
        compiler_params=_params(("arbitrary", "arbitrary")),
        name="conv_mixer",
    )(x, g.reshape(1, d), w_in.astype(BF16), cw, cb.reshape(1, width), w_out.astype(BF16))


def _dsa_proj_kernel(x_ref, pos_ref, g_ref, w_main_ref, w_qi_ref, w_kiwi_ref, qg_ref, kg_ref,
                     inv_ref, sgn_ref,
                     q_ref, k_ref, v_ref, z_ref, qi_ref, ki_ref, wi_ref,
                     *, width, n_pair, n_qi_tile, scale, wi_scale):
    x = x_ref[0]
    d = x.shape[-1]
    tm = x.shape[0]
    xb = (_rms(x, d) * g_ref[...]).astype(BF16)
    pos = pos_ref[0].astype(F32)
    ang_a = pos * inv_ref[0:1, :]
    ang_b = pos * inv_ref[1:2, :]
    cos_a, sin_a = jnp.cos(ang_a), jnp.sin(ang_a) * sgn_ref[0:1, :]
    cos_b, sin_b = jnp.cos(ang_b), jnp.sin(ang_b) * sgn_ref[1:2, :]
    lane = lax.broadcasted_iota(jnp.int32, (tm, LANES), 1)
    lo = lane < 64
    first_b = (lane & 63) < 16
    head0 = (lane & 32) == 0

    q = _dot(xb, w_main_ref[:, 0 * width:1 * width])
    k = _dot(xb, w_main_ref[:, 1 * width:2 * width])
    v = _dot(xb, w_main_ref[:, 2 * width:3 * width])
    z = _dot(xb, w_main_ref[:, 3 * width:4 * width])
    z_ref[0] = z.astype(BF16)

    def head_norm_rope(t, gain):
        sq = t * t
        s0 = jnp.sum(jnp.where(head0, sq, 0.0), axis=-1, keepdims=True)
        s1 = jnp.sum(jnp.where(head0, 0.0, sq), axis=-1, keepdims=True)
        ms = jnp.where(head0, s0, s1) * (1.0 / 64)
        t = t * lax.rsqrt(ms + EPS) * gain
        return t * cos_a + pltpu.roll(t, 64, 1) * sin_a

    for j in range(n_pair):
        sl = slice(j * LANES, (j + 1) * LANES)
        qt = head_norm_rope(q[:, sl], qg_ref[...]) * scale
        q_ref[0, 2 * j] = jnp.where(head0, qt, 0.0).astype(BF16)
        q_ref[0, 2 * j + 1] = jnp.where(head0, 0.0, qt).astype(BF16)
        k_ref[0, j] = head_norm_rope(k[:, sl], kg_ref[...]).astype(BF16)
        v_ref[0, j] = v[:, sl].astype(BF16)

    qi = _dot(xb, w_qi_ref[...])
    for j in range(n_qi_tile):
        t = _rope_tile(qi[:, j * LANES:(j + 1) * LANES], cos_b, sin_b, first_b, 16)
        qi_ref[0, 2 * j] = jnp.where(lo, t, 0.0).astype(BF16)
        qi_ref[0, 2 * j + 1] = jnp.where(lo, 0.0, t).astype(BF16)

    kiwi = _dot(xb, w_kiwi_ref[...])
    ki_ref[0] = _rope_tile(kiwi[:, 0:LANES], cos_b, sin_b, first_b, 16).astype(BF16)
    wi_ref[0] = kiwi[:, LANES:2 * LANES] * wi_scale


def _key_to_float(key):
    return pltpu.bitcast(jnp.where(key >= 0, key, key ^ 0x7FFFFFFF), F32)


def _dsa_select_bias(score_scr, bias_scr, nt, tq, topk):
    topk_f = float(topk)

    def count(pred_fn):
        hits = jnp.where(pred_fn(score_scr[0:nt]), 1.0, 0.0)
        return jnp.sum(jnp.sum(hits, axis=0), axis=1, keepdims=True)

    def count_ge(cand):
        return count(lambda s: s >= cand)

    base = jnp.where(count_ge(jnp.zeros((tq, 1), F32)) >= topk_f, 0, INT_MIN).astype(jnp.int32)

    def bit_step(i, base):
        cand = base | jnp.left_shift(jnp.int32(1), 30 - i)
        return jnp.where(count_ge(_key_to_float(cand)) >= topk_f, cand, base)

    thr = _key_to_float(lax.fori_loop(0, 31, bit_step, base))
    lowest = jnp.full((tq, 1), -F32_MAX, F32)
    thr = jnp.where(count_ge(lowest) < topk_f, lowest, thr)
    excess = jnp.max(count_ge(thr)) > topk_f

    def plain(_):
        bias_scr[0:nt] = jnp.where(score_scr[0:nt] >= thr, 0.0, NEG_BIG)
        return 0

    def index_ordered_ties(_):
        need = topk_f - count(lambda s: s > thr)
        r = lax.broadcasted_iota(jnp.int32, (LANES, LANES), 0)
        c = lax.broadcasted_iota(jnp.int32, (LANES, LANES), 1)
        upper = jnp.where(r < c, 1.0, 0.0).astype(BF16)

        def tile_step(t, carry):
            s = score_scr[t]
            tie_f = jnp.where(s == thr, 1.0, 0.0)
            before = _dot(tie_f.astype(BF16), upper) + carry
            tie_bias = jnp.where(s == thr, jnp.where(before < need, 0.0, NEG_BIG), NEG_BIG)
            bias_scr[t] = jnp.where(s > thr, 0.0, tie_bias)
            return carry + jnp.sum(tie_f, axis=1, keepdims=True)

        lax.fori_loop(0, nt, tile_step, jnp.zeros((tq, 1), F32))
        return 0

    lax.cond(excess, index_ordered_ties, plain, 0)


def _dsa_attn_kernel(q_ref, qi_ref, wi_ref, k_ref, v_ref, ki_ref, z_ref, x_ref, w_out_ref,
                     o_ref, score_scr, bias_scr, o_scr, wi_scr, *, tq, n_blocks, topk, n_idx_heads, n_pair,
                     unroll_small, unroll_large):
    qb = pl.program_id(0)
    lo = lax.broadcasted_iota(jnp.int32, (tq, LANES), 1) < 64
    tq_tiles = tq // LANES
    wi = wi_ref[0]
    for h in range(n_idx_heads):
        wi_scr[h] = jnp.broadcast_to(wi[:, h:h + 1], (tq, LANES))

    def body(i):
        nvis = (i + 1) * tq
        nt = nvis // LANES
        diag_vis = _diag_visible(tq)
        if nvis <= topk:
            for t in range(tq_tiles):
                bias_scr[nt - tq_tiles + t] = jnp.where(diag_vis[:, t * LANES:(t + 1) * LANES], 0.0, NEG_BIG)
        else:
            score_scr[0:nt] = jnp.zeros((nt, tq, LANES), F32)

            def idx_head(h, _):
                rel = jnp.maximum(_dot_nt(qi_ref[0, h], ki_ref[0, 0:nvis, :]), 0.0)
                w = wi_scr[h]
                for t in range(nt):
                    score_scr[t] += w * rel[:, t * LANES:(t + 1) * LANES]
                return 0

            lax.fori_loop(0, n_idx_heads, idx_head, 0, unroll=2)
            for t in range(nt - tq_tiles, nt):
                d = t - (nt - tq_tiles)
                score_scr[t] = jnp.where(diag_vis[:, d * LANES:(d + 1) * LANES], score_scr[t], -jnp.inf)
            _dsa_select_bias(score_scr, bias_scr, nt, tq, topk)

        def pair_step(j, _):
            kt = k_ref[0, j, 0:nvis, :]
            vt = v_ref[0, j, 0:nvis, :]
            qq = q_ref[0, pl.ds(2 * j, 2)].reshape(2 * tq, LANES)
            logit = _dot_nt(qq, kt)
            bias = jnp.concatenate([bias_scr[t] for t in range(nt)], axis=1)
            o_even = _softmax_pv(logit[0:tq] + bias, vt)
            o_odd = _softmax_pv(logit[tq:2 * tq] + bias, vt)
            o_scr[j] = jnp.where(lo, o_even, o_odd)
            return 0

        lax.fori_loop(0, n_pair, pair_step, 0, unroll=unroll_small if 2 * i < n_blocks else unroll_large)

    _per_block_variants(qb, n_blocks, body)
    o = jnp.concatenate([o_scr[j] for j in range(n_pair)], axis=1)
    g = (o * _silu(z_ref[0].astype(F32))).astype(BF16)
    o_ref[0] = x_ref[0] + _dot(g, w_out_ref[...])


def _rope_inv(half):
    return ROPE_THETA ** (-jnp.arange(half, dtype=F32) / half)


def _dsa_layer(x, pos3, g, w_in, q_norm, k_norm, w_out, *, tm=256, tq=256):
    b, s, d = x.shape
    width = w_out.shape[0]
    head_dim = q_norm.shape[0]
    assert head_dim == 64 and width % LANES == 0
    n_pair = width // LANES
    idx_dim = 64
    n_idx_heads = (w_in.shape[1] - 4 * width - idx_dim) // (idx_dim + 1)
    assert 4 * width + n_idx_heads * idx_dim + idx_dim + n_idx_heads == w_in.shape[1]
    n_qi_tile = n_idx_heads * idx_dim // LANES
    c0 = 4 * width
    c1 = c0 + n_idx_heads * idx_dim
    c2 = c1 + idx_dim
    half = head_dim // 2

    def pair_split(t):
        t4 = t.reshape(t.shape[:-1] + (n_pair, 2, 2, half))
        return jnp.swapaxes(t4, -3, -2).reshape(t.shape)

    w_main = jnp.concatenate([pair_split(w_in[:, :width]), pair_split(w_in[:, width:2 * width]),
                              w_in[:, 2 * width:c0]], axis=1).astype(BF16)
    w_qi = w_in[:, c0:c1].astype(BF16)
    w_ki = w_in[:, c1:c2]
    w_wi = jnp.pad(w_in[:, c2:], ((0, 0), (0, LANES - n_idx_heads)))
    w_kiwi = jnp.concatenate([w_ki, w_ki, w_wi], axis=1).astype(BF16)

    inv32 = _rope_inv(32)
    inv16 = _rope_inv(16)
    zeros32 = jnp.zeros((32,), F32)
    inv_a = jnp.tile(inv32, 4)
    inv_b = jnp.tile(jnp.concatenate([inv16, inv16, zeros32]), 2)
    inv = jnp.stack([inv_a, inv_b])
    sgn_a = jnp.concatenate([-jnp.ones((64,), F32), jnp.ones((64,), F32)])
    sgn_b = jnp.tile(jnp.concatenate([-jnp.ones((16,), F32), jnp.ones((48,), F32)]), 2)
    sgn = jnp.stack([sgn_a, sgn_b])

    def pair_gain(gn):
        return jnp.concatenate([gn[:half], gn[:half], gn[half:], gn[half:]]).reshape(1, LANES)

    qg = pair_gain(q_norm)
    kg = pair_gain(k_norm)

    topk = min(TOPK_MAX, s // 4)
    assert s % tq == 0 and tq % (1 << CHUNK_SHIFT) == 0
    n_heads = 2 * n_pair
    row = lambda bi, i: (bi, i, 0)
    head_row = lambda bi, i: (bi, 0, i, 0)
    q, k, v, z, qi, ki, wi = pl.pallas_call(
        functools.partial(_dsa_proj_kernel, width=width, n_pair=n_pair, n_qi_tile=n_qi_tile,
                          scale=head_dim ** -0.5 * LOG2E,
                          wi_scale=n_idx_heads ** -0.5 * idx_dim ** -0.5),
        grid=(b, s // tm),
        in_specs=[
            pl.BlockSpec((1, tm, d), row),
            pl.BlockSpec((1, tm, 1), row),
            _const_spec((1, d)),
            _const_spec((d, c0)),
            _const_spec((d, c1 - c0)),
            _const_spec((d, 3 * LANES)),
            _const_spec((1, LANES)),
            _const_spec((1, LANES)),
            _const_spec((2, LANES)),
            _const_spec((2, LANES)),
        ],
        out_specs=[
            pl.BlockSpec((1, n_heads, tm, LANES), head_row),
            pl.BlockSpec((1, n_pair, tm, LANES), head_row),
            pl.BlockSpec((1, n_pair, tm, LANES), head_row),
            pl.BlockSpec((1, tm, width), row),
            pl.BlockSpec((1, n_idx_heads, tm, LANES), head_row),
            pl.BlockSpec((1, tm, LANES), row),
            pl.BlockSpec((1, tm, LANES), row),
        ],
        out_shape=[
            jax.ShapeDtypeStruct((b, n_heads, s, LANES), BF16),
            jax.ShapeDtypeStruct((b, n_pair, s, LANES), BF16),
            jax.ShapeDtypeStruct((b, n_pair, s, LANES), BF16),
            jax.ShapeDtypeStruct((b, s, width), BF16),
            jax.ShapeDtypeStruct((b, n_idx_heads, s, LANES), BF16),
            jax.ShapeDtypeStruct((b, s, LANES), BF16),
            jax.ShapeDtypeStruct((b, s, LANES), F32),
        ],
        compiler_params=_params(("arbitrary", "arbitrary")),
        name="dsa_proj",
    )(x, pos3, g.reshape(1, d), w_main, w_qi, w_kiwi, qg, kg, inv, sgn)

    qrow = lambda i, bi: (bi, i, 0)
    qhead = lambda i, bi: (bi, 0, i, 0)
    full = lambda i, bi: (bi, 0, 0, 0)
    return pl.pallas_call(
        functools.partial(_dsa_attn_kernel, tq=tq, n_blocks=s // tq, topk=topk,
                          n_idx_heads=n_idx_heads, n_pair=n_pair, unroll_small=4, unroll_large=2),
        grid=(s // tq, b),
        in_specs=[
            pl.BlockSpec((1, n_heads, tq, LANES), qhead),
            pl.BlockSpec((1, n_idx_heads, tq, LANES), qhead),
            pl.BlockSpec((1, tq, LANES), qrow),
            pl.BlockSpec((1, n_pair, s, LANES), full),
            pl.BlockSpec((1, n_pair, s, LANES), full),
            pl.BlockSpec((1, s, LANES), lambda i, bi: (bi, 0, 0)),
            pl.BlockSpec((1, tq, width), qrow),
            pl.BlockSpec((1, tq, d), qrow),
            _single((width, d), lambda i, bi: (0, 0)),
        ],
        out_specs=pl.BlockSpec((1, tq, d), qrow),
        out_shape=jax.ShapeDtypeStruct((b, s, d), F32),
        scratch_shapes=[
            pltpu.VMEM((s // LANES, tq, LANES), F32),
            pltpu.VMEM((s // LANES, tq, LANES), F32),
            pltpu.VMEM((n_pair, tq, LANES), F32),
            pltpu.VMEM((n_idx_heads, tq, LANES), F32),
        ],
        compiler_params=_params(("arbitrary", "arbitrary")),
        name="dsa_attn",
    )(q, qi, wi, k, v, ki, z, x, w_out.astype(BF16))


def _mla_proj_kernel(x_ref, pos_ref, g_ref, w_in_ref, qlat_g_ref, kvlat_g_ref, w_uq_ref, w_uq_rot_ref,
                     w_uk_ref, w_uv_ref, gains_ref, inv_ref, sgn_ref,
                     q_ref, k_ref, v_ref, z_ref,
                     *, q_lora, kv_lora, n_heads, qk_dim, scale):
    x = x_ref[0]
    d = x.shape[-1]
    xb = (_rms(x, d) * g_ref[...]).astype(BF16)
    pos = pos_ref[0].astype(F32)
    ang = pos * inv_ref[...]
    cos_t, sin_t = jnp.cos(ang), jnp.sin(ang) * sgn_ref[...]
    cos_q = cos_t * (gains_ref[0:1, :] * scale)
    sin_q = sin_t * (gains_ref[1:2, :] * scale)
    kg = gains_ref[2:3, :]

    c1 = q_lora + kv_lora
    cq = _dot(xb, w_in_ref[:, 0:q_lora])
    ckv = _dot(xb, w_in_ref[:, q_lora:c1])
    kr = _dot(xb, w_in_ref[:, c1:c1 + LANES])
    kr_rot = _dot(xb, w_in_ref[:, c1 + LANES:c1 + 2 * LANES])
    z = _dot(xb, w_in_ref[:, c1 + 2 * LANES:])
    z_ref[0] = z.astype(BF16)

    cq_b = (_rms(cq, q_lora) * qlat_g_ref[...]).astype(BF16)
    q = _dot(cq_b, w_uq_ref[...])
    q_rot = _dot(cq_b, w_uq_rot_ref[...])
    ckv_b = (_rms(ckv, kv_lora) * kvlat_g_ref[...]).astype(BF16)
    kn = _dot(ckv_b, w_uk_ref[...])
    v = _dot(ckv_b, w_uv_ref[...])

    k_rope = kr * (kg * cos_t) + kr_rot * (gains_ref[3:4, :] * sin_t)
    kr_ss = jnp.sum(kr * kr, axis=-1, keepdims=True)
    inv_n = 1.0 / qk_dim
    for h in range(n_heads):
        sl = slice(h * LANES, (h + 1) * LANES)
        qh = q[:, sl]
        q_s = lax.rsqrt(jnp.sum(qh * qh, axis=-1, keepdims=True) * inv_n + EPS)
        q_ref[0, h] = (q_s * (qh * cos_q + q_rot[:, sl] * sin_q)).astype(BF16)
        kh = kn[:, sl]
        k_s = lax.rsqrt((jnp.sum(kh * kh, axis=-1, keepdims=True) + kr_ss) * inv_n + EPS)
        k_ref[0, h] = (k_s * (kh * kg + k_rope)).astype(BF16)
    for j in range(n_heads // 2):
        v_ref[0, j] = v[:, j * LANES:(j + 1) * LANES].astype(BF16)


def _mla_attn_kernel(q_ref, k_ref, v_ref, z_ref, x_ref, w_out_ref, o_ref, o_scr,
                     *, tq, n_blocks, n_pair):
    qb = pl.program_id(0)
    lo = lax.broadcasted_iota(jnp.int32, (tq, LANES), 1) < 64

    def body(i):
        nvis = (i + 1) * tq
        diag_bias = jnp.where(_diag_visible(tq), 0.0, NEG_BIG)

        def head_out(h, vt):
            logit = _dot_nt(q_ref[0, h], k_ref[0, h, 0:nvis, :])
            last = logit[:, nvis - tq:nvis] + diag_bias
            if nvis > tq:
                logit = jnp.concatenate([logit[:, 0:nvis - tq], last], axis=1)
            else:
                logit = last
            return _softmax_pv(logit, vt)

        def pair_step(j, _):
            vt = v_ref[0, j, 0:nvis, :]
            o_scr[j] = jnp.where(lo, head_out(2 * j, vt), head_out(2 * j + 1, vt))
            return 0

        lax.fori_loop(0, n_pair, pair_step, 0, unroll=4)

    _per_block_variants(qb, n_blocks, body)
    o = jnp.concatenate([o_scr[j] for j in range(n_pair)], axis=1)
    g = (o * _silu(z_ref[0].astype(F32))).astype(BF16)
    o_ref[0] = x_ref[0] + _dot(g, w_out_ref[...])


def _mla_layer(x, pos3, g, w_in, q_lat_norm, kv_lat_norm, w_uq, w_ukv, q_norm, k_norm, w_out,
               *, tm=256, tq=256):
    b, s, d = x.shape
    width = w_out.shape[0]
    q_lora = q_lat_norm.shape[0]
    kv_lora = kv_lat_norm.shape[0]
    qk_dim = q_norm.shape[0]
    v_dim = 64
    nope = 64
    rope_dim = qk_dim - nope
    n_heads = width // v_dim
    assert nope + rope_dim <= LANES and w_ukv.shape[1] == n_heads * (nope + v_dim)
    assert w_in.shape[1] == q_lora + kv_lora + rope_dim + width
    assert s % tq == 0 and tq % (1 << CHUNK_SHIFT) == 0
    pad = LANES - qk_dim

    rope_half = rope_dim // 2

    def rope_lanes(t):
        return jnp.pad(t, [(0, 0)] * (t.ndim - 1) + [(nope, LANES - nope - rope_dim)])

    def partner(t):
        return jnp.concatenate([t[..., rope_half:], t[..., :rope_half]], axis=-1)

    c1 = q_lora + kv_lora
    w_kr = w_in[:, c1:c1 + rope_dim]
    w_in_p = jnp.concatenate([w_in[:, :c1], rope_lanes(w_kr), rope_lanes(partner(w_kr)),
                              w_in[:, c1 + rope_dim:]], axis=1).astype(BF16)
    w_uq3 = w_uq.reshape(q_lora, n_heads, qk_dim)
    w_uq_p = jnp.pad(w_uq3, ((0, 0), (0, 0), (0, pad))).reshape(q_lora, n_heads * LANES).astype(BF16)
    w_uq_rot = rope_lanes(partner(w_uq3[:, :, nope:])).reshape(q_lora, n_heads * LANES).astype(BF16)
    w_ukv3 = w_ukv.reshape(kv_lora, n_heads, nope + v_dim)
    w_uk_p = jnp.pad(w_ukv3[:, :, :nope], ((0, 0), (0, 0), (0, LANES - nope)))
    w_uk_p = w_uk_p.reshape(kv_lora, n_heads * LANES).astype(BF16)
    w_uv = w_ukv3[:, :, nope:].reshape(kv_lora, n_heads * v_dim).astype(BF16)
    gains = jnp.stack([jnp.pad(q_norm, (0, pad)), rope_lanes(partner(q_norm[nope:])),
                       jnp.pad(k_norm, (0, pad)), rope_lanes(partner(k_norm[nope:]))])

    inv_h = _rope_inv(rope_half)
    inv = jnp.concatenate([jnp.zeros((nope,), F32), inv_h, inv_h,
                           jnp.zeros((LANES - nope - rope_dim,), F32)]).reshape(1, LANES)
    sgn = jnp.concatenate([jnp.ones((nope,), F32), -jnp.ones((rope_half,), F32),
                           jnp.ones((LANES - nope - rope_half,), F32)]).reshape(1, LANES)

    n_pair = n_heads // 2
    row = lambda bi, i: (bi, i, 0)
    head_row = lambda bi, i: (bi, 0, i, 0)
    n_in = w_in_p.shape[1]
    q, k, v, z = pl.pallas_call(
        functools.partial(_mla_proj_kernel, q_lora=q_lora, kv_lora=kv_lora, n_heads=n_heads,
                          qk_dim=qk_dim, scale=qk_dim ** -0.5 * LOG2E),
        grid=(b, s // tm),
        in_specs=[
            pl.BlockSpec((1, tm, d), row),
            pl.BlockSpec((1, tm, 1), row),
            _const_spec((1, d)),
            _const_spec((d, n_in)),
            _const_spec((1, q_lora)),
            _const_spec((1, kv_lora)),
            _const_spec((q_lora, n_heads * LANES)),
            _const_spec((q_lora, n_heads * LANES)),
            _const_spec((kv_lora, n_heads * LANES)),
            _const_spec((kv_lora, n_heads * v_dim)),
            _const_spec((4, LANES)),
            _const_spec((1, LANES)),
            _const_spec((1, LANES)),
        ],
        out_specs=[
            pl.BlockSpec((1, n_heads, tm, LANES), head_row),
            pl.BlockSpec((1, n_heads, tm, LANES), head_row),
            pl.BlockSpec((1, n_pair, tm, LANES), head_row),
            pl.BlockSpec((1, tm, width), row),
        ],
        out_shape=[
            jax.ShapeDtypeStruct((b, n_heads, s, LANES), BF16),
            jax.ShapeDtypeStruct((b, n_heads, s, LANES), BF16),
            jax.ShapeDtypeStruct((b, n_pair, s, LANES), BF16),
            jax.ShapeDtypeStruct((b, s, width), BF16),
        ],
        compiler_params=_params(("arbitrary", "arbitrary")),
        name="mla_proj",
    )(x, pos3, g.reshape(1, d), w_in_p, q_lat_norm.reshape(1, q_lora), kv_lat_norm.reshape(1, kv_lora),
      w_uq_p, w_uq_rot, w_uk_p, w_uv, gains, inv, sgn)

    qrow = lambda i, bi: (bi, i, 0)
    qhead = lambda i, bi: (bi, 0, i, 0)
    full = lambda i, bi: (bi, 0, 0, 0)
    return pl.pallas_call(
        functools.partial(_mla_attn_kernel, tq=tq, n_blocks=s // tq, n_pair=n_pair),
        grid=(s // tq, b),
        in_specs=[
            pl.BlockSpec((1, n_heads, tq, LANES), qhead),
            pl.BlockSpec((1, n_heads, s, LANES), full),
            pl.BlockSpec((1, n_pair, s, LANES), full),
            pl.BlockSpec((1, tq, width), qrow),
            pl.BlockSpec((1, tq, d), qrow),
            _single((width, d), lambda i, bi: (0, 0)),
        ],
        out_specs=pl.BlockSpec((1, tq, d), qrow),
        out_shape=jax.ShapeDtypeStruct((b, s, d), F32),
        scratch_shapes=[
            pltpu.VMEM((n_pair, tq, LANES), F32),
        ],
        compiler_params=_params(("arbitrary", "arbitrary")),
        name="mla_attn",
    )(q, k, v, z, x, w_out.astype(BF16))


def kernel(x, positions, a_norm, a_w_in, a_conv_w, a_conv_b, a_w_out, b_norm, b_w_in, b_q_norm, b_k_norm, b_w_out, c_norm, c_w_in, c_q_lat_norm, c_kv_lat_norm, c_w_uq, c_w_ukv, c_q_norm, c_k_norm, c_w_out):
    depth = a_norm.shape[0] + b_norm.shape[0] + c_norm.shape[0]
    pos3 = positions.reshape(positions.shape + (1,))
    for i in range(depth):
        kind, j = i % 3, i // 3
        if kind == 0:
            x = _conv_layer(x, a_norm[j], a_w_in[j], a_conv_w[j], a_conv_b[j], a_w_out[j])
        elif kind == 1:
            x = _dsa_layer(x, pos3, b_norm[j], b_w_in[j], b_q_norm[j], b_k_norm[j], b_w_out[j])
        else:
            x = _mla_layer(x, pos3, c_norm[j], c_w_in[j], c_q_lat_norm[j], c_kv_lat_norm[j],
                           c_w_uq[j], c_w_ukv[j], c_q_norm[j], c_k_norm[j], c_w_out[j])
    return x
```

```python
import functools
import math

import jax
import jax.numpy as jnp
from jax import lax
from jax.experimental import pallas as pl
from jax.experimental.pallas import tpu as pltpu

EPS = 1e-6
ROPE_THETA = 10000.0
CHUNK_SHIFT = 6
TOPK_MAX = 256
LANES = 128
NEG_BIG = -1e30
INT_MIN = -(2 ** 31)
F32_MAX = 3.4028234663852886e38
LOG2E = math.log2(math.e)
LOGIT_BOUND = 64.0
VMEM_LIMIT = 56 * 1024 * 1024

BF16 = jnp.bfloat16
F32 = jnp.float32


def _dot(a, b):
    return jnp.dot(a, b, preferred_element_type=F32)


def _dot_nt(a, b):
    return lax.dot_general(a, b, (((1,), (1,)), ((), ())), preferred_element_type=F32)


def _rms(x, n):
    return x * lax.rsqrt(jnp.sum(x * x, axis=-1, keepdims=True) * (1.0 / n) + EPS)


def _silu(z):
    return z * jax.nn.sigmoid(z)


def _rope_tile(t, cos_t, sin_t, first_mask, half):
    partner = jnp.where(first_mask, pltpu.roll(t, LANES - half, 1), pltpu.roll(t, half, 1))
    return t * cos_t + partner * sin_t


def _params(sem):
    return pltpu.CompilerParams(dimension_semantics=sem, vmem_limit_bytes=VMEM_LIMIT)


def _const_spec(shape):
    nd = len(shape)
    return pl.BlockSpec(shape, lambda *_: (0,) * nd)


def _single(shape, index_map):
    return pl.BlockSpec(shape, index_map, pipeline_mode=pl.Buffered(1))


def _diag_visible(tq):
    r = lax.broadcasted_iota(jnp.int32, (tq, tq), 0)
    c = lax.broadcasted_iota(jnp.int32, (tq, tq), 1)
    return (c >> CHUNK_SHIFT) <= (r >> CHUNK_SHIFT)


def _softmax_pv(logit, v, *, bounded):
    if not bounded:
        logit = logit - jnp.max(logit, axis=-1, keepdims=True)
    p = jnp.exp2(logit)
    den = jnp.sum(p, axis=-1, keepdims=True)
    return _dot(p.astype(BF16), v) / den


def _logits_bounded(q_gain, k_gain, n, scale):
    bound = 1.02 * n * scale * jnp.max(jnp.abs(q_gain)) * jnp.max(jnp.abs(k_gain))
    return (bound <= LOGIT_BOUND).astype(jnp.int32).reshape(1, 1)


def _per_block_variants(qb, n_blocks, body):
    for i in range(n_blocks):
        pl.when(qb == i)(functools.partial(body, i))


def _conv_kernel(x_ref, g_ref, w_in_ref, cw_ref, cb_ref, w_out_ref, o_ref, u_scr, *, tm, width):
    j = pl.program_id(1)
    x = x_ref[0]
    d = x.shape[-1]
    xb = (_rms(x, d) * g_ref[...]).astype(BF16)

    @pl.when(j == 0)
    def _():
        u_scr[0:8, :] = jnp.zeros((8, width), F32)

    @pl.when(j > 0)
    def _():
        u_scr[0:8, :] = u_scr[tm:tm + 8, :]

    bg = _dot(xb, w_in_ref[:, 0 * width:1 * width])
    cg = _dot(xb, w_in_ref[:, 1 * width:2 * width])
    hv = _dot(xb, w_in_ref[:, 2 * width:3 * width])
    z = _dot(xb, w_in_ref[:, 3 * width:4 * width])
    u = cg * hv
    u_scr[8:tm + 8, :] = u
    y = (cw_ref[2:3, :] * u + cw_ref[1:2, :] * u_scr[7:7 + tm, :]
         + cw_ref[0:1, :] * u_scr[6:6 + tm, :] + cb_ref[...])
    g = (bg * y * _silu(z)).astype(BF16)
    o_ref[0] = x + _dot(g, w_out_ref[...])


def _conv_layer(x, g, w_in, cw, cb, w_out, *, tm=256):
    b, s, d = x.shape
    width = w_out.shape[0]
    return pl.pallas_call(
        functools.partial(_conv_kernel, tm=tm, width=width),
        grid=(b, s // tm),
        in_specs=[
            pl.BlockSpec((1, tm, d), lambda bi, i: (bi, i, 0)),
            _const_spec((1, d)),
            _const_spec((d, 4 * width)),
            _const_spec((3, width)),
            _const_spec((1, width)),
            _const_spec((width, d)),
        ],
        out_specs=pl.BlockSpec((1, tm, d), lambda bi, i: (bi, i, 0)),
        out_shape=jax.ShapeDtypeStruct((b, s, d), F32),
        scratch_shapes=[pltpu.VMEM((tm + 8, width), F32)],
        compiler_params=_params(("arbitrary", "arbitrary")),
        name="conv_mixer",
    )(x, g.reshape(1, d), w_in.astype(BF16), cw, cb.reshape(1, width), w_out.astype(BF16))


def _dsa_proj_kernel(x_ref, pos_ref, g_ref, w_main_ref, w_qi_ref, w_kiwi_ref, qg_ref, kg_ref,
                     inv_ref, sgn_ref,
                     q_ref, k_ref, v_ref, z_ref, qi_ref, ki_ref, wi_ref,
                     *, width, n_pair, n_qi_tile, scale, wi_scale):
    x = x_ref[0]
    d = x.shape[-1]
    tm = x.shape[0]
    xb = (_rms(x, d) * g_ref[...]).astype(BF16)
    pos = pos_ref[0].astype(F32)
    ang_a = pos * inv_ref[0:1, :]
    ang_b = pos * inv_ref[1:2, :]
    cos_a, sin_a = jnp.cos(ang_a), jnp.sin(ang_a) * sgn_ref[0:1, :]
    cos_b, sin_b = jnp.cos(ang_b), jnp.sin(ang_b) * sgn_ref[1:2, :]
    lane = lax.broadcasted_iota(jnp.int32, (tm, LANES), 1)
    lo = lane < 64
    first_b = (lane & 63) < 16
    head0 = (lane & 32) == 0

    q = _dot(xb, w_main_ref[:, 0 * width:1 * width])
    k = _dot(xb, w_main_ref[:, 1 * width:2 * width])
    v = _dot(xb, w_main_ref[:, 2 * width:3 * width])
    z = _dot(xb, w_main_ref[:, 3 * width:4 * width])
    z_ref[0] = z.astype(BF16)

    def head_norm_rope(t, gain):
        sq = t * t
        s0 = jnp.sum(jnp.where(head0, sq, 0.0), axis=-1, keepdims=True)
        s1 = jnp.sum(jnp.where(head0, 0.0, sq), axis=-1, keepdims=True)
        ms = jnp.where(head0, s0, s1) * (1.0 / 64)
        t = t * lax.rsqrt(ms + EPS) * gain
        return t * cos_a + pltpu.roll(t, 64, 1) * sin_a

    for j in range(n_pair):
        sl = slice(j * LANES, (j + 1) * LANES)
        qt = head_norm_rope(q[:, sl], qg_ref[...]) * scale
        q_ref[0, 2 * j] = jnp.where(head0, qt, 0.0).astype(BF16)
        q_ref[0, 2 * j + 1] = jnp.where(head0, 0.0, qt).astype(BF16)
        k_ref[0, j] = head_norm_rope(k[:, sl], kg_ref[...]).astype(BF16)
        v_ref[0, j] = v[:, sl].astype(BF16)

    qi = _dot(xb, w_qi_ref[...])
    for j in range(n_qi_tile):
        t = _rope_tile(qi[:, j * LANES:(j + 1) * LANES], cos_b, sin_b, first_b, 16)
        qi_ref[0, 2 * j] = jnp.where(lo, t, 0.0).astype(BF16)
        qi_ref[0, 2 * j + 1] = jnp.where(lo, 0.0, t).astype(BF16)

    kiwi = _dot(xb, w_kiwi_ref[...])
    ki_ref[0] = _rope_tile(kiwi[:, 0:LANES], cos_b, sin_b, first_b, 16).astype(BF16)
    wi_ref[0] = kiwi[:, LANES:2 * LANES] * wi_scale


def _key_to_float(key):
    return pltpu.bitcast(jnp.where(key >= 0, key, key ^ 0x7FFFFFFF), F32)


def _dsa_select_bias(score_scr, bias_scr, nt, tq, topk):
    topk_f = float(topk)

    def count(pred_fn):
        hits = jnp.where(pred_fn(score_scr[0:nt]), 1.0, 0.0)
        return jnp.sum(jnp.sum(hits, axis=0), axis=1, keepdims=True)

    def count_ge(cand):
        return count(lambda s: s >= cand)

    base = jnp.where(count_ge(jnp.zeros((tq, 1), F32)) >= topk_f, 0, INT_MIN).astype(jnp.int32)

    def bit_step(i, base):
        cand = base | jnp.left_shift(jnp.int32(1), 30 - i)
        return jnp.where(count_ge(_key_to_float(cand)) >= topk_f, cand, base)

    thr = _key_to_float(lax.fori_loop(0, 31, bit_step, base))
    lowest = jnp.full((tq, 1), -F32_MAX, F32)
    thr = jnp.where(count_ge(lowest) < topk_f, lowest, thr)
    excess = jnp.max(count_ge(thr)) > topk_f

    def plain(_):
        bias_scr[0:nt] = jnp.where(score_scr[0:nt] >= thr, 0.0, NEG_BIG)
        return 0

    def index_ordered_ties(_):
        need = topk_f - count(lambda s: s > thr)
        r = lax.broadcasted_iota(jnp.int32, (LANES, LANES), 0)
        c = lax.broadcasted_iota(jnp.int32, (LANES, LANES), 1)
        upper = jnp.where(r < c, 1.0, 0.0).astype(BF16)

        def tile_step(t, carry):
            s = score_scr[t]
            tie_f = jnp.where(s == thr, 1.0, 0.0)
            before = _dot(tie_f.astype(BF16), upper) + carry
            tie_bias = jnp.where(s == thr, jnp.where(before < need, 0.0, NEG_BIG), NEG_BIG)
            bias_scr[t] = jnp.where(s > thr, 0.0, tie_bias)
            return carry + jnp.sum(tie_f, axis=1, keepdims=True)

        lax.fori_loop(0, nt, tile_step, jnp.zeros((tq, 1), F32))
        return 0

    lax.cond(excess, index_ordered_ties, plain, 0)


def _dsa_attn_kernel(bounded_ref, q_ref, qi_ref, wi_ref, k_ref, v_ref, ki_ref, z_ref, x_ref, w_out_ref,
                     o_ref, score_scr, bias_scr, o_scr, wi_scr, *, tq, n_blocks, topk, n_idx_heads, n_pair,
                     unroll_small, unroll_large):
    qb = pl.program_id(0)
    bounded = bounded_ref[0, 0] != 0
    lo = lax.broadcasted_iota(jnp.int32, (tq, LANES), 1) < 64
    tq_tiles = tq // LANES
    n_tiles = score_scr.shape[0]

    def pair_out(j, nvis, is_bounded):
        kt = k_ref[0, j, 0:nvis, :]
        vt = v_ref[0, j, 0:nvis, :]
        qq = q_ref[0, pl.ds(2 * j, 2)].reshape(2 * tq, LANES)
        logit = _dot_nt(qq, kt)
        bias = jnp.concatenate([bias_scr[t] for t in range(nvis // LANES)], axis=1)
        o_even = _softmax_pv(logit[0:tq] + bias, vt, bounded=is_bounded)
        o_odd = _softmax_pv(logit[tq:2 * tq] + bias, vt, bounded=is_bounded)
        return jnp.where(lo, o_even, o_odd)

    wi = wi_ref[0]
    for h in range(n_idx_heads):
        wi_scr[h] = jnp.broadcast_to(wi[:, h:h + 1], (tq, LANES))

    def body(i):
        nvis = (i + 1) * tq
        nt = nvis // LANES
        diag_vis = _diag_visible(tq)
        if nvis <= topk:
            for t in range(tq_tiles):
                bias_scr[nt - tq_tiles + t] = jnp.where(diag_vis[:, t * LANES:(t + 1) * LANES], 0.0, NEG_BIG)
        else:
            score_scr[0:nt] = jnp.zeros((nt, tq, LANES), F32)

            def idx_head(h, _):
                rel = jnp.maximum(_dot_nt(qi_ref[0, h], ki_ref[0, 0:nvis, :]), 0.0)
                w = wi_scr[h]
                for t in range(nt):
                    score_scr[t] += w * rel[:, t * LANES:(t + 1) * LANES]
                return 0

            lax.fori_loop(0, n_idx_heads, idx_head, 0, unroll=2)
            for t in range(nt - tq_tiles, nt):
                d = t - (nt - tq_tiles)
                score_scr[t] = jnp.where(diag_vis[:, d * LANES:(d + 1) * LANES], score_scr[t], -jnp.inf)
            _dsa_select_bias(score_scr, bias_scr, nt, tq, topk)

        @pl.when(bounded)
        def _():
            def pair_step(j, _):
                o_scr[j] = pair_out(j, nvis, True)
                return 0

            lax.fori_loop(0, n_pair, pair_step, 0, unroll=unroll_small if 2 * i < n_blocks else unroll_large)

    _per_block_variants(qb, n_blocks, body)

    @pl.when(jnp.logical_not(bounded))
    def _():
        def mask_tile(t, _):
            bias_scr[t] = jnp.full((tq, LANES), NEG_BIG, F32)
            return 0

        lax.fori_loop((qb + 1) * tq_tiles, n_tiles, mask_tile, 0)

        def pair_step(j, _):
            o_scr[j] = pair_out(j, n_tiles * LANES, False)
            return 0

        lax.fori_loop(0, n_pair, pair_step, 0)

    o = jnp.concatenate([o_scr[j] for j in range(n_pair)], axis=1)
    g = (o * _silu(z_ref[0].astype(F32))).astype(BF16)
    o_ref[0] = x_ref[0] + _dot(g, w_out_ref[...])


def _rope_inv(half):
    return ROPE_THETA ** (-jnp.arange(half, dtype=F32) / half)


def _dsa_layer(x, pos3, g, w_in, q_norm, k_norm, w_out, *, tm=256, tq=256):
    b, s, d = x.shape
    width = w_out.shape[0]
    head_dim = q_norm.shape[0]
    assert head_dim == 64 and width % LANES == 0
    n_pair = width // LANES
    idx_dim = 64
    n_idx_heads = (w_in.shape[1] - 4 * width - idx_dim) // (idx_dim + 1)
    assert 4 * width + n_idx_heads * idx_dim + idx_dim + n_idx_heads == w_in.shape[1]
    n_qi_tile = n_idx_heads * idx_dim // LANES
    c0 = 4 * width
    c1 = c0 + n_idx_heads * idx_dim
    c2 = c1 + idx_dim
    half = head_dim // 2

    def pair_split(t):
        t4 = t.reshape(t.shape[:-1] + (n_pair, 2, 2, half))
        return jnp.swapaxes(t4, -3, -2).reshape(t.shape)

    w_main = jnp.concatenate([pair_split(w_in[:, :width]), pair_split(w_in[:, width:2 * width]),
                              w_in[:, 2 * width:c0]], axis=1).astype(BF16)
    w_qi = w_in[:, c0:c1].astype(BF16)
    w_ki = w_in[:, c1:c2]
    w_wi = jnp.pad(w_in[:, c2:], ((0, 0), (0, LANES - n_idx_heads)))
    w_kiwi = jnp.concatenate([w_ki, w_ki, w_wi], axis=1).astype(BF16)

    inv32 = _rope_inv(32)
    inv16 = _rope_inv(16)
    zeros32 = jnp.zeros((32,), F32)
    inv_a = jnp.tile(inv32, 4)
    inv_b = jnp.tile(jnp.concatenate([inv16, inv16, zeros32]), 2)
    inv = jnp.stack([inv_a, inv_b])
    sgn_a = jnp.concatenate([-jnp.ones((64,), F32), jnp.ones((64,), F32)])
    sgn_b = jnp.tile(jnp.concatenate([-jnp.ones((16,), F32), jnp.ones((48,), F32)]), 2)
    sgn = jnp.stack([sgn_a, sgn_b])

    def pair_gain(gn):
        return jnp.concatenate([gn[:half], gn[:half], gn[half:], gn[half:]]).reshape(1, LANES)

    qg = pair_gain(q_norm)
    kg = pair_gain(k_norm)

    topk = min(TOPK_MAX, s // 4)
    assert s % tq == 0 and tq % (1 << CHUNK_SHIFT) == 0
    n_heads = 2 * n_pair
    row = lambda bi, i: (bi, i, 0)
    head_row = lambda bi, i: (bi, 0, i, 0)
    q, k, v, z, qi, ki, wi = pl.pallas_call(
        functools.partial(_dsa_proj_kernel, width=width, n_pair=n_pair, n_qi_tile=n_qi_tile,
                          scale=head_dim ** -0.5 * LOG2E,
                          wi_scale=n_idx_heads ** -0.5 * idx_dim ** -0.5),
        grid=(b, s // tm),
        in_specs=[
            pl.BlockSpec((1, tm, d), row),
            pl.BlockSpec((1, tm, 1), row),
            _const_spec((1, d)),
            _const_spec((d, c0)),
            _const_spec((d, c1 - c0)),
            _const_spec((d, 3 * LANES)),
            _const_spec((1, LANES)),
            _const_spec((1, LANES)),
            _const_spec((2, LANES)),
            _const_spec((2, LANES)),
        ],
        out_specs=[
            pl.BlockSpec((1, n_heads, tm, LANES), head_row),
            pl.BlockSpec((1, n_pair, tm, LANES), head_row),
            pl.BlockSpec((1, n_pair, tm, LANES), head_row),
            pl.BlockSpec((1, tm, width), row),
            pl.BlockSpec((1, n_idx_heads, tm, LANES), head_row),
            pl.BlockSpec((1, tm, LANES), row),
            pl.BlockSpec((1, tm, LANES), row),
        ],
        out_shape=[
            jax.ShapeDtypeStruct((b, n_heads, s, LANES), BF16),
            jax.ShapeDtypeStruct((b, n_pair, s, LANES), BF16),
            jax.ShapeDtypeStruct((b, n_pair, s, LANES), BF16),
            jax.ShapeDtypeStruct((b, s, width), BF16),
            jax.ShapeDtypeStruct((b, n_idx_heads, s, LANES), BF16),
            jax.ShapeDtypeStruct((b, s, LANES), BF16),
            jax.ShapeDtypeStruct((b, s, LANES), F32),
        ],
        compiler_params=_params(("arbitrary", "arbitrary")),
        name="dsa_proj",
    )(x, pos3, g.reshape(1, d), w_main, w_qi, w_kiwi, qg, kg, inv, sgn)

    qrow = lambda i, bi: (bi, i, 0)
    qhead = lambda i, bi: (bi, 0, i, 0)
    full = lambda i, bi: (bi, 0, 0, 0)
    return pl.pallas_call(
        functools.partial(_dsa_attn_kernel, tq=tq, n_blocks=s // tq, topk=topk,
                          n_idx_heads=n_idx_heads, n_pair=n_pair, unroll_small=4, unroll_large=2),
        grid=(s // tq, b),
        in_specs=[
            pl.BlockSpec(memory_space=pltpu.SMEM),
            pl.BlockSpec((1, n_heads, tq, LANES), qhead),
            pl.BlockSpec((1, n_idx_heads, tq, LANES), qhead),
            pl.BlockSpec((1, tq, LANES), qrow),
            pl.BlockSpec((1, n_pair, s, LANES), full),
            pl.BlockSpec((1, n_pair, s, LANES), full),
            pl.BlockSpec((1, s, LANES), lambda i, bi: (bi, 0, 0)),
            pl.BlockSpec((1, tq, width), qrow),
            pl.BlockSpec((1, tq, d), qrow),
            _single((width, d), lambda i, bi: (0, 0)),
        ],
        out_specs=pl.BlockSpec((1, tq, d), qrow),
        out_shape=jax.ShapeDtypeStruct((b, s, d), F32),
        scratch_shapes=[
            pltpu.VMEM((s // LANES, tq, LANES), F32),
            pltpu.VMEM((s // LANES, tq, LANES), F32),
            pltpu.VMEM((n_pair, tq, LANES), F32),
            pltpu.VMEM((n_idx_heads, tq, LANES), F32),
        ],
        compiler_params=_params(("arbitrary", "arbitrary")),
        name="dsa_attn",
    )(_logits_bounded(q_norm, k_norm, head_dim, head_dim ** -0.5 * LOG2E),
      q, qi, wi, k, v, ki, z, x, w_out.astype(BF16))


def _mla_proj_kernel(x_ref, pos_ref, g_ref, w_in_ref, qlat_g_ref, kvlat_g_ref, w_uq_ref, w_uq_rot_ref,
                     w_uk_ref, w_uv_ref, gains_ref, inv_ref, sgn_ref,
                     q_ref, k_ref, v_ref, z_ref,
                     *, q_lora, kv_lora, n_heads, qk_dim, scale):
    x = x_ref[0]
    d = x.shape[-1]
    xb = (_rms(x, d) * g_ref[...]).astype(BF16)
    pos = pos_ref[0].astype(F32)
    ang = pos * inv_ref[...]
    cos_t, sin_t = jnp.cos(ang), jnp.sin(ang) * sgn_ref[...]
    cos_q = cos_t * (gains_ref[0:1, :] * scale)
    sin_q = sin_t * (gains_ref[1:2, :] * scale)
    kg = gains_ref[2:3, :]

    c1 = q_lora + kv_lora
    cq = _dot(xb, w_in_ref[:, 0:q_lora])
    ckv = _dot(xb, w_in_ref[:, q_lora:c1])
    kr = _dot(xb, w_in_ref[:, c1:c1 + LANES])
    kr_rot = _dot(xb, w_in_ref[:, c1 + LANES:c1 + 2 * LANES])
    z = _dot(xb, w_in_ref[:, c1 + 2 * LANES:])
    z_ref[0] = z.astype(BF16)

    cq_b = (_rms(cq, q_lora) * qlat_g_ref[...]).astype(BF16)
    q = _dot(cq_b, w_uq_ref[...])
    q_rot = _dot(cq_b, w_uq_rot_ref[...])
    ckv_b = (_rms(ckv, kv_lora) * kvlat_g_ref[...]).astype(BF16)
    kn = _dot(ckv_b, w_uk_ref[...])
    v = _dot(ckv_b, w_uv_ref[...])

    k_rope = kr * (kg * cos_t) + kr_rot * (gains_ref[3:4, :] * sin_t)
    kr_ss = jnp.sum(kr * kr, axis=-1, keepdims=True)
    inv_n = 1.0 / qk_dim
    for h in range(n_heads):
        sl = slice(h * LANES, (h + 1) * LANES)
        qh = q[:, sl]
        q_s = lax.rsqrt(jnp.sum(qh * qh, axis=-1, keepdims=True) * inv_n + EPS)
        q_ref[0, h] = (q_s * (qh * cos_q + q_rot[:, sl] * sin_q)).astype(BF16)
        kh = kn[:, sl]
        k_s = lax.rsqrt((jnp.sum(kh * kh, axis=-1, keepdims=True) + kr_ss) * inv_n + EPS)
        k_ref[0, h] = (k_s * (kh * kg + k_rope)).astype(BF16)
    for j in range(n_heads // 2):
        v_ref[0, j] = v[:, j * LANES:(j + 1) * LANES].astype(BF16)


def _mla_attn_kernel(bounded_ref, q_ref, k_ref, v_ref, z_ref, x_ref, w_out_ref, o_ref, o_scr,
                     *, tq, seq, n_blocks, n_pair):
    qb = pl.program_id(0)
    bounded = bounded_ref[0, 0] != 0
    lo = lax.broadcasted_iota(jnp.int32, (tq, LANES), 1) < 64

    def body(i):
        nvis = (i + 1) * tq
        diag_bias = jnp.where(_diag_visible(tq), 0.0, NEG_BIG)

        def head_out(h, vt):
            logit = _dot_nt(q_ref[0, h], k_ref[0, h, 0:nvis, :])
            last = logit[:, nvis - tq:nvis] + diag_bias
            if nvis > tq:
                logit = jnp.concatenate([logit[:, 0:nvis - tq], last], axis=1)
            else:
                logit = last
            return _softmax_pv(logit, vt, bounded=True)

        def pair_step(j, _):
            vt = v_ref[0, j, 0:nvis, :]
            o_scr[j] = jnp.where(lo, head_out(2 * j, vt), head_out(2 * j + 1, vt))
            return 0

        lax.fori_loop(0, n_pair, pair_step, 0, unroll=4)

    @pl.when(bounded)
    def _():
        _per_block_variants(qb, n_blocks, body)

    @pl.when(jnp.logical_not(bounded))
    def _():
        s_idx = lax.broadcasted_iota(jnp.int32, (tq, seq), 1)
        q_chunk = (qb * tq + lax.broadcasted_iota(jnp.int32, (tq, seq), 0)) >> CHUNK_SHIFT
        bias = jnp.where((s_idx >> CHUNK_SHIFT) <= q_chunk, 0.0, NEG_BIG)

        def pair_step(j, _):
            vt = v_ref[0, j]
            outs = [_softmax_pv(_dot_nt(q_ref[0, 2 * j + e], k_ref[0, 2 * j + e]) + bias, vt, bounded=False)
                    for e in range(2)]
            o_scr[j] = jnp.where(lo, outs[0], outs[1])
            return 0

        lax.fori_loop(0, n_pair, pair_step, 0)

    o = jnp.concatenate([o_scr[j] for j in range(n_pair)], axis=1)
    g = (o * _silu(z_ref[0].astype(F32))).astype(BF16)
    o_ref[0] = x_ref[0] + _dot(g, w_out_ref[...])


def _mla_layer(x, pos3, g, w_in, q_lat_norm, kv_lat_norm, w_uq, w_ukv, q_norm, k_norm, w_out,
               *, tm=256, tq=256):
    b, s, d = x.shape
    width = w_out.shape[0]
    q_lora = q_lat_norm.shape[0]
    kv_lora = kv_lat_norm.shape[0]
    qk_dim = q_norm.shape[0]
    v_dim = 64
    nope = 64
    rope_dim = qk_dim - nope
    n_heads = width // v_dim
    assert nope + rope_dim <= LANES and w_ukv.shape[1] == n_heads * (nope + v_dim)
    assert w_in.shape[1] == q_lora + kv_lora + rope_dim + width
    assert s % tq == 0 and tq % (1 << CHUNK_SHIFT) == 0
    pad = LANES - qk_dim

    rope_half = rope_dim // 2

    def rope_lanes(t):
        return jnp.pad(t, [(0, 0)] * (t.ndim - 1) + [(nope, LANES - nope - rope_dim)])

    def partner(t):
        return jnp.concatenate([t[..., rope_half:], t[..., :rope_half]], axis=-1)

    c1 = q_lora + kv_lora
    w_kr = w_in[:, c1:c1 + rope_dim]
    w_in_p = jnp.concatenate([w_in[:, :c1], rope_lanes(w_kr), rope_lanes(partner(w_kr)),
                              w_in[:, c1 + rope_dim:]], axis=1).astype(BF16)
    w_uq3 = w_uq.reshape(q_lora, n_heads, qk_dim)
    w_uq_p = jnp.pad(w_uq3, ((0, 0), (0, 0), (0, pad))).reshape(q_lora, n_heads * LANES).astype(BF16)
    w_uq_rot = rope_lanes(partner(w_uq3[:, :, nope:])).reshape(q_lora, n_heads * LANES).astype(BF16)
    w_ukv3 = w_ukv.reshape(kv_lora, n_heads, nope + v_dim)
    w_uk_p = jnp.pad(w_ukv3[:, :, :nope], ((0, 0), (0, 0), (0, LANES - nope)))
    w_uk_p = w_uk_p.reshape(kv_lora, n_heads * LANES).astype(BF16)
    w_uv = w_ukv3[:, :, nope:].reshape(kv_lora, n_heads * v_dim).astype(BF16)
    gains = jnp.stack([jnp.pad(q_norm, (0, pad)), rope_lanes(partner(q_norm[nope:])),
                       jnp.pad(k_norm, (0, pad)), rope_lanes(partner(k_norm[nope:]))])

    inv_h = _rope_inv(rope_half)
    inv = jnp.concatenate([jnp.zeros((nope,), F32), inv_h, inv_h,
                           jnp.zeros((LANES - nope - rope_dim,), F32)]).reshape(1, LANES)
    sgn = jnp.concatenate([jnp.ones((nope,), F32), -jnp.ones((rope_half,), F32),
                           jnp.ones((LANES - nope - rope_half,), F32)]).reshape(1, LANES)

    n_pair = n_heads // 2
    row = lambda bi, i: (bi, i, 0)
    head_row = lambda bi, i: (bi, 0, i, 0)
    n_in = w_in_p.shape[1]
    q, k, v, z = pl.pallas_call(
        functools.partial(_mla_proj_kernel, q_lora=q_lora, kv_lora=kv_lora, n_heads=n_heads,
                          qk_dim=qk_dim, scale=qk_dim ** -0.5 * LOG2E),
        grid=(b, s // tm),
        in_specs=[
            pl.BlockSpec((1, tm, d), row),
            pl.BlockSpec((1, tm, 1), row),
            _const_spec((1, d)),
            _const_spec((d, n_in)),
            _const_spec((1, q_lora)),
            _const_spec((1, kv_lora)),
            _const_spec((q_lora, n_heads * LANES)),
            _const_spec((q_lora, n_heads * LANES)),
            _const_spec((kv_lora, n_heads * LANES)),
            _const_spec((kv_lora, n_heads * v_dim)),
            _const_spec((4, LANES)),
            _const_spec((1, LANES)),
            _const_spec((1, LANES)),
        ],
        out_specs=[
            pl.BlockSpec((1, n_heads, tm, LANES), head_row),
            pl.BlockSpec((1, n_heads, tm, LANES), head_row),
            pl.BlockSpec((1, n_pair, tm, LANES), head_row),
            pl.BlockSpec((1, tm, width), row),
        ],
        out_shape=[
            jax.ShapeDtypeStruct((b, n_heads, s, LANES), BF16),
            jax.ShapeDtypeStruct((b, n_heads, s, LANES), BF16),
            jax.ShapeDtypeStruct((b, n_pair, s, LANES), BF16),
            jax.ShapeDtypeStruct((b, s, width), BF16),
        ],
        compiler_params=_params(("arbitrary", "arbitrary")),
        name="mla_proj",
    )(x, pos3, g.reshape(1, d), w_in_p, q_lat_norm.reshape(1, q_lora), kv_lat_norm.reshape(1, kv_lora),
      w_uq_p, w_uq_rot, w_uk_p, w_uv, gains, inv, sgn)

    qrow = lambda i, bi: (bi, i, 0)
    qhead = lambda i, bi: (bi, 0, i, 0)
    full = lambda i, bi: (bi, 0, 0, 0)
    return pl.pallas_call(
        functools.partial(_mla_attn_kernel, tq=tq, seq=s, n_blocks=s // tq, n_pair=n_pair),
        grid=(s // tq, b),
        in_specs=[
            pl.BlockSpec(memory_space=pltpu.SMEM),
            pl.BlockSpec((1, n_heads, tq, LANES), qhead),
            pl.BlockSpec((1, n_heads, s, LANES), full),
            pl.BlockSpec((1, n_pair, s, LANES), full),
            pl.BlockSpec((1, tq, width), qrow),
            pl.BlockSpec((1, tq, d), qrow),
            _single((width, d), lambda i, bi: (0, 0)),
        ],
        out_specs=pl.BlockSpec((1, tq, d), qrow),
        out_shape=jax.ShapeDtypeStruct((b, s, d), F32),
        scratch_shapes=[
            pltpu.VMEM((n_pair, tq, LANES), F32),
        ],
        compiler_params=_params(("arbitrary", "arbitrary")),
        name="mla_attn",
    )(_logits_bounded(q_norm, k_norm, qk_dim, qk_dim ** -0.5 * LOG2E), q, k, v, z, x, w_out.astype(BF16))


def kernel(x, positions, a_norm, a_w_in, a_conv_w, a_conv_b, a_w_out, b_norm, b_w_in, b_q_norm, b_k_norm, b_w_out, c_norm, c_w_in, c_q_lat_norm, c_kv_lat_norm, c_w_uq, c_w_ukv, c_q_norm, c_k_norm, c_w_out):
    depth = a_norm.shape[0] + b_norm.shape[0] + c_norm.shape[0]
    pos3 = positions.reshape(positions.shape + (1,))
    for i in range(depth):
        kind, j = i % 3, i // 3
        if kind == 0:
            x = _conv_layer(x, a_norm[j], a_w_in[j], a_conv_w[j], a_conv_b[j], a_w_out[j])
        elif kind == 1:
            x = _dsa_layer(x, pos3, b_norm[j], b_w_in[j], b_q_norm[j], b_k_norm[j], b_w_out[j])
        else:
            x = _mla_layer(x, pos3, c_norm[j], c_w_in[j], c_q_lat_norm[j], c_kv_lat_norm[j],
                           c_w_uq[j], c_w_ukv[j], c_q_norm[j], c_k_norm[j], c_w_out[j])
    return x
```

```python
import functools
import math

import jax
import jax.numpy as jnp
from jax import lax
from jax.experimental import pallas as pl
from jax.experimental.pallas import tpu as pltpu

EPS = 1e-6
ROPE_THETA = 10000.0
CHUNK_SHIFT = 6
TOPK_MAX = 256
LANES = 128
NEG_BIG = -1e30
INT_MIN = -(2 ** 31)
F32_MAX = 3.4028234663852886e38
LOG2E = math.log2(math.e)
SEARCH_ROW_GROUPS = 4
SEARCH_UNROLL = 4
LOGIT_BOUND = 64.0
VMEM_LIMIT = 56 * 1024 * 1024

BF16 = jnp.bfloat16
F32 = jnp.float32


def _dot(a, b):
    return jnp.dot(a, b, preferred_element_type=F32)


def _dot_nt(a, b):
    return lax.dot_general(a, b, (((1,), (1,)), ((), ())), preferred_element_type=F32)


def _rms(x, n):
    return x * lax.rsqrt(jnp.sum(x * x, axis=-1, keepdims=True) * (1.0 / n) + EPS)


def _silu(z):
    return z * jax.nn.sigmoid(z)


def _rope_tile(t, cos_t, sin_t, first_mask, half):
    partner = jnp.where(first_mask, pltpu.roll(t, LANES - half, 1), pltpu.roll(t, half, 1))
    return t * cos_t + partner * sin_t


def _params(sem):
    return pltpu.CompilerParams(dimension_semantics=sem, vmem_limit_bytes=VMEM_LIMIT)


def _const_spec(shape):
    nd = len(shape)
    return pl.BlockSpec(shape, lambda *_: (0,) * nd)


def _single(shape, index_map):
    return pl.BlockSpec(shape, index_map, pipeline_mode=pl.Buffered(1))


def _diag_visible(tq):
    r = lax.broadcasted_iota(jnp.int32, (tq, tq), 0)
    c = lax.broadcasted_iota(jnp.int32, (tq, tq), 1)
    return (c >> CHUNK_SHIFT) <= (r >> CHUNK_SHIFT)


def _softmax_pv(logit, v, *, bounded):
    if not bounded:
        logit = logit - jnp.max(logit, axis=-1, keepdims=True)
    p = jnp.exp2(logit)
    den = jnp.sum(p, axis=-1, keepdims=True)
    return _dot(p.astype(BF16), v) / den


def _logits_bounded(q_gain, k_gain, n, scale):
    bound = 1.02 * n * scale * jnp.max(jnp.abs(q_gain)) * jnp.max(jnp.abs(k_gain))
    return (bound <= LOGIT_BOUND).astype(jnp.int32).reshape(1, 1)


def _per_block_variants(qb, n_blocks, body):
    for i in range(n_blocks):
        pl.when(qb == i)(functools.partial(body, i))


def _conv_kernel(x_ref, g_ref, w_in_ref, cw_ref, cb_ref, w_out_ref, o_ref, u_scr, *, tm, width):
    j = pl.program_id(1)
    x = x_ref[0]
    d = x.shape[-1]
    xb = (_rms(x, d) * g_ref[...]).astype(BF16)

    @pl.when(j == 0)
    def _():
        u_scr[0:8, :] = jnp.zeros((8, width), F32)

    @pl.when(j > 0)
    def _():
        u_scr[0:8, :] = u_scr[tm:tm + 8, :]

    bg = _dot(xb, w_in_ref[:, 0 * width:1 * width])
    cg = _dot(xb, w_in_ref[:, 1 * width:2 * width])
    hv = _dot(xb, w_in_ref[:, 2 * width:3 * width])
    z = _dot(xb, w_in_ref[:, 3 * width:4 * width])
    u = cg * hv
    u_scr[8:tm + 8, :] = u
    y = (cw_ref[2:3, :] * u + cw_ref[1:2, :] * u_scr[7:7 + tm, :]
         + cw_ref[0:1, :] * u_scr[6:6 + tm, :] + cb_ref[...])
    g = (bg * y * _silu(z)).astype(BF16)
    o_ref[0] = x + _dot(g, w_out_ref[...])


def _conv_layer(x, g, w_in, cw, cb, w_out, *, tm=256):
    b, s, d = x.shape
    width = w_out.shape[0]
    return pl.pallas_call(
        functools.partial(_conv_kernel, tm=tm, width=width),
        grid=(b, s // tm),
        in_specs=[
            pl.BlockSpec((1, tm, d), lambda bi, i: (bi, i, 0)),
            _const_spec((1, d)),
            _const_spec((d, 4 * width)),
            _const_spec((3, width)),
            _const_spec((1, width)),
            _const_spec((width, d)),
        ],
        out_specs=pl.BlockSpec((1, tm, d), lambda bi, i: (bi, i, 0)),
        out_shape=jax.ShapeDtypeStruct((b, s, d), F32),
        scratch_shapes=[pltpu.VMEM((tm + 8, width), F32)],
        compiler_params=_params(("arbitrary", "arbitrary")),
        name="conv_mixer",
    )(x, g.reshape(1, d), w_in.astype(BF16), cw, cb.reshape(1, width), w_out.astype(BF16))


def _dsa_proj_kernel(x_ref, pos_ref, g_ref, w_main_ref, w_qi_ref, w_kiwi_ref, qg_ref, kg_ref,
                     inv_ref, sgn_ref,
                     q_ref, k_ref, v_ref, z_ref, qi_ref, ki_ref, wi_ref,
                     *, width, n_pair, n_qi_tile, scale, wi_scale):
    x = x_ref[0]
    d = x.shape[-1]
    tm = x.shape[0]
    xb = (_rms(x, d) * g_ref[...]).astype(BF16)
    pos = pos_ref[0].astype(F32)
    ang_a = pos * inv_ref[0:1, :]
    ang_b = pos * inv_ref[1:2, :]
    cos_a, sin_a = jnp.cos(ang_a), jnp.sin(ang_a) * sgn_ref[0:1, :]
    cos_b, sin_b = jnp.cos(ang_b), jnp.sin(ang_b) * sgn_ref[1:2, :]
    lane = lax.broadcasted_iota(jnp.int32, (tm, LANES), 1)
    lo = lane < 64
    first_b = (lane & 63) < 16
    head0 = (lane & 32) == 0

    q = _dot(xb, w_main_ref[:, 0 * width:1 * width])
    k = _dot(xb, w_main_ref[:, 1 * width:2 * width])
    v = _dot(xb, w_main_ref[:, 2 * width:3 * width])
    z = _dot(xb, w_main_ref[:, 3 * width:4 * width])
    z_ref[0] = z.astype(BF16)

    def head_norm_rope(t, gain):
        sq = t * t
        s0 = jnp.sum(jnp.where(head0, sq, 0.0), axis=-1, keepdims=True)
        s1 = jnp.sum(jnp.where(head0, 0.0, sq), axis=-1, keepdims=True)
        ms = jnp.where(head0, s0, s1) * (1.0 / 64)
        t = t * lax.rsqrt(ms + EPS) * gain
        return t * cos_a + pltpu.roll(t, 64, 1) * sin_a

    for j in range(n_pair):
        sl = slice(j * LANES, (j + 1) * LANES)
        qt = head_norm_rope(q[:, sl], qg_ref[...]) * scale
        q_ref[0, 2 * j] = jnp.where(head0, qt, 0.0).astype(BF16)
        q_ref[0, 2 * j + 1] = jnp.where(head0, 0.0, qt).astype(BF16)
        k_ref[0, j] = head_norm_rope(k[:, sl], kg_ref[...]).astype(BF16)
        v_ref[0, j] = v[:, sl].astype(BF16)

    qi = _dot(xb, w_qi_ref[...])
    for j in range(n_qi_tile):
        t = _rope_tile(qi[:, j * LANES:(j + 1) * LANES], cos_b, sin_b, first_b, 16)
        qi_ref[0, 2 * j] = jnp.where(lo, t, 0.0).astype(BF16)
        qi_ref[0, 2 * j + 1] = jnp.where(lo, 0.0, t).astype(BF16)

    kiwi = _dot(xb, w_kiwi_ref[...])
    ki_ref[0] = _rope_tile(kiwi[:, 0:LANES], cos_b, sin_b, first_b, 16).astype(BF16)
    wi_ref[0] = kiwi[:, LANES:2 * LANES] * wi_scale


def _key_to_float(key):
    return pltpu.bitcast(jnp.where(key >= 0, key, key ^ 0x7FFFFFFF), F32)


def _dsa_select_bias(score_scr, bias_scr, nt, tq, topk):
    topk_f = float(topk)

    def count(pred_fn):
        hits = jnp.where(pred_fn(score_scr[0:nt]), 1.0, 0.0)
        return jnp.sum(jnp.sum(hits, axis=0), axis=1, keepdims=True)

    def count_ge(cand):
        return count(lambda s: s >= cand)

    rg = tq // SEARCH_ROW_GROUPS

    def bit_step(i, bases):
        bit = jnp.left_shift(jnp.int32(1), 31 - i)
        out = []
        for g, base_g in enumerate(bases):
            cand = base_g + bit
            hits = jnp.where(score_scr[0:nt, g * rg:(g + 1) * rg, :] >= _key_to_float(cand), 1.0, 0.0)
            n_ge = jnp.sum(jnp.sum(hits, axis=0), axis=1, keepdims=True)
            out.append(jnp.where(n_ge >= topk_f, cand, base_g))
        return tuple(out)

    bases = tuple(jnp.full((rg, 1), INT_MIN, jnp.int32) for _ in range(SEARCH_ROW_GROUPS))
    bases = lax.fori_loop(0, 32, bit_step, bases, unroll=SEARCH_UNROLL)
    thr = _key_to_float(jnp.concatenate(bases, axis=0))
    lowest = jnp.full((tq, 1), -F32_MAX, F32)
    thr = jnp.where(count_ge(lowest) < topk_f, lowest, thr)
    excess = jnp.max(count_ge(thr)) > topk_f

    def plain(_):
        bias_scr[0:nt] = jnp.where(score_scr[0:nt] >= thr, 0.0, NEG_BIG)
        return 0

    def index_ordered_ties(_):
        need = topk_f - count(lambda s: s > thr)
        r = lax.broadcasted_iota(jnp.int32, (LANES, LANES), 0)
        c = lax.broadcasted_iota(jnp.int32, (LANES, LANES), 1)
        upper = jnp.where(r < c, 1.0, 0.0).astype(BF16)

        def tile_step(t, carry):
            s = score_scr[t]
            tie_f = jnp.where(s == thr, 1.0, 0.0)
            before = _dot(tie_f.astype(BF16), upper) + carry
            tie_bias = jnp.where(s == thr, jnp.where(before < need, 0.0, NEG_BIG), NEG_BIG)
            bias_scr[t] = jnp.where(s > thr, 0.0, tie_bias)
            return carry + jnp.sum(tie_f, axis=1, keepdims=True)

        lax.fori_loop(0, nt, tile_step, jnp.zeros((tq, 1), F32))
        return 0

    lax.cond(excess, index_ordered_ties, plain, 0)


def _dsa_attn_kernel(bounded_ref, q_ref, qi_ref, wi_ref, k_ref, v_ref, ki_ref, z_ref, x_ref, w_out_ref,
                     o_ref, score_scr, bias_scr, o_scr, wi_scr, *, tq, n_blocks, topk, n_idx_heads, n_pair,
                     unroll_small, unroll_large):
    qb = pl.program_id(0)
    bounded = bounded_ref[0, 0] != 0
    lo = lax.broadcasted_iota(jnp.int32, (tq, LANES), 1) < 64
    tq_tiles = tq // LANES
    n_tiles = score_scr.shape[0]

    def pair_out(j, nvis, is_bounded):
        kt = k_ref[0, j, 0:nvis, :]
        vt = v_ref[0, j, 0:nvis, :]
        qq = q_ref[0, pl.ds(2 * j, 2)].reshape(2 * tq, LANES)
        logit = _dot_nt(qq, kt)
        bias = jnp.concatenate([bias_scr[t] for t in range(nvis // LANES)], axis=1)
        o_even = _softmax_pv(logit[0:tq] + bias, vt, bounded=is_bounded)
        o_odd = _softmax_pv(logit[tq:2 * tq] + bias, vt, bounded=is_bounded)
        return jnp.where(lo, o_even, o_odd)

    wi = wi_ref[0]
    for h in range(n_idx_heads):
        wi_scr[h] = jnp.broadcast_to(wi[:, h:h + 1], (tq, LANES))

    def body(i):
        nvis = (i + 1) * tq
        nt = nvis // LANES
        diag_vis = _diag_visible(tq)
        if nvis <= topk:
            for t in range(tq_tiles):
                bias_scr[nt - tq_tiles + t] = jnp.where(diag_vis[:, t * LANES:(t + 1) * LANES], 0.0, NEG_BIG)
        else:
            score_scr[0:nt] = jnp.zeros((nt, tq, LANES), F32)

            def idx_head(h, _):
                rel = jnp.maximum(_dot_nt(qi_ref[0, h], ki_ref[0, 0:nvis, :]), 0.0)
                w = wi_scr[h]
                for t in range(nt):
                    score_scr[t] += w * rel[:, t * LANES:(t + 1) * LANES]
                return 0

            lax.fori_loop(0, n_idx_heads, idx_head, 0, unroll=2)
            for t in range(nt - tq_tiles, nt):
                d = t - (nt - tq_tiles)
                score_scr[t] = jnp.where(diag_vis[:, d * LANES:(d + 1) * LANES], score_scr[t], -jnp.inf)
            _dsa_select_bias(score_scr, bias_scr, nt, tq, topk)

        @pl.when(bounded)
        def _():
            def pair_step(j, _):
                o_scr[j] = pair_out(j, nvis, True)
                return 0

            lax.fori_loop(0, n_pair, pair_step, 0, unroll=unroll_small if 2 * i < n_blocks else unroll_large)

    _per_block_variants(qb, n_blocks, body)

    @pl.when(jnp.logical_not(bounded))
    def _():
        def mask_tile(t, _):
            bias_scr[t] = jnp.full((tq, LANES), NEG_BIG, F32)
            return 0

        lax.fori_loop((qb + 1) * tq_tiles, n_tiles, mask_tile, 0)

        def pair_step(j, _):
            o_scr[j] = pair_out(j, n_tiles * LANES, False)
            return 0

        lax.fori_loop(0, n_pair, pair_step, 0)

    o = jnp.concatenate([o_scr[j] for j in range(n_pair)], axis=1)
    g = (o * _silu(z_ref[0].astype(F32))).astype(BF16)
    o_ref[0] = x_ref[0] + _dot(g, w_out_ref[...])


def _rope_inv(half):
    return ROPE_THETA ** (-jnp.arange(half, dtype=F32) / half)


def _dsa_layer(x, pos3, g, w_in, q_norm, k_norm, w_out, *, tm=256, tq=256):
    b, s, d = x.shape
    width = w_out.shape[0]
    head_dim = q_norm.shape[0]
    assert head_dim == 64 and width % LANES == 0
    n_pair = width // LANES
    idx_dim = 64
    n_idx_heads = (w_in.shape[1] - 4 * width - idx_dim) // (idx_dim + 1)
    assert 4 * width + n_idx_heads * idx_dim + idx_dim + n_idx_heads == w_in.shape[1]
    n_qi_tile = n_idx_heads * idx_dim // LANES
    c0 = 4 * width
    c1 = c0 + n_idx_heads * idx_dim
    c2 = c1 + idx_dim
    half = head_dim // 2

    def pair_split(t):
        t4 = t.reshape(t.shape[:-1] + (n_pair, 2, 2, half))
        return jnp.swapaxes(t4, -3, -2).reshape(t.shape)

    w_main = jnp.concatenate([pair_split(w_in[:, :width]), pair_split(w_in[:, width:2 * width]),
                              w_in[:, 2 * width:c0]], axis=1).astype(BF16)
    w_qi = w_in[:, c0:c1].astype(BF16)
    w_ki = w_in[:, c1:c2]
    w_wi = jnp.pad(w_in[:, c2:], ((0, 0), (0, LANES - n_idx_heads)))
    w_kiwi = jnp.concatenate([w_ki, w_ki, w_wi], axis=1).astype(BF16)

    inv32 = _rope_inv(32)
    inv16 = _rope_inv(16)
    zeros32 = jnp.zeros((32,), F32)
    inv_a = jnp.tile(inv32, 4)
    inv_b = jnp.tile(jnp.concatenate([inv16, inv16, zeros32]), 2)
    inv = jnp.stack([inv_a, inv_b])
    sgn_a = jnp.concatenate([-jnp.ones((64,), F32), jnp.ones((64,), F32)])
    sgn_b = jnp.tile(jnp.concatenate([-jnp.ones((16,), F32), jnp.ones((48,), F32)]), 2)
    sgn = jnp.stack([sgn_a, sgn_b])

    def pair_gain(gn):
        return jnp.concatenate([gn[:half], gn[:half], gn[half:], gn[half:]]).reshape(1, LANES)

    qg = pair_gain(q_norm)
    kg = pair_gain(k_norm)

    topk = min(TOPK_MAX, s // 4)
    assert s % tq == 0 and tq % (1 << CHUNK_SHIFT) == 0
    n_heads = 2 * n_pair
    row = lambda bi, i: (bi, i, 0)
    head_row = lambda bi, i: (bi, 0, i, 0)
    q, k, v, z, qi, ki, wi = pl.pallas_call(
        functools.partial(_dsa_proj_kernel, width=width, n_pair=n_pair, n_qi_tile=n_qi_tile,
                          scale=head_dim ** -0.5 * LOG2E,
                          wi_scale=n_idx_heads ** -0.5 * idx_dim ** -0.5),
        grid=(b, s // tm),
        in_specs=[
            pl.BlockSpec((1, tm, d), row),
            pl.BlockSpec((1, tm, 1), row),
            _const_spec((1, d)),
            _const_spec((d, c0)),
            _const_spec((d, c1 - c0)),
            _const_spec((d, 3 * LANES)),
            _const_spec((1, LANES)),
            _const_spec((1, LANES)),
            _const_spec((2, LANES)),
            _const_spec((2, LANES)),
        ],
        out_specs=[
            pl.BlockSpec((1, n_heads, tm, LANES), head_row),
            pl.BlockSpec((1, n_pair, tm, LANES), head_row),
            pl.BlockSpec((1, n_pair, tm, LANES), head_row),
            pl.BlockSpec((1, tm, width), row),
            pl.BlockSpec((1, n_idx_heads, tm, LANES), head_row),
            pl.BlockSpec((1, tm, LANES), row),
            pl.BlockSpec((1, tm, LANES), row),
        ],
        out_shape=[
            jax.ShapeDtypeStruct((b, n_heads, s, LANES), BF16),
            jax.ShapeDtypeStruct((b, n_pair, s, LANES), BF16),
            jax.ShapeDtypeStruct((b, n_pair, s, LANES), BF16),
            jax.ShapeDtypeStruct((b, s, width), BF16),
            jax.ShapeDtypeStruct((b, n_idx_heads, s, LANES), BF16),
            jax.ShapeDtypeStruct((b, s, LANES), BF16),
            jax.ShapeDtypeStruct((b, s, LANES), F32),
        ],
        compiler_params=_params(("arbitrary", "arbitrary")),
        name="dsa_proj",
    )(x, pos3, g.reshape(1, d), w_main, w_qi, w_kiwi, qg, kg, inv, sgn)

    qrow = lambda i, bi: (bi, i, 0)
    qhead = lambda i, bi: (bi, 0, i, 0)
    full = lambda i, bi: (bi, 0, 0, 0)
    return pl.pallas_call(
        functools.partial(_dsa_attn_kernel, tq=tq, n_blocks=s // tq, topk=topk,
                          n_idx_heads=n_idx_heads, n_pair=n_pair, unroll_small=4, unroll_large=2),
        grid=(s // tq, b),
        in_specs=[
            pl.BlockSpec(memory_space=pltpu.SMEM),
            pl.BlockSpec((1, n_heads, tq, LANES), qhead),
            pl.BlockSpec((1, n_idx_heads, tq, LANES), qhead),
            pl.BlockSpec((1, tq, LANES), qrow),
            pl.BlockSpec((1, n_pair, s, LANES), full),
            pl.BlockSpec((1, n_pair, s, LANES), full),
            pl.BlockSpec((1, s, LANES), lambda i, bi: (bi, 0, 0)),
            pl.BlockSpec((1, tq, width), qrow),
            pl.BlockSpec((1, tq, d), qrow),
            _single((width, d), lambda i, bi: (0, 0)),
        ],
        out_specs=pl.BlockSpec((1, tq, d), qrow),
        out_shape=jax.ShapeDtypeStruct((b, s, d), F32),
        scratch_shapes=[
            pltpu.VMEM((s // LANES, tq, LANES), F32),
            pltpu.VMEM((s // LANES, tq, LANES), F32),
            pltpu.VMEM((n_pair, tq, LANES), F32),
            pltpu.VMEM((n_idx_heads, tq, LANES), F32),
        ],
        compiler_params=_params(("arbitrary", "arbitrary")),
        name="dsa_attn",
    )(_logits_bounded(q_norm, k_norm, head_dim, head_dim ** -0.5 * LOG2E),
      q, qi, wi, k, v, ki, z, x, w_out.astype(BF16))


def _mla_proj_kernel(x_ref, pos_ref, g_ref, w_in_ref, qlat_g_ref, kvlat_g_ref, w_uq_ref, w_uq_rot_ref,
                     w_uk_ref, w_uv_ref, gains_ref, inv_ref, sgn_ref,
                     q_ref, k_ref, v_ref, z_ref,
                     *, q_lora, kv_lora, n_heads, qk_dim, scale):
    x = x_ref[0]
    d = x.shape[-1]
    xb = (_rms(x, d) * g_ref[...]).astype(BF16)
    pos = pos_ref[0].astype(F32)
    ang = pos * inv_ref[...]
    cos_t, sin_t = jnp.cos(ang), jnp.sin(ang) * sgn_ref[...]
    cos_q = cos_t * (gains_ref[0:1, :] * scale)
    sin_q = sin_t * (gains_ref[1:2, :] * scale)
    kg = gains_ref[2:3, :]

    c1 = q_lora + kv_lora
    cq = _dot(xb, w_in_ref[:, 0:q_lora])
    ckv = _dot(xb, w_in_ref[:, q_lora:c1])
    kr = _dot(xb, w_in_ref[:, c1:c1 + LANES])
    kr_rot = _dot(xb, w_in_ref[:, c1 + LANES:c1 + 2 * LANES])
    z = _dot(xb, w_in_ref[:, c1 + 2 * LANES:])
    z_ref[0] = z.astype(BF16)

    cq_b = (_rms(cq, q_lora) * qlat_g_ref[...]).astype(BF16)
    q = _dot(cq_b, w_uq_ref[...])
    q_rot = _dot(cq_b, w_uq_rot_ref[...])
    ckv_b = (_rms(ckv, kv_lora) * kvlat_g_ref[...]).astype(BF16)
    kn = _dot(ckv_b, w_uk_ref[...])
    v = _dot(ckv_b, w_uv_ref[...])

    k_rope = kr * (kg * cos_t) + kr_rot * (gains_ref[3:4, :] * sin_t)
    kr_ss = jnp.sum(kr * kr, axis=-1, keepdims=True)
    inv_n = 1.0 / qk_dim
    for h in range(n_heads):
        sl = slice(h * LANES, (h + 1) * LANES)
        qh = q[:, sl]
        q_s = lax.rsqrt(jnp.sum(qh * qh, axis=-1, keepdims=True) * inv_n + EPS)
        q_ref[0, h] = (q_s * (qh * cos_q + q_rot[:, sl] * sin_q)).astype(BF16)
        kh = kn[:, sl]
        k_s = lax.rsqrt((jnp.sum(kh * kh, axis=-1, keepdims=True) + kr_ss) * inv_n + EPS)
        k_ref[0, h] = (k_s * (kh * kg + k_rope)).astype(BF16)
    for j in range(n_heads // 2):
        v_ref[0, j] = v[:, j * LANES:(j + 1) * LANES].astype(BF16)


def _mla_attn_kernel(bounded_ref, q_ref, k_ref, v_ref, z_ref, x_ref, w_out_ref, o_ref, o_scr,
                     *, tq, seq, n_blocks, n_pair):
    qb = pl.program_id(0)
    bounded = bounded_ref[0, 0] != 0
    lo = lax.broadcasted_iota(jnp.int32, (tq, LANES), 1) < 64

    def body(i):
        nvis = (i + 1) * tq
        diag_bias = jnp.where(_diag_visible(tq), 0.0, NEG_BIG)

        def head_out(h, vt):
            logit = _dot_nt(q_ref[0, h], k_ref[0, h, 0:nvis, :])
            last = logit[:, nvis - tq:nvis] + diag_bias
            if nvis > tq:
                logit = jnp.concatenate([logit[:, 0:nvis - tq], last], axis=1)
            else:
                logit = last
            return _softmax_pv(logit, vt, bounded=True)

        def pair_step(j, _):
            vt = v_ref[0, j, 0:nvis, :]
            o_scr[j] = jnp.where(lo, head_out(2 * j, vt), head_out(2 * j + 1, vt))
            return 0

        lax.fori_loop(0, n_pair, pair_step, 0, unroll=4)

    @pl.when(bounded)
    def _():
        _per_block_variants(qb, n_blocks, body)

    @pl.when(jnp.logical_not(bounded))
    def _():
        s_idx = lax.broadcasted_iota(jnp.int32, (tq, seq), 1)
        q_chunk = (qb * tq + lax.broadcasted_iota(jnp.int32, (tq, seq), 0)) >> CHUNK_SHIFT
        bias = jnp.where((s_idx >> CHUNK_SHIFT) <= q_chunk, 0.0, NEG_BIG)

        def pair_step(j, _):
            vt = v_ref[0, j]
            outs = [_softmax_pv(_dot_nt(q_ref[0, 2 * j + e], k_ref[0, 2 * j + e]) + bias, vt, bounded=False)
                    for e in range(2)]
            o_scr[j] = jnp.where(lo, outs[0], outs[1])
            return 0

        lax.fori_loop(0, n_pair, pair_step, 0)

    o = jnp.concatenate([o_scr[j] for j in range(n_pair)], axis=1)
    g = (o * _silu(z_ref[0].astype(F32))).astype(BF16)
    o_ref[0] = x_ref[0] + _dot(g, w_out_ref[...])


def _mla_layer(x, pos3, g, w_in, q_lat_norm, kv_lat_norm, w_uq, w_ukv, q_norm, k_norm, w_out,
               *, tm=256, tq=256):
    b, s, d = x.shape
    width = w_out.shape[0]
    q_lora = q_lat_norm.shape[0]
    kv_lora = kv_lat_norm.shape[0]
    qk_dim = q_norm.shape[0]
    v_dim = 64
    nope = 64
    rope_dim = qk_dim - nope
    n_heads = width // v_dim
    assert nope + rope_dim <= LANES and w_ukv.shape[1] == n_heads * (nope + v_dim)
    assert w_in.shape[1] == q_lora + kv_lora + rope_dim + width
    assert s % tq == 0 and tq % (1 << CHUNK_SHIFT) == 0
    pad = LANES - qk_dim

    rope_half = rope_dim // 2

    def rope_lanes(t):
        return jnp.pad(t, [(0, 0)] * (t.ndim - 1) + [(nope, LANES - nope - rope_dim)])

    def partner(t):
        return jnp.concatenate([t[..., rope_half:], t[..., :rope_half]], axis=-1)

    c1 = q_lora + kv_lora
    w_kr = w_in[:, c1:c1 + rope_dim]
    w_in_p = jnp.concatenate([w_in[:, :c1], rope_lanes(w_kr), rope_lanes(partner(w_kr)),
                              w_in[:, c1 + rope_dim:]], axis=1).astype(BF16)
    w_uq3 = w_uq.reshape(q_lora, n_heads, qk_dim)
    w_uq_p = jnp.pad(w_uq3, ((0, 0), (0, 0), (0, pad))).reshape(q_lora, n_heads * LANES).astype(BF16)
    w_uq_rot = rope_lanes(partner(w_uq3[:, :, nope:])).reshape(q_lora, n_heads * LANES).astype(BF16)
    w_ukv3 = w_ukv.reshape(kv_lora, n_heads, nope + v_dim)
    w_uk_p = jnp.pad(w_ukv3[:, :, :nope], ((0, 0), (0, 0), (0, LANES - nope)))
    w_uk_p = w_uk_p.reshape(kv_lora, n_heads * LANES).astype(BF16)
    w_uv = w_ukv3[:, :, nope:].reshape(kv_lora, n_heads * v_dim).astype(BF16)
    gains = jnp.stack([jnp.pad(q_norm, (0, pad)), rope_lanes(partner(q_norm[nope:])),
                       jnp.pad(k_norm, (0, pad)), rope_lanes(partner(k_norm[nope:]))])

    inv_h = _rope_inv(rope_half)
    inv = jnp.concatenate([jnp.zeros((nope,), F32), inv_h, inv_h,
                           jnp.zeros((LANES - nope - rope_dim,), F32)]).reshape(1, LANES)
    sgn = jnp.concatenate([jnp.ones((nope,), F32), -jnp.ones((rope_half,), F32),
                           jnp.ones((LANES - nope - rope_half,), F32)]).reshape(1, LANES)

    n_pair = n_heads // 2
    row = lambda bi, i: (bi, i, 0)
    head_row = lambda bi, i: (bi, 0, i, 0)
    n_in = w_in_p.shape[1]
    q, k, v, z = pl.pallas_call(
        functools.partial(_mla_proj_kernel, q_lora=q_lora, kv_lora=kv_lora, n_heads=n_heads,
                          qk_dim=qk_dim, scale=qk_dim ** -0.5 * LOG2E),
        grid=(b, s // tm),
        in_specs=[
            pl.BlockSpec((1, tm, d), row),
            pl.BlockSpec((1, tm, 1), row),
            _const_spec((1, d)),
            _const_spec((d, n_in)),
            _const_spec((1, q_lora)),
            _const_spec((1, kv_lora)),
            _const_spec((q_lora, n_heads * LANES)),
            _const_spec((q_lora, n_heads * LANES)),
            _const_spec((kv_lora, n_heads * LANES)),
            _const_spec((kv_lora, n_heads * v_dim)),
            _const_spec((4, LANES)),
            _const_spec((1, LANES)),
            _const_spec((1, LANES)),
        ],
        out_specs=[
            pl.BlockSpec((1, n_heads, tm, LANES), head_row),
            pl.BlockSpec((1, n_heads, tm, LANES), head_row),
            pl.BlockSpec((1, n_pair, tm, LANES), head_row),
            pl.BlockSpec((1, tm, width), row),
        ],
        out_shape=[
            jax.ShapeDtypeStruct((b, n_heads, s, LANES), BF16),
            jax.ShapeDtypeStruct((b, n_heads, s, LANES), BF16),
            jax.ShapeDtypeStruct((b, n_pair, s, LANES), BF16),
            jax.ShapeDtypeStruct((b, s, width), BF16),
        ],
        compiler_params=_params(("arbitrary", "arbitrary")),
        name="mla_proj",
    )(x, pos3, g.reshape(1, d), w_in_p, q_lat_norm.reshape(1, q_lora), kv_lat_norm.reshape(1, kv_lora),
      w_uq_p, w_uq_rot, w_uk_p, w_uv, gains, inv, sgn)

    qrow = lambda i, bi: (bi, i, 0)
    qhead = lambda i, bi: (bi, 0, i, 0)
    full = lambda i, bi: (bi, 0, 0, 0)
    return pl.pallas_call(
        functools.partial(_mla_attn_kernel, tq=tq, seq=s, n_blocks=s // tq, n_pair=n_pair),
        grid=(s // tq, b),
        in_specs=[
            pl.BlockSpec(memory_space=pltpu.SMEM),
            pl.BlockSpec((1, n_heads, tq, LANES), qhead),
            pl.BlockSpec((1, n_heads, s, LANES), full),
            pl.BlockSpec((1, n_pair, s, LANES), full),
            pl.BlockSpec((1, tq, width), qrow),
            pl.BlockSpec((1, tq, d), qrow),
            _single((width, d), lambda i, bi: (0, 0)),
        ],
        out_specs=pl.BlockSpec((1, tq, d), qrow),
        out_shape=jax.ShapeDtypeStruct((b, s, d), F32),
        scratch_shapes=[
            pltpu.VMEM((n_pair, tq, LANES), F32),
        ],
        compiler_params=_params(("arbitrary", "arbitrary")),
        name="mla_attn",
    )(_logits_bounded(q_norm, k_norm, qk_dim, qk_dim ** -0.5 * LOG2E), q, k, v, z, x, w_out.astype(BF16))


def kernel(x, positions, a_norm, a_w_in, a_conv_w, a_conv_b, a_w_out, b_norm, b_w_in, b_q_norm, b_k_norm, b_w_out, c_norm, c_w_in, c_q_lat_norm, c_kv_lat_norm, c_w_uq, c_w_ukv, c_q_norm, c_k_norm, c_w_out):
    depth = a_norm.shape[0] + b_norm.shape[0] + c_norm.shape[0]
    pos3 = positions.reshape(positions.shape + (1,))
    for i in range(depth):
        kind, j = i % 3, i // 3
        if kind == 0:
            x = _conv_layer(x, a_norm[j], a_w_in[j], a_conv_w[j], a_conv_b[j], a_w_out[j])
        elif kind == 1:
            x = _dsa_layer(x, pos3, b_norm[j], b_w_in[j], b_q_norm[j], b_k_norm[j], b_w_out[j])
        else:
            x = _mla_layer(x, pos3, c_norm[j], c_w_in[j], c_q_lat_norm[j], c_kv_lat_norm[j],
                           c_w_uq[j], c_w_ukv[j], c_q_norm[j], c_k_norm[j], c_w_out[j])
    return x
```

```python
import functools
import math

import jax
import jax.numpy as jnp
from jax import lax
from jax.experimental import pallas as pl
from jax.experimental.pallas import tpu as pltpu

EPS = 1e-6
ROPE_THETA = 10000.0
CHUNK_SHIFT = 6
TOPK_MAX = 256
LANES = 128
NEG_BIG = -1e30
INT_MIN = -(2 ** 31)
F32_MAX = 3.4028234663852886e38
LOG2E = math.log2(math.e)
SEARCH_ROW_GROUPS = 4
SEARCH_UNROLL = 4
LOGIT_BOUND = 64.0
VMEM_LIMIT = 56 * 1024 * 1024

BF16 = jnp.bfloat16
F32 = jnp.float32


def _dot(a, b):
    return jnp.dot(a, b, preferred_element_type=F32)


def _dot_nt(a, b):
    return lax.dot_general(a, b, (((1,), (1,)), ((), ())), preferred_element_type=F32)


def _rms(x, n):
    return x * lax.rsqrt(jnp.sum(x * x, axis=-1, keepdims=True) * (1.0 / n) + EPS)


def _silu(z):
    return z * jax.nn.sigmoid(z)


def _rope_tile(t, cos_t, sin_t, first_mask, half):
    partner = jnp.where(first_mask, pltpu.roll(t, LANES - half, 1), pltpu.roll(t, half, 1))
    return t * cos_t + partner * sin_t


def _params(sem):
    return pltpu.CompilerParams(dimension_semantics=sem, vmem_limit_bytes=VMEM_LIMIT)


def _single(shape, index_map):
    return pl.BlockSpec(shape, index_map, pipeline_mode=pl.Buffered(1))


def _const_spec(shape):
    nd = len(shape)
    return _single(shape, lambda *_: (0,) * nd)


def _diag_visible(tq):
    r = lax.broadcasted_iota(jnp.int32, (tq, tq), 0)
    c = lax.broadcasted_iota(jnp.int32, (tq, tq), 1)
    return (c >> CHUNK_SHIFT) <= (r >> CHUNK_SHIFT)


def _softmax_pv(logit, v, *, bounded):
    if not bounded:
        logit = logit - jnp.max(logit, axis=-1, keepdims=True)
    p = jnp.exp2(logit)
    den = jnp.sum(p, axis=-1, keepdims=True)
    return _dot(p.astype(BF16), v) / den


def _logits_bounded(q_gain, k_gain, n, scale):
    bound = 1.02 * n * scale * jnp.max(jnp.abs(q_gain)) * jnp.max(jnp.abs(k_gain))
    return (bound <= LOGIT_BOUND).astype(jnp.int32).reshape(1, 1)


def _per_block_variants(qb, n_blocks, body):
    for i in range(n_blocks):
        pl.when(qb == i)(functools.partial(body, i))


def _conv_kernel(x_ref, g_ref, w_in_ref, cw_ref, cb_ref, w_out_ref, o_ref, u_scr, *, tm, width):
    j = pl.program_id(1)
    x = x_ref[0]
    d = x.shape[-1]
    xb = (_rms(x, d) * g_ref[...]).astype(BF16)

    @pl.when(j == 0)
    def _():
        u_scr[0:8, :] = jnp.zeros((8, width), F32)

    @pl.when(j > 0)
    def _():
        u_scr[0:8, :] = u_scr[tm:tm + 8, :]

    bg = _dot(xb, w_in_ref[:, 0 * width:1 * width])
    cg = _dot(xb, w_in_ref[:, 1 * width:2 * width])
    hv = _dot(xb, w_in_ref[:, 2 * width:3 * width])
    z = _dot(xb, w_in_ref[:, 3 * width:4 * width])
    u = cg * hv
    u_scr[8:tm + 8, :] = u
    y = (cw_ref[2:3, :] * u + cw_ref[1:2, :] * u_scr[7:7 + tm, :]
         + cw_ref[0:1, :] * u_scr[6:6 + tm, :] + cb_ref[...])
    g = (bg * y * _silu(z)).astype(BF16)
    o_ref[0] = x + _dot(g, w_out_ref[...])


def _conv_layer(x, g, w_in, cw, cb, w_out, *, tm=256):
    b, s, d = x.shape
    width = w_out.shape[0]
    return pl.pallas_call(
        functools.partial(_conv_kernel, tm=tm, width=width),
        grid=(b, s // tm),
        in_specs=[
            pl.BlockSpec((1, tm, d), lambda bi, i: (bi, i, 0)),
            _const_spec((1, d)),
            _const_spec((d, 4 * width)),
            _const_spec((3, width)),
            _const_spec((1, width)),
            _const_spec((width, d)),
        ],
        out_specs=pl.BlockSpec((1, tm, d), lambda bi, i: (bi, i, 0)),
        out_shape=jax.ShapeDtypeStruct((b, s, d), F32),
        scratch_shapes=[pltpu.VMEM((tm + 8, width), F32)],
        compiler_params=_params(("arbitrary", "arbitrary")),
        name="conv_mixer",
    )(x, g.reshape(1, d), w_in.astype(BF16), cw, cb.reshape(1, width), w_out.astype(BF16))


def _dsa_proj_kernel(x_ref, pos_ref, g_ref, w_main_ref, w_qi_ref, w_kiwi_ref, qg_ref, kg_ref,
                     inv_ref, sgn_ref,
                     q_ref, k_ref, v_ref, z_ref, qi_ref, ki_ref, wi_ref,
                     *, width, n_pair, n_qi_tile, scale, wi_scale):
    x = x_ref[0]
    d = x.shape[-1]
    tm = x.shape[0]
    xb = (_rms(x, d) * g_ref[...]).astype(BF16)
    pos = pos_ref[0].astype(F32)
    ang_a = pos * inv_ref[0:1, :]
    ang_b = pos * inv_ref[1:2, :]
    cos_a, sin_a = jnp.cos(ang_a), jnp.sin(ang_a) * sgn_ref[0:1, :]
    cos_b, sin_b = jnp.cos(ang_b), jnp.sin(ang_b) * sgn_ref[1:2, :]
    lane = lax.broadcasted_iota(jnp.int32, (tm, LANES), 1)
    lo = lane < 64
    first_b = (lane & 63) < 16
    head0 = (lane & 32) == 0

    q = _dot(xb, w_main_ref[:, 0 * width:1 * width])
    k = _dot(xb, w_main_ref[:, 1 * width:2 * width])
    v = _dot(xb, w_main_ref[:, 2 * width:3 * width])
    z = _dot(xb, w_main_ref[:, 3 * width:4 * width])
    z_ref[0] = z.astype(BF16)

    def head_norm_rope(t, gain):
        sq = t * t
        s0 = jnp.sum(jnp.where(head0, sq, 0.0), axis=-1, keepdims=True)
        s1 = jnp.sum(jnp.where(head0, 0.0, sq), axis=-1, keepdims=True)
        ms = jnp.where(head0, s0, s1) * (1.0 / 64)
        t = t * lax.rsqrt(ms + EPS) * gain
        return t * cos_a + pltpu.roll(t, 64, 1) * sin_a

    for j in range(n_pair):
        sl = slice(j * LANES, (j + 1) * LANES)
        qt = head_norm_rope(q[:, sl], qg_ref[...]) * scale
        q_ref[0, 2 * j] = jnp.where(head0, qt, 0.0).astype(BF16)
        q_ref[0, 2 * j + 1] = jnp.where(head0, 0.0, qt).astype(BF16)
        k_ref[0, j] = head_norm_rope(k[:, sl], kg_ref[...]).astype(BF16)
        v_ref[0, j] = v[:, sl].astype(BF16)

    qi = _dot(xb, w_qi_ref[...])
    for j in range(n_qi_tile):
        t = _rope_tile(qi[:, j * LANES:(j + 1) * LANES], cos_b, sin_b, first_b, 16)
        qi_ref[0, 2 * j] = jnp.where(lo, t, 0.0).astype(BF16)
        qi_ref[0, 2 * j + 1] = jnp.where(lo, 0.0, t).astype(BF16)

    kiwi = _dot(xb, w_kiwi_ref[...])
    ki_ref[0] = _rope_tile(kiwi[:, 0:LANES], cos_b, sin_b, first_b, 16).astype(BF16)
    wi_ref[0] = kiwi[:, LANES:2 * LANES] * wi_scale


def _key_to_float(key):
    return pltpu.bitcast(jnp.where(key >= 0, key, key ^ 0x7FFFFFFF), F32)


def _dsa_select_bias(score_scr, bias_scr, nt, tq, topk):
    topk_f = float(topk)

    def count(pred_fn):
        hits = jnp.where(pred_fn(score_scr[0:nt]), 1.0, 0.0)
        return jnp.sum(jnp.sum(hits, axis=0), axis=1, keepdims=True)

    def count_ge(cand):
        return count(lambda s: s >= cand)

    rg = tq // SEARCH_ROW_GROUPS

    def bit_step(i, bases):
        bit = jnp.left_shift(jnp.int32(1), 31 - i)
        out = []
        for g, base_g in enumerate(bases):
            cand = base_g + bit
            hits = jnp.where(score_scr[0:nt, g * rg:(g + 1) * rg, :] >= _key_to_float(cand), 1.0, 0.0)
            n_ge = jnp.sum(jnp.sum(hits, axis=0), axis=1, keepdims=True)
            out.append(jnp.where(n_ge >= topk_f, cand, base_g))
        return tuple(out)

    bases = tuple(jnp.full((rg, 1), INT_MIN, jnp.int32) for _ in range(SEARCH_ROW_GROUPS))
    bases = lax.fori_loop(0, 32, bit_step, bases, unroll=SEARCH_UNROLL)
    thr = _key_to_float(jnp.concatenate(bases, axis=0))
    lowest = jnp.full((tq, 1), -F32_MAX, F32)
    thr = jnp.where(count_ge(lowest) < topk_f, lowest, thr)
    excess = jnp.max(count_ge(thr)) > topk_f

    def plain(_):
        bias_scr[0:nt] = jnp.where(score_scr[0:nt] >= thr, 0.0, NEG_BIG)
        return 0

    def index_ordered_ties(_):
        need = topk_f - count(lambda s: s > thr)
        r = lax.broadcasted_iota(jnp.int32, (LANES, LANES), 0)
        c = lax.broadcasted_iota(jnp.int32, (LANES, LANES), 1)
        upper = jnp.where(r < c, 1.0, 0.0).astype(BF16)

        def tile_step(t, carry):
            s = score_scr[t]
            tie_f = jnp.where(s == thr, 1.0, 0.0)
            before = _dot(tie_f.astype(BF16), upper) + carry
            tie_bias = jnp.where(s == thr, jnp.where(before < need, 0.0, NEG_BIG), NEG_BIG)
            bias_scr[t] = jnp.where(s > thr, 0.0, tie_bias)
            return carry + jnp.sum(tie_f, axis=1, keepdims=True)

        lax.fori_loop(0, nt, tile_step, jnp.zeros((tq, 1), F32))
        return 0

    lax.cond(excess, index_ordered_ties, plain, 0)


def _dsa_attn_kernel(bounded_ref, q_ref, qi_ref, wi_ref, k_ref, v_ref, ki_ref, z_ref, x_ref, w_out_ref,
                     o_ref, score_scr, bias_scr, o_scr, wi_scr, *, tq, n_blocks, topk, n_idx_heads, n_pair,
                     unroll_small, unroll_large):
    qb = pl.program_id(1)
    bounded = bounded_ref[0, 0] != 0
    lo = lax.broadcasted_iota(jnp.int32, (tq, LANES), 1) < 64
    tq_tiles = tq // LANES
    n_tiles = score_scr.shape[0]

    def pair_out(j, nvis, is_bounded):
        kt = k_ref[0, j, 0:nvis, :]
        vt = v_ref[0, j, 0:nvis, :]
        qq = q_ref[0, pl.ds(2 * j, 2)].reshape(2 * tq, LANES)
        logit = _dot_nt(qq, kt)
        bias = jnp.concatenate([bias_scr[t] for t in range(nvis // LANES)], axis=1)
        o_even = _softmax_pv(logit[0:tq] + bias, vt, bounded=is_bounded)
        o_odd = _softmax_pv(logit[tq:2 * tq] + bias, vt, bounded=is_bounded)
        return jnp.where(lo, o_even, o_odd)

    wi = wi_ref[0]
    for h in range(n_idx_heads):
        wi_scr[h] = jnp.broadcast_to(wi[:, h:h + 1], (tq, LANES))

    def body(i):
        nvis = (i + 1) * tq
        nt = nvis // LANES
        diag_vis = _diag_visible(tq)
        if nvis <= topk:
            for t in range(tq_tiles):
                bias_scr[nt - tq_tiles + t] = jnp.where(diag_vis[:, t * LANES:(t + 1) * LANES], 0.0, NEG_BIG)
        else:
            score_scr[0:nt] = jnp.zeros((nt, tq, LANES), F32)

            def idx_head(h, _):
                rel = jnp.maximum(_dot_nt(qi_ref[0, h], ki_ref[0, 0:nvis, :]), 0.0)
                w = wi_scr[h]
                for t in range(nt):
                    score_scr[t] += w * rel[:, t * LANES:(t + 1) * LANES]
                return 0

            lax.fori_loop(0, n_idx_heads, idx_head, 0, unroll=2)
            for t in range(nt - tq_tiles, nt):
                d = t - (nt - tq_tiles)
                score_scr[t] = jnp.where(diag_vis[:, d * LANES:(d + 1) * LANES], score_scr[t], -jnp.inf)
            _dsa_select_bias(score_scr, bias_scr, nt, tq, topk)

        @pl.when(bounded)
        def _():
            def pair_step(j, _):
                o_scr[j] = pair_out(j, nvis, True)
                return 0

            lax.fori_loop(0, n_pair, pair_step, 0, unroll=unroll_small if 2 * i < n_blocks else unroll_large)

    _per_block_variants(qb, n_blocks, body)

    @pl.when(jnp.logical_not(bounded))
    def _():
        def mask_tile(t, _):
            bias_scr[t] = jnp.full((tq, LANES), NEG_BIG, F32)
            return 0

        lax.fori_loop((qb + 1) * tq_tiles, n_tiles, mask_tile, 0)

        def pair_step(j, _):
            o_scr[j] = pair_out(j, n_tiles * LANES, False)
            return 0

        lax.fori_loop(0, n_pair, pair_step, 0)

    o = jnp.concatenate([o_scr[j] for j in range(n_pair)], axis=1)
    g = (o * _silu(z_ref[0].astype(F32))).astype(BF16)
    o_ref[0] = x_ref[0] + _dot(g, w_out_ref[...])


def _rope_inv(half):
    return ROPE_THETA ** (-jnp.arange(half, dtype=F32) / half)


def _dsa_layer(x, pos3, g, w_in, q_norm, k_norm, w_out, *, tm=256, tq=256):
    b, s, d = x.shape
    width = w_out.shape[0]
    head_dim = q_norm.shape[0]
    assert head_dim == 64 and width % LANES == 0
    n_pair = width // LANES
    idx_dim = 64
    n_idx_heads = (w_in.shape[1] - 4 * width - idx_dim) // (idx_dim + 1)
    assert 4 * width + n_idx_heads * idx_dim + idx_dim + n_idx_heads == w_in.shape[1]
    n_qi_tile = n_idx_heads * idx_dim // LANES
    c0 = 4 * width
    c1 = c0 + n_idx_heads * idx_dim
    c2 = c1 + idx_dim
    half = head_dim // 2

    def pair_split(t):
        t4 = t.reshape(t.shape[:-1] + (n_pair, 2, 2, half))
        return jnp.swapaxes(t4, -3, -2).reshape(t.shape)

    w_main = jnp.concatenate([pair_split(w_in[:, :width]), pair_split(w_in[:, width:2 * width]),
                              w_in[:, 2 * width:c0]], axis=1).astype(BF16)
    w_qi = w_in[:, c0:c1].astype(BF16)
    w_ki = w_in[:, c1:c2]
    w_wi = jnp.pad(w_in[:, c2:], ((0, 0), (0, LANES - n_idx_heads)))
    w_kiwi = jnp.concatenate([w_ki, w_ki, w_wi], axis=1).astype(BF16)

    inv32 = _rope_inv(32)
    inv16 = _rope_inv(16)
    zeros32 = jnp.zeros((32,), F32)
    inv_a = jnp.tile(inv32, 4)
    inv_b = jnp.tile(jnp.concatenate([inv16, inv16, zeros32]), 2)
    inv = jnp.stack([inv_a, inv_b])
    sgn_a = jnp.concatenate([-jnp.ones((64,), F32), jnp.ones((64,), F32)])
    sgn_b = jnp.tile(jnp.concatenate([-jnp.ones((16,), F32), jnp.ones((48,), F32)]), 2)
    sgn = jnp.stack([sgn_a, sgn_b])

    def pair_gain(gn):
        return jnp.concatenate([gn[:half], gn[:half], gn[half:], gn[half:]]).reshape(1, LANES)

    qg = pair_gain(q_norm)
    kg = pair_gain(k_norm)

    topk = min(TOPK_MAX, s // 4)
    assert s % tq == 0 and tq % (1 << CHUNK_SHIFT) == 0
    n_heads = 2 * n_pair
    row = lambda bi, i: (bi, i, 0)
    head_row = lambda bi, i: (bi, 0, i, 0)
    q, k, v, z, qi, ki, wi = pl.pallas_call(
        functools.partial(_dsa_proj_kernel, width=width, n_pair=n_pair, n_qi_tile=n_qi_tile,
                          scale=head_dim ** -0.5 * LOG2E,
                          wi_scale=n_idx_heads ** -0.5 * idx_dim ** -0.5),
        grid=(b, s // tm),
        in_specs=[
            pl.BlockSpec((1, tm, d), row),
            pl.BlockSpec((1, tm, 1), row),
            _const_spec((1, d)),
            _const_spec((d, c0)),
            _const_spec((d, c1 - c0)),
            _const_spec((d, 3 * LANES)),
            _const_spec((1, LANES)),
            _const_spec((1, LANES)),
            _const_spec((2, LANES)),
            _const_spec((2, LANES)),
        ],
        out_specs=[
            pl.BlockSpec((1, n_heads, tm, LANES), head_row),
            pl.BlockSpec((1, n_pair, tm, LANES), head_row),
            pl.BlockSpec((1, n_pair, tm, LANES), head_row),
            pl.BlockSpec((1, tm, width), row),
            pl.BlockSpec((1, n_idx_heads, tm, LANES), head_row),
            pl.BlockSpec((1, tm, LANES), row),
            pl.BlockSpec((1, tm, LANES), row),
        ],
        out_shape=[
            jax.ShapeDtypeStruct((b, n_heads, s, LANES), BF16),
            jax.ShapeDtypeStruct((b, n_pair, s, LANES), BF16),
            jax.ShapeDtypeStruct((b, n_pair, s, LANES), BF16),
            jax.ShapeDtypeStruct((b, s, width), BF16),
            jax.ShapeDtypeStruct((b, n_idx_heads, s, LANES), BF16),
            jax.ShapeDtypeStruct((b, s, LANES), BF16),
            jax.ShapeDtypeStruct((b, s, LANES), F32),
        ],
        compiler_params=_params(("arbitrary", "arbitrary")),
        name="dsa_proj",
    )(x, pos3, g.reshape(1, d), w_main, w_qi, w_kiwi, qg, kg, inv, sgn)

    qrow = lambda bi, i: (bi, i, 0)
    qhead = lambda bi, i: (bi, 0, i, 0)
    full = lambda bi, i: (bi, 0, 0, 0)
    return pl.pallas_call(
        functools.partial(_dsa_attn_kernel, tq=tq, n_blocks=s // tq, topk=topk,
                          n_idx_heads=n_idx_heads, n_pair=n_pair, unroll_small=4, unroll_large=2),
        grid=(b, s // tq),
        in_specs=[
            pl.BlockSpec(memory_space=pltpu.SMEM),
            pl.BlockSpec((1, n_heads, tq, LANES), qhead),
            pl.BlockSpec((1, n_idx_heads, tq, LANES), qhead),
            pl.BlockSpec((1, tq, LANES), qrow),
            pl.BlockSpec((1, n_pair, s, LANES), full),
            pl.BlockSpec((1, n_pair, s, LANES), full),
            pl.BlockSpec((1, s, LANES), lambda bi, i: (bi, 0, 0)),
            pl.BlockSpec((1, tq, width), qrow),
            pl.BlockSpec((1, tq, d), qrow),
            _single((width, d), lambda bi, i: (0, 0)),
        ],
        out_specs=pl.BlockSpec((1, tq, d), qrow),
        out_shape=jax.ShapeDtypeStruct((b, s, d), F32),
        scratch_shapes=[
            pltpu.VMEM((s // LANES, tq, LANES), F32),
            pltpu.VMEM((s // LANES, tq, LANES), F32),
            pltpu.VMEM((n_pair, tq, LANES), F32),
            pltpu.VMEM((n_idx_heads, tq, LANES), F32),
        ],
        compiler_params=_params(("arbitrary", "arbitrary")),
        name="dsa_attn",
    )(_logits_bounded(q_norm, k_norm, head_dim, head_dim ** -0.5 * LOG2E),
      q, qi, wi, k, v, ki, z, x, w_out.astype(BF16))


def _mla_proj_kernel(x_ref, pos_ref, g_ref, w_in_ref, qlat_g_ref, kvlat_g_ref, w_uq_ref, w_uq_rot_ref,
                     w_uk_ref, w_uv_ref, gains_ref, inv_ref, sgn_ref,
                     q_ref, k_ref, v_ref, z_ref,
                     *, q_lora, kv_lora, n_heads, qk_dim, scale):
    x = x_ref[0]
    d = x.shape[-1]
    xb = (_rms(x, d) * g_ref[...]).astype(BF16)
    pos = pos_ref[0].astype(F32)
    ang = pos * inv_ref[...]
    cos_t, sin_t = jnp.cos(ang), jnp.sin(ang) * sgn_ref[...]
    cos_q = cos_t * (gains_ref[0:1, :] * scale)
    sin_q = sin_t * (gains_ref[1:2, :] * scale)
    kg = gains_ref[2:3, :]

    c1 = q_lora + kv_lora
    cq = _dot(xb, w_in_ref[:, 0:q_lora])
    ckv = _dot(xb, w_in_ref[:, q_lora:c1])
    kr = _dot(xb, w_in_ref[:, c1:c1 + LANES])
    kr_rot = _dot(xb, w_in_ref[:, c1 + LANES:c1 + 2 * LANES])
    z = _dot(xb, w_in_ref[:, c1 + 2 * LANES:])
    z_ref[0] = z.astype(BF16)

    cq_b = (_rms(cq, q_lora) * qlat_g_ref[...]).astype(BF16)
    q = _dot(cq_b, w_uq_ref[...])
    q_rot = _dot(cq_b, w_uq_rot_ref[...])
    ckv_b = (_rms(ckv, kv_lora) * kvlat_g_ref[...]).astype(BF16)
    kn = _dot(ckv_b, w_uk_ref[...])
    v = _dot(ckv_b, w_uv_ref[...])

    k_rope = kr * (kg * cos_t) + kr_rot * (gains_ref[3:4, :] * sin_t)
    kr_ss = jnp.sum(kr * kr, axis=-1, keepdims=True)
    inv_n = 1.0 / qk_dim
    for h in range(n_heads):
        sl = slice(h * LANES, (h + 1) * LANES)
        qh = q[:, sl]
        q_s = lax.rsqrt(jnp.sum(qh * qh, axis=-1, keepdims=True) * inv_n + EPS)
        q_ref[0, h] = (q_s * (qh * cos_q + q_rot[:, sl] * sin_q)).astype(BF16)
        kh = kn[:, sl]
        k_s = lax.rsqrt((jnp.sum(kh * kh, axis=-1, keepdims=True) + kr_ss) * inv_n + EPS)
        k_ref[0, h] = (k_s * (kh * kg + k_rope)).astype(BF16)
    for j in range(n_heads // 2):
        v_ref[0, j] = v[:, j * LANES:(j + 1) * LANES].astype(BF16)


def _mla_attn_kernel(bounded_ref, q_ref, k_ref, v_ref, z_ref, x_ref, w_out_ref, o_ref, o_scr,
                     *, tq, seq, n_blocks, n_pair):
    qb = pl.program_id(1)
    bounded = bounded_ref[0, 0] != 0
    lo = lax.broadcasted_iota(jnp.int32, (tq, LANES), 1) < 64

    def body(i):
        nvis = (i + 1) * tq
        diag_bias = jnp.where(_diag_visible(tq), 0.0, NEG_BIG)

        def head_out(h, vt):
            logit = _dot_nt(q_ref[0, h], k_ref[0, h, 0:nvis, :])
            last = logit[:, nvis - tq:nvis] + diag_bias
            if nvis > tq:
                logit = jnp.concatenate([logit[:, 0:nvis - tq], last], axis=1)
            else:
                logit = last
            return _softmax_pv(logit, vt, bounded=True)

        def pair_step(j, _):
            vt = v_ref[0, j, 0:nvis, :]
            o_scr[j] = jnp.where(lo, head_out(2 * j, vt), head_out(2 * j + 1, vt))
            return 0

        lax.fori_loop(0, n_pair, pair_step, 0, unroll=4)

    @pl.when(bounded)
    def _():
        _per_block_variants(qb, n_blocks, body)

    @pl.when(jnp.logical_not(bounded))
    def _():
        s_idx = lax.broadcasted_iota(jnp.int32, (tq, seq), 1)
        q_chunk = (qb * tq + lax.broadcasted_iota(jnp.int32, (tq, seq), 0)) >> CHUNK_SHIFT
        bias = jnp.where((s_idx >> CHUNK_SHIFT) <= q_chunk, 0.0, NEG_BIG)

        def pair_step(j, _):
            vt = v_ref[0, j]
            outs = [_softmax_pv(_dot_nt(q_ref[0, 2 * j + e], k_ref[0, 2 * j + e]) + bias, vt, bounded=False)
                    for e in range(2)]
            o_scr[j] = jnp.where(lo, outs[0], outs[1])
            return 0

        lax.fori_loop(0, n_pair, pair_step, 0)

    o = jnp.concatenate([o_scr[j] for j in range(n_pair)], axis=1)
    g = (o * _silu(z_ref[0].astype(F32))).astype(BF16)
    o_ref[0] = x_ref[0] + _dot(g, w_out_ref[...])


def _mla_layer(x, pos3, g, w_in, q_lat_norm, kv_lat_norm, w_uq, w_ukv, q_norm, k_norm, w_out,
               *, tm=256, tq=256):
    b, s, d = x.shape
    width = w_out.shape[0]
    q_lora = q_lat_norm.shape[0]
    kv_lora = kv_lat_norm.shape[0]
    qk_dim = q_norm.shape[0]
    v_dim = 64
    nope = 64
    rope_dim = qk_dim - nope
    n_heads = width // v_dim
    assert nope + rope_dim <= LANES and w_ukv.shape[1] == n_heads * (nope + v_dim)
    assert w_in.shape[1] == q_lora + kv_lora + rope_dim + width
    assert s % tq == 0 and tq % (1 << CHUNK_SHIFT) == 0
    pad = LANES - qk_dim

    rope_half = rope_dim // 2

    def rope_lanes(t):
        return jnp.pad(t, [(0, 0)] * (t.ndim - 1) + [(nope, LANES - nope - rope_dim)])

    def partner(t):
        return jnp.concatenate([t[..., rope_half:], t[..., :rope_half]], axis=-1)

    c1 = q_lora + kv_lora
    w_kr = w_in[:, c1:c1 + rope_dim]
    w_in_p = jnp.concatenate([w_in[:, :c1], rope_lanes(w_kr), rope_lanes(partner(w_kr)),
                              w_in[:, c1 + rope_dim:]], axis=1).astype(BF16)
    w_uq3 = w_uq.reshape(q_lora, n_heads, qk_dim)
    w_uq_p = jnp.pad(w_uq3, ((0, 0), (0, 0), (0, pad))).reshape(q_lora, n_heads * LANES).astype(BF16)
    w_uq_rot = rope_lanes(partner(w_uq3[:, :, nope:])).reshape(q_lora, n_heads * LANES).astype(BF16)
    w_ukv3 = w_ukv.reshape(kv_lora, n_heads, nope + v_dim)
    w_uk_p = jnp.pad(w_ukv3[:, :, :nope], ((0, 0), (0, 0), (0, LANES - nope)))
    w_uk_p = w_uk_p.reshape(kv_lora, n_heads * LANES).astype(BF16)
    w_uv = w_ukv3[:, :, nope:].reshape(kv_lora, n_heads * v_dim).astype(BF16)
    gains = jnp.stack([jnp.pad(q_norm, (0, pad)), rope_lanes(partner(q_norm[nope:])),
                       jnp.pad(k_norm, (0, pad)), rope_lanes(partner(k_norm[nope:]))])

    inv_h = _rope_inv(rope_half)
    inv = jnp.concatenate([jnp.zeros((nope,), F32), inv_h, inv_h,
                           jnp.zeros((LANES - nope - rope_dim,), F32)]).reshape(1, LANES)
    sgn = jnp.concatenate([jnp.ones((nope,), F32), -jnp.ones((rope_half,), F32),
                           jnp.ones((LANES - nope - rope_half,), F32)]).reshape(1, LANES)

    n_pair = n_heads // 2
    row = lambda bi, i: (bi, i, 0)
    head_row = lambda bi, i: (bi, 0, i, 0)
    n_in = w_in_p.shape[1]
    q, k, v, z = pl.pallas_call(
        functools.partial(_mla_proj_kernel, q_lora=q_lora, kv_lora=kv_lora, n_heads=n_heads,
                          qk_dim=qk_dim, scale=qk_dim ** -0.5 * LOG2E),
        grid=(b, s // tm),
        in_specs=[
            pl.BlockSpec((1, tm, d), row),
            pl.BlockSpec((1, tm, 1), row),
            _const_spec((1, d)),
            _const_spec((d, n_in)),
            _const_spec((1, q_lora)),
            _const_spec((1, kv_lora)),
            _const_spec((q_lora, n_heads * LANES)),
            _const_spec((q_lora, n_heads * LANES)),
            _const_spec((kv_lora, n_heads * LANES)),
            _const_spec((kv_lora, n_heads * v_dim)),
            _const_spec((4, LANES)),
            _const_spec((1, LANES)),
            _const_spec((1, LANES)),
        ],
        out_specs=[
            pl.BlockSpec((1, n_heads, tm, LANES), head_row),
            pl.BlockSpec((1, n_heads, tm, LANES), head_row),
            pl.BlockSpec((1, n_pair, tm, LANES), head_row),
            pl.BlockSpec((1, tm, width), row),
        ],
        out_shape=[
            jax.ShapeDtypeStruct((b, n_heads, s, LANES), BF16),
            jax.ShapeDtypeStruct((b, n_heads, s, LANES), BF16),
            jax.ShapeDtypeStruct((b, n_pair, s, LANES), BF16),
            jax.ShapeDtypeStruct((b, s, width), BF16),
        ],
        compiler_params=_params(("arbitrary", "arbitrary")),
        name="mla_proj",
    )(x, pos3, g.reshape(1, d), w_in_p, q_lat_norm.reshape(1, q_lora), kv_lat_norm.reshape(1, kv_lora),
      w_uq_p, w_uq_rot, w_uk_p, w_uv, gains, inv, sgn)

    qrow = lambda bi, i: (bi, i, 0)
    qhead = lambda bi, i: (bi, 0, i, 0)
    full = lambda bi, i: (bi, 0, 0, 0)
    return pl.pallas_call(
        functools.partial(_mla_attn_kernel, tq=tq, seq=s, n_blocks=s // tq, n_pair=n_pair),
        grid=(b, s // tq),
        in_specs=[
            pl.BlockSpec(memory_space=pltpu.SMEM),
            pl.BlockSpec((1, n_heads, tq, LANES), qhead),
            pl.BlockSpec((1, n_heads, s, LANES), full),
            pl.BlockSpec((1, n_pair, s, LANES), full),
            pl.BlockSpec((1, tq, width), qrow),
            pl.BlockSpec((1, tq, d), qrow),
            _single((width, d), lambda bi, i: (0, 0)),
        ],
        out_specs=pl.BlockSpec((1, tq, d), qrow),
        out_shape=jax.ShapeDtypeStruct((b, s, d), F32),
        scratch_shapes=[
            pltpu.VMEM((n_pair, tq, LANES), F32),
        ],
        compiler_params=_params(("arbitrary", "arbitrary")),
        name="mla_attn",
    )(_logits_bounded(q_norm, k_norm, qk_dim, qk_dim ** -0.5 * LOG2E), q, k, v, z, x, w_out.astype(BF16))


def kernel(x, positions, a_norm, a_w_in, a_conv_w, a_conv_b, a_w_out, b_norm, b_w_in, b_q_norm, b_k_norm, b_w_out, c_norm, c_w_in, c_q_lat_norm, c_kv_lat_norm, c_w_uq, c_w_ukv, c_q_norm, c_k_norm, c_w_out):
    depth = a_norm.shape[0] + b_norm.shape[0] + c_norm.shape[0]
    pos3 = positions.reshape(positions.shape + (1,))
    for i in range(depth):
        kind, j = i % 3, i // 3
        if kind == 0:
            x = _conv_layer(x, a_norm[j], a_w_in[j], a_conv_w[j], a_conv_b[j], a_w_out[j])
        elif kind == 1:
            x = _dsa_layer(x, pos3, b_norm[j], b_w_in[j], b_q_norm[j], b_k_norm[j], b_w_out[j])
        else:
            x = _mla_layer(x, pos3, c_norm[j], c_w_in[j], c_q_lat_norm[j], c_kv_lat_norm[j],
                           c_w_uq[j], c_w_ukv[j], c_q_norm[j], c_k_norm[j], c_w_out[j])
    return x
```

```python
import functools
import math

import jax
import jax.numpy as jnp
from jax import lax
from jax.experimental import pallas as pl
from jax.experimental.pallas import tpu as pltpu

EPS = 1e-6
ROPE_THETA = 10000.0
CHUNK_SHIFT = 6
TOPK_MAX = 256
LANES = 128
NEG_BIG = -1e30
INT_MIN = -(2 ** 31)
F32_MAX = 3.4028234663852886e38
LOG2E = math.log2(math.e)
SEARCH_ROW_GROUPS = 4
SEARCH_UNROLL = 4
LOGIT_BOUND = 64.0
VMEM_LIMIT = 56 * 1024 * 1024

BF16 = jnp.bfloat16
F32 = jnp.float32


def _dot(a, b):
    return jnp.dot(a, b, preferred_element_type=F32)


def _dot_nt(a, b):
    return lax.dot_general(a, b, (((1,), (1,)), ((), ())), preferred_element_type=F32)


def _rms(x, n):
    return x * lax.rsqrt(jnp.sum(x * x, axis=-1, keepdims=True) * (1.0 / n) + EPS)


def _silu(z):
    return z * jax.nn.sigmoid(z)


def _rope_tile(t, cos_t, sin_t, first_mask, half):
    partner = jnp.where(first_mask, pltpu.roll(t, LANES - half, 1), pltpu.roll(t, half, 1))
    return t * cos_t + partner * sin_t


def _params(sem):
    return pltpu.CompilerParams(dimension_semantics=sem, vmem_limit_bytes=VMEM_LIMIT)


def _single(shape, index_map):
    return pl.BlockSpec(shape, index_map, pipeline_mode=pl.Buffered(1))


def _const_spec(shape):
    nd = len(shape)
    return _single(shape, lambda *_: (0,) * nd)


def _diag_visible(tq):
    r = lax.broadcasted_iota(jnp.int32, (tq, tq), 0)
    c = lax.broadcasted_iota(jnp.int32, (tq, tq), 1)
    return (c >> CHUNK_SHIFT) <= (r >> CHUNK_SHIFT)


def _softmax_pv(logit, v, *, bounded):
    if not bounded:
        logit = logit - jnp.max(logit, axis=-1, keepdims=True)
    p = jnp.exp2(logit)
    den = jnp.sum(p, axis=-1, keepdims=True)
    return _dot(p.astype(BF16), v) / den


def _logits_bounded(q_gain, k_gain, n, scale):
    bound = 1.02 * n * scale * jnp.max(jnp.abs(q_gain)) * jnp.max(jnp.abs(k_gain))
    return (bound <= LOGIT_BOUND).astype(jnp.int32).reshape(1, 1)


def _per_block_variants(qb, n_blocks, body):
    for i in range(n_blocks):
        pl.when(qb == i)(functools.partial(body, i))


def _conv_kernel(x_ref, g_ref, w_in_f32_ref, cw_ref, cb_ref, w_out_f32_ref, o_ref, u_scr, w_in_ref, w_out_ref,
                 *, tm, width):
    j = pl.program_id(1)

    @pl.when(jnp.logical_and(pl.program_id(0) == 0, j == 0))
    def _():
        w_in_ref[...] = w_in_f32_ref[...].astype(BF16)
        w_out_ref[...] = w_out_f32_ref[...].astype(BF16)

    x = x_ref[0]
    d = x.shape[-1]
    xb = (_rms(x, d) * g_ref[...]).astype(BF16)

    @pl.when(j == 0)
    def _():
        u_scr[0:8, :] = jnp.zeros((8, width), F32)

    @pl.when(j > 0)
    def _():
        u_scr[0:8, :] = u_scr[tm:tm + 8, :]

    bg = _dot(xb, w_in_ref[:, 0 * width:1 * width])
    cg = _dot(xb, w_in_ref[:, 1 * width:2 * width])
    hv = _dot(xb, w_in_ref[:, 2 * width:3 * width])
    z = _dot(xb, w_in_ref[:, 3 * width:4 * width])
    u = cg * hv
    u_scr[8:tm + 8, :] = u
    y = (cw_ref[2:3, :] * u + cw_ref[1:2, :] * u_scr[7:7 + tm, :]
         + cw_ref[0:1, :] * u_scr[6:6 + tm, :] + cb_ref[...])
    g = (bg * y * _silu(z)).astype(BF16)
    o_ref[0] = x + _dot(g, w_out_ref[...])


def _conv_layer(x, layer, g, w_in_all, cw, cb, w_out_all, *, tm=256):
    b, s, d = x.shape
    width = w_out_all.shape[1]
    return pl.pallas_call(
        functools.partial(_conv_kernel, tm=tm, width=width),
        grid=(b, s // tm),
        in_specs=[
            pl.BlockSpec((1, tm, d), lambda bi, i: (bi, i, 0)),
            _const_spec((1, d)),
            _single((None, d, 4 * width), lambda bi, i: (layer, 0, 0)),
            _const_spec((3, width)),
            _const_spec((1, width)),
            _single((None, width, d), lambda bi, i: (layer, 0, 0)),
        ],
        out_specs=pl.BlockSpec((1, tm, d), lambda bi, i: (bi, i, 0)),
        out_shape=jax.ShapeDtypeStruct((b, s, d), F32),
        scratch_shapes=[
            pltpu.VMEM((tm + 8, width), F32),
            pltpu.VMEM((d, 4 * width), BF16),
            pltpu.VMEM((width, d), BF16),
        ],
        compiler_params=_params(("arbitrary", "arbitrary")),
        name="conv_mixer",
    )(x, g.reshape(1, d), w_in_all, cw, cb.reshape(1, width), w_out_all)


def _dsa_proj_kernel(x_ref, pos_ref, g_ref, w_main_ref, w_qi_ref, w_kiwi_ref, qg_ref, kg_ref,
                     inv_ref, sgn_ref,
                     q_ref, k_ref, v_ref, z_ref, qi_ref, ki_ref, wi_ref,
                     *, width, n_pair, n_qi_tile, scale, wi_scale):
    x = x_ref[0]
    d = x.shape[-1]
    tm = x.shape[0]
    xb = (_rms(x, d) * g_ref[...]).astype(BF16)
    pos = pos_ref[0].astype(F32)
    ang_a = pos * inv_ref[0:1, :]
    ang_b = pos * inv_ref[1:2, :]
    cos_a, sin_a = jnp.cos(ang_a), jnp.sin(ang_a) * sgn_ref[0:1, :]
    cos_b, sin_b = jnp.cos(ang_b), jnp.sin(ang_b) * sgn_ref[1:2, :]
    lane = lax.broadcasted_iota(jnp.int32, (tm, LANES), 1)
    lo = lane < 64
    first_b = (lane & 63) < 16
    head0 = (lane & 32) == 0

    q = _dot(xb, w_main_ref[:, 0 * width:1 * width])
    k = _dot(xb, w_main_ref[:, 1 * width:2 * width])
    v = _dot(xb, w_main_ref[:, 2 * width:3 * width])
    z = _dot(xb, w_main_ref[:, 3 * width:4 * width])
    z_ref[0] = z.astype(BF16)

    def head_norm_rope(t, gain):
        sq = t * t
        s0 = jnp.sum(jnp.where(head0, sq, 0.0), axis=-1, keepdims=True)
        s1 = jnp.sum(jnp.where(head0, 0.0, sq), axis=-1, keepdims=True)
        ms = jnp.where(head0, s0, s1) * (1.0 / 64)
        t = t * lax.rsqrt(ms + EPS) * gain
        return t * cos_a + pltpu.roll(t, 64, 1) * sin_a

    for j in range(n_pair):
        sl = slice(j * LANES, (j + 1) * LANES)
        qt = head_norm_rope(q[:, sl], qg_ref[...]) * scale
        q_ref[0, 2 * j] = jnp.where(head0, qt, 0.0).astype(BF16)
        q_ref[0, 2 * j + 1] = jnp.where(head0, 0.0, qt).astype(BF16)
        k_ref[0, j] = head_norm_rope(k[:, sl], kg_ref[...]).astype(BF16)
        v_ref[0, j] = v[:, sl].astype(BF16)

    qi = _dot(xb, w_qi_ref[...])
    for j in range(n_qi_tile):
        t = _rope_tile(qi[:, j * LANES:(j + 1) * LANES], cos_b, sin_b, first_b, 16)
        qi_ref[0, 2 * j] = jnp.where(lo, t, 0.0).astype(BF16)
        qi_ref[0, 2 * j + 1] = jnp.where(lo, 0.0, t).astype(BF16)

    kiwi = _dot(xb, w_kiwi_ref[...])
    ki_ref[0] = _rope_tile(kiwi[:, 0:LANES], cos_b, sin_b, first_b, 16).astype(BF16)
    wi_ref[0] = kiwi[:, LANES:2 * LANES] * wi_scale


def _key_to_float(key):
    return pltpu.bitcast(jnp.where(key >= 0, key, key ^ 0x7FFFFFFF), F32)


def _search_init(tq):
    rg = tq // SEARCH_ROW_GROUPS
    return tuple(jnp.full((rg, 1), INT_MIN, jnp.int32) for _ in range(SEARCH_ROW_GROUPS))


def _search_step(step, bases, score_scr, nt, topk):
    rg = bases[0].shape[0]
    bit = jnp.left_shift(jnp.int32(1), 31 - step)
    out = []
    for g, base_g in enumerate(bases):
        cand = base_g + bit
        hits = jnp.where(score_scr[0:nt, g * rg:(g + 1) * rg, :] >= _key_to_float(cand), 1.0, 0.0)
        n_ge = jnp.sum(jnp.sum(hits, axis=0), axis=1, keepdims=True)
        out.append(jnp.where(n_ge >= float(topk), cand, base_g))
    return tuple(out)


def _select_bias(bases, score_scr, bias_scr, slot, nt, tq, topk):
    topk_f = float(topk)

    def count(pred_fn):
        hits = jnp.where(pred_fn(score_scr[0:nt]), 1.0, 0.0)
        return jnp.sum(jnp.sum(hits, axis=0), axis=1, keepdims=True)

    thr = _key_to_float(jnp.concatenate(bases, axis=0))
    lowest = jnp.full((tq, 1), -F32_MAX, F32)
    thr = jnp.where(count(lambda s: s >= lowest) < topk_f, lowest, thr)
    excess = jnp.max(count(lambda s: s >= thr)) > topk_f

    def plain(_):
        bias_scr[slot, 0:nt] = jnp.where(score_scr[0:nt] >= thr, 0.0, NEG_BIG)
        return 0

    def index_ordered_ties(_):
        need = topk_f - count(lambda s: s > thr)
        r = lax.broadcasted_iota(jnp.int32, (LANES, LANES), 0)
        c = lax.broadcasted_iota(jnp.int32, (LANES, LANES), 1)
        upper = jnp.where(r < c, 1.0, 0.0).astype(BF16)

        def tile_step(t, carry):
            s = score_scr[t]
            tie_f = jnp.where(s == thr, 1.0, 0.0)
            before = _dot(tie_f.astype(BF16), upper) + carry
            tie_bias = jnp.where(s == thr, jnp.where(before < need, 0.0, NEG_BIG), NEG_BIG)
            bias_scr[slot, t] = jnp.where(s > thr, 0.0, tie_bias)
            return carry + jnp.sum(tie_f, axis=1, keepdims=True)

        lax.fori_loop(0, nt, tile_step, jnp.zeros((tq, 1), F32))
        return 0

    lax.cond(excess, index_ordered_ties, plain, 0)


def _dsa_attn_kernel(bounded_ref, q_ref, qi_ref, wi_ref, k_ref, v_ref, ki_ref, z_ref, x_ref, w_out_ref,
                     o_ref, score_scr, bias_scr, o_scr, wi_scr, *, tq, n_blocks, topk, n_idx_heads, n_pair,
                     unroll_small, unroll_large):
    qb = pl.program_id(1)
    bounded = bounded_ref[0, 0] != 0
    lo = lax.broadcasted_iota(jnp.int32, (tq, LANES), 1) < 64
    tq_tiles = tq // LANES
    n_tiles = score_scr.shape[0]
    steps_per_pair = 32 // n_pair

    def pair_out(j, nvis, slot, is_bounded):
        kt = k_ref[0, j, 0:nvis, :]
        vt = v_ref[0, j, 0:nvis, :]
        qq = q_ref[0, pl.ds(2 * j, 2)].reshape(2 * tq, LANES)
        logit = _dot_nt(qq, kt)
        bias = jnp.concatenate([bias_scr[slot, t] for t in range(nvis // LANES)], axis=1)
        o_even = _softmax_pv(logit[0:tq] + bias, vt, bounded=is_bounded)
        o_odd = _softmax_pv(logit[tq:2 * tq] + bias, vt, bounded=is_bounded)
        return jnp.where(lo, o_even, o_odd)

    def diag_bias_tiles(slot, nt):
        diag_vis = _diag_visible(tq)
        for d in range(tq_tiles):
            bias_scr[slot, nt - tq_tiles + d] = jnp.where(diag_vis[:, d * LANES:(d + 1) * LANES], 0.0, NEG_BIG)

    wi = wi_ref[0]
    for h in range(n_idx_heads):
        wi_scr[h] = jnp.broadcast_to(wi[:, h:h + 1], (tq, LANES))

    def body(i):
        cur, nxt = i % 2, (i + 1) % 2
        nvis = (i + 1) * tq
        if i == 0:
            diag_bias_tiles(cur, nvis // LANES)
        nvis2 = nvis + tq
        nt2 = nvis2 // LANES
        search_next = i + 1 < n_blocks and nvis2 > topk
        if i + 1 < n_blocks and not search_next:
            diag_bias_tiles(nxt, nt2)
        if search_next:
            score_scr[0:nt2] = jnp.zeros((nt2, tq, LANES), F32)

            def idx_head(h, _):
                rel = jnp.maximum(_dot_nt(qi_ref[0, h], ki_ref[0, 0:nvis2, :]), 0.0)
                w = wi_scr[h]
                for t in range(nt2):
                    score_scr[t] += w * rel[:, t * LANES:(t + 1) * LANES]
                return 0

            lax.fori_loop(0, n_idx_heads, idx_head, 0, unroll=2)
            diag_vis = _diag_visible(tq)
            for d in range(tq_tiles):
                t = nt2 - tq_tiles + d
                score_scr[t] = jnp.where(diag_vis[:, d * LANES:(d + 1) * LANES], score_scr[t], -jnp.inf)

        def pair_step(j, bases):
            o_scr[j] = pair_out(j, nvis, cur, True)
            for s_ in range(steps_per_pair if search_next else 0):
                bases = _search_step(j * steps_per_pair + s_, bases, score_scr, nt2, topk)
            return bases

        bases = lax.fori_loop(0, n_pair, pair_step, _search_init(tq) if search_next else (),
                              unroll=unroll_small if 2 * i < n_blocks else unroll_large)
        if search_next:
            _select_bias(bases, score_scr, bias_scr, nxt, nt2, tq, topk)

    _per_block_variants(qb, n_blocks, body)

    @pl.when(jnp.logical_not(bounded))
    def _():
        slot = qb % 2

        def mask_tile(t, _):
            bias_scr[slot, t] = jnp.full((tq, LANES), NEG_BIG, F32)
            return 0

        lax.fori_loop((qb + 1) * tq_tiles, n_tiles, mask_tile, 0)

        def pair_step(j, _):
            o_scr[j] = pair_out(j, n_tiles * LANES, slot, False)
            return 0

        lax.fori_loop(0, n_pair, pair_step, 0)

    o = jnp.concatenate([o_scr[j] for j in range(n_pair)], axis=1)
    g = (o * _silu(z_ref[0].astype(F32))).astype(BF16)
    o_ref[0] = x_ref[0] + _dot(g, w_out_ref[...])


def _rope_inv(half):
    return ROPE_THETA ** (-jnp.arange(half, dtype=F32) / half)


def _dsa_layer(x, pos3, g, w_in, q_norm, k_norm, w_out, *, tm=256, tq=256):
    b, s, d = x.shape
    width = w_out.shape[0]
    head_dim = q_norm.shape[0]
    assert head_dim == 64 and width % LANES == 0
    n_pair = width // LANES
    idx_dim = 64
    n_idx_heads = (w_in.shape[1] - 4 * width - idx_dim) // (idx_dim + 1)
    assert 4 * width + n_idx_heads * idx_dim + idx_dim + n_idx_heads == w_in.shape[1]
    n_qi_tile = n_idx_heads * idx_dim // LANES
    c0 = 4 * width
    c1 = c0 + n_idx_heads * idx_dim
    c2 = c1 + idx_dim
    half = head_dim // 2

    def pair_split(t):
        t4 = t.reshape(t.shape[:-1] + (n_pair, 2, 2, half))
        return jnp.swapaxes(t4, -3, -2).reshape(t.shape)

    w_main = jnp.concatenate([pair_split(w_in[:, :width]), pair_split(w_in[:, width:2 * width]),
                              w_in[:, 2 * width:c0]], axis=1).astype(BF16)
    w_qi = w_in[:, c0:c1].astype(BF16)
    w_ki = w_in[:, c1:c2]
    w_wi = jnp.pad(w_in[:, c2:], ((0, 0), (0, LANES - n_idx_heads)))
    w_kiwi = jnp.concatenate([w_ki, w_ki, w_wi], axis=1).astype(BF16)

    inv32 = _rope_inv(32)
    inv16 = _rope_inv(16)
    zeros32 = jnp.zeros((32,), F32)
    inv_a = jnp.tile(inv32, 4)
    inv_b = jnp.tile(jnp.concatenate([inv16, inv16, zeros32]), 2)
    inv = jnp.stack([inv_a, inv_b])
    sgn_a = jnp.concatenate([-jnp.ones((64,), F32), jnp.ones((64,), F32)])
    sgn_b = jnp.tile(jnp.concatenate([-jnp.ones((16,), F32), jnp.ones((48,), F32)]), 2)
    sgn = jnp.stack([sgn_a, sgn_b])

    def pair_gain(gn):
        return jnp.concatenate([gn[:half], gn[:half], gn[half:], gn[half:]]).reshape(1, LANES)

    qg = pair_gain(q_norm)
    kg = pair_gain(k_norm)

    topk = min(TOPK_MAX, s // 4)
    assert s % tq == 0 and tq % (1 << CHUNK_SHIFT) == 0
    n_heads = 2 * n_pair
    row = lambda bi, i: (bi, i, 0)
    head_row = lambda bi, i: (bi, 0, i, 0)
    q, k, v, z, qi, ki, wi = pl.pallas_call(
        functools.partial(_dsa_proj_kernel, width=width, n_pair=n_pair, n_qi_tile=n_qi_tile,
                          scale=head_dim ** -0.5 * LOG2E,
                          wi_scale=n_idx_heads ** -0.5 * idx_dim ** -0.5),
        grid=(b, s // tm),
        in_specs=[
            pl.BlockSpec((1, tm, d), row),
            pl.BlockSpec((1, tm, 1), row),
            _const_spec((1, d)),
            _const_spec((d, c0)),
            _const_spec((d, c1 - c0)),
            _const_spec((d, 3 * LANES)),
            _const_spec((1, LANES)),
            _const_spec((1, LANES)),
            _const_spec((2, LANES)),
            _const_spec((2, LANES)),
        ],
        out_specs=[
            pl.BlockSpec((1, n_heads, tm, LANES), head_row),
            pl.BlockSpec((1, n_pair, tm, LANES), head_row),
            pl.BlockSpec((1, n_pair, tm, LANES), head_row),
            pl.BlockSpec((1, tm, width), row),
            pl.BlockSpec((1, n_idx_heads, tm, LANES), head_row),
            pl.BlockSpec((1, tm, LANES), row),
            pl.BlockSpec((1, tm, LANES), row),
        ],
        out_shape=[
            jax.ShapeDtypeStruct((b, n_heads, s, LANES), BF16),
            jax.ShapeDtypeStruct((b, n_pair, s, LANES), BF16),
            jax.ShapeDtypeStruct((b, n_pair, s, LANES), BF16),
            jax.ShapeDtypeStruct((b, s, width), BF16),
            jax.ShapeDtypeStruct((b, n_idx_heads, s, LANES), BF16),
            jax.ShapeDtypeStruct((b, s, LANES), BF16),
            jax.ShapeDtypeStruct((b, s, LANES), F32),
        ],
        compiler_params=_params(("arbitrary", "arbitrary")),
        name="dsa_proj",
    )(x, pos3, g.reshape(1, d), w_main, w_qi, w_kiwi, qg, kg, inv, sgn)

    qrow = lambda bi, i: (bi, i, 0)
    qhead = lambda bi, i: (bi, 0, i, 0)
    full = lambda bi, i: (bi, 0, 0, 0)
    n_blocks = s // tq
    assert tq <= topk and 32 % n_pair == 0
    nxt_head = lambda bi, i: (bi, 0, jnp.minimum(i + 1, n_blocks - 1), 0)
    nxt_row = lambda bi, i: (bi, jnp.minimum(i + 1, n_blocks - 1), 0)
    return pl.pallas_call(
        functools.partial(_dsa_attn_kernel, tq=tq, n_blocks=n_blocks, topk=topk,
                          n_idx_heads=n_idx_heads, n_pair=n_pair, unroll_small=2, unroll_large=1),
        grid=(b, n_blocks),
        in_specs=[
            pl.BlockSpec(memory_space=pltpu.SMEM),
            pl.BlockSpec((1, n_heads, tq, LANES), qhead),
            pl.BlockSpec((1, n_idx_heads, tq, LANES), nxt_head),
            pl.BlockSpec((1, tq, LANES), nxt_row),
            pl.BlockSpec((1, n_pair, s, LANES), full),
            pl.BlockSpec((1, n_pair, s, LANES), full),
            pl.BlockSpec((1, s, LANES), lambda bi, i: (bi, 0, 0)),
            pl.BlockSpec((1, tq, width), qrow),
            pl.BlockSpec((1, tq, d), qrow),
            _single((width, d), lambda bi, i: (0, 0)),
        ],
        out_specs=pl.BlockSpec((1, tq, d), qrow),
        out_shape=jax.ShapeDtypeStruct((b, s, d), F32),
        scratch_shapes=[
            pltpu.VMEM((s // LANES, tq, LANES), F32),
            pltpu.VMEM((2, s // LANES, tq, LANES), F32),
            pltpu.VMEM((n_pair, tq, LANES), F32),
            pltpu.VMEM((n_idx_heads, tq, LANES), F32),
        ],
        compiler_params=_params(("arbitrary", "arbitrary")),
        name="dsa_attn",
    )(_logits_bounded(q_norm, k_norm, head_dim, head_dim ** -0.5 * LOG2E),
      q, qi, wi, k, v, ki, z, x, w_out.astype(BF16))


def _mla_proj_kernel(x_ref, pos_ref, g_ref, w_in_ref, qlat_g_ref, kvlat_g_ref, w_uq_ref, w_uq_rot_ref,
                     w_uk_ref, w_uv_ref, gains_ref, inv_ref, sgn_ref,
                     q_ref, k_ref, v_ref, z_ref,
                     *, q_lora, kv_lora, n_heads, qk_dim, scale):
    x = x_ref[0]
    d = x.shape[-1]
    xb = (_rms(x, d) * g_ref[...]).astype(BF16)
    pos = pos_ref[0].astype(F32)
    ang = pos * inv_ref[...]
    cos_t, sin_t = jnp.cos(ang), jnp.sin(ang) * sgn_ref[...]
    cos_q = cos_t * (gains_ref[0:1, :] * scale)
    sin_q = sin_t * (gains_ref[1:2, :] * scale)
    kg = gains_ref[2:3, :]

    c1 = q_lora + kv_lora
    cq = _dot(xb, w_in_ref[:, 0:q_lora])
    ckv = _dot(xb, w_in_ref[:, q_lora:c1])
    kr = _dot(xb, w_in_ref[:, c1:c1 + LANES])
    kr_rot = _dot(xb, w_in_ref[:, c1 + LANES:c1 + 2 * LANES])
    z = _dot(xb, w_in_ref[:, c1 + 2 * LANES:])
    z_ref[0] = z.astype(BF16)

    cq_b = (_rms(cq, q_lora) * qlat_g_ref[...]).astype(BF16)
    q = _dot(cq_b, w_uq_ref[...])
    q_rot = _dot(cq_b, w_uq_rot_ref[...])
    ckv_b = (_rms(ckv, kv_lora) * kvlat_g_ref[...]).astype(BF16)
    kn = _dot(ckv_b, w_uk_ref[...])
    v = _dot(ckv_b, w_uv_ref[...])

    k_rope = kr * (kg * cos_t) + kr_rot * (gains_ref[3:4, :] * sin_t)
    kr_ss = jnp.sum(kr * kr, axis=-1, keepdims=True)
    inv_n = 1.0 / qk_dim
    for h in range(n_heads):
        sl = slice(h * LANES, (h + 1) * LANES)
        qh = q[:, sl]
        q_s = lax.rsqrt(jnp.sum(qh * qh, axis=-1, keepdims=True) * inv_n + EPS)
        q_ref[0, h] = (q_s * (qh * cos_q + q_rot[:, sl] * sin_q)).astype(BF16)
        kh = kn[:, sl]
        k_s = lax.rsqrt((jnp.sum(kh * kh, axis=-1, keepdims=True) + kr_ss) * inv_n + EPS)
        k_ref[0, h] = (k_s * (kh * kg + k_rope)).astype(BF16)
    for j in range(n_heads // 2):
        v_ref[0, j] = v[:, j * LANES:(j + 1) * LANES].astype(BF16)


def _mla_attn_kernel(bounded_ref, q_ref, k_ref, v_ref, z_ref, x_ref, w_out_ref, o_ref, o_scr,
                     *, tq, seq, n_blocks, n_pair):
    qb = pl.program_id(1)
    bounded = bounded_ref[0, 0] != 0
    lo = lax.broadcasted_iota(jnp.int32, (tq, LANES), 1) < 64

    def body(i):
        nvis = (i + 1) * tq
        diag_bias = jnp.where(_diag_visible(tq), 0.0, NEG_BIG)

        def head_out(h, vt):
            logit = _dot_nt(q_ref[0, h], k_ref[0, h, 0:nvis, :])
            last = logit[:, nvis - tq:nvis] + diag_bias
            if nvis > tq:
                logit = jnp.concatenate([logit[:, 0:nvis - tq], last], axis=1)
            else:
                logit = last
            return _softmax_pv(logit, vt, bounded=True)

        def pair_step(j, _):
            vt = v_ref[0, j, 0:nvis, :]
            o_scr[j] = jnp.where(lo, head_out(2 * j, vt), head_out(2 * j + 1, vt))
            return 0

        lax.fori_loop(0, n_pair, pair_step, 0, unroll=4)

    @pl.when(bounded)
    def _():
        _per_block_variants(qb, n_blocks, body)

    @pl.when(jnp.logical_not(bounded))
    def _():
        s_idx = lax.broadcasted_iota(jnp.int32, (tq, seq), 1)
        q_chunk = (qb * tq + lax.broadcasted_iota(jnp.int32, (tq, seq), 0)) >> CHUNK_SHIFT
        bias = jnp.where((s_idx >> CHUNK_SHIFT) <= q_chunk, 0.0, NEG_BIG)

        def pair_step(j, _):
            vt = v_ref[0, j]
            outs = [_softmax_pv(_dot_nt(q_ref[0, 2 * j + e], k_ref[0, 2 * j + e]) + bias, vt, bounded=False)
                    for e in range(2)]
            o_scr[j] = jnp.where(lo, outs[0], outs[1])
            return 0

        lax.fori_loop(0, n_pair, pair_step, 0)

    o = jnp.concatenate([o_scr[j] for j in range(n_pair)], axis=1)
    g = (o * _silu(z_ref[0].astype(F32))).astype(BF16)
    o_ref[0] = x_ref[0] + _dot(g, w_out_ref[...])


def _mla_layer(x, pos3, g, w_in, q_lat_norm, kv_lat_norm, w_uq, w_ukv, q_norm, k_norm, w_out,
               *, tm=256, tq=256):
    b, s, d = x.shape
    width = w_out.shape[0]
    q_lora = q_lat_norm.shape[0]
    kv_lora = kv_lat_norm.shape[0]
    qk_dim = q_norm.shape[0]
    v_dim = 64
    nope = 64
    rope_dim = qk_dim - nope
    n_heads = width // v_dim
    assert nope + rope_dim <= LANES and w_ukv.shape[1] == n_heads * (nope + v_dim)
    assert w_in.shape[1] == q_lora + kv_lora + rope_dim + width
    assert s % tq == 0 and tq % (1 << CHUNK_SHIFT) == 0
    pad = LANES - qk_dim

    rope_half = rope_dim // 2

    def rope_lanes(t):
        return jnp.pad(t, [(0, 0)] * (t.ndim - 1) + [(nope, LANES - nope - rope_dim)])

    def partner(t):
        return jnp.concatenate([t[..., rope_half:], t[..., :rope_half]], axis=-1)

    c1 = q_lora + kv_lora
    w_kr = w_in[:, c1:c1 + rope_dim]
    w_in_p = jnp.concatenate([w_in[:, :c1], rope_lanes(w_kr), rope_lanes(partner(w_kr)),
                              w_in[:, c1 + rope_dim:]], axis=1).astype(BF16)
    w_uq3 = w_uq.reshape(q_lora, n_heads, qk_dim)
    w_uq_p = jnp.pad(w_uq3, ((0, 0), (0, 0), (0, pad))).reshape(q_lora, n_heads * LANES).astype(BF16)
    w_uq_rot = rope_lanes(partner(w_uq3[:, :, nope:])).reshape(q_lora, n_heads * LANES).astype(BF16)
    w_ukv3 = w_ukv.reshape(kv_lora, n_heads, nope + v_dim)
    w_uk_p = jnp.pad(w_ukv3[:, :, :nope], ((0, 0), (0, 0), (0, LANES - nope)))
    w_uk_p = w_uk_p.reshape(kv_lora, n_heads * LANES).astype(BF16)
    w_uv = w_ukv3[:, :, nope:].reshape(kv_lora, n_heads * v_dim).astype(BF16)
    gains = jnp.stack([jnp.pad(q_norm, (0, pad)), rope_lanes(partner(q_norm[nope:])),
                       jnp.pad(k_norm, (0, pad)), rope_lanes(partner(k_norm[nope:]))])

    inv_h = _rope_inv(rope_half)
    inv = jnp.concatenate([jnp.zeros((nope,), F32), inv_h, inv_h,
                           jnp.zeros((LANES - nope - rope_dim,), F32)]).reshape(1, LANES)
    sgn = jnp.concatenate([jnp.ones((nope,), F32), -jnp.ones((rope_half,), F32),
                           jnp.ones((LANES - nope - rope_half,), F32)]).reshape(1, LANES)

    n_pair = n_heads // 2
    row = lambda bi, i: (bi, i, 0)
    head_row = lambda bi, i: (bi, 0, i, 0)
    n_in = w_in_p.shape[1]
    q, k, v, z = pl.pallas_call(
        functools.partial(_mla_proj_kernel, q_lora=q_lora, kv_lora=kv_lora, n_heads=n_heads,
                          qk_dim=qk_dim, scale=qk_dim ** -0.5 * LOG2E),
        grid=(b, s // tm),
        in_specs=[
            pl.BlockSpec((1, tm, d), row),
            pl.BlockSpec((1, tm, 1), row),
            _const_spec((1, d)),
            _const_spec((d, n_in)),
            _const_spec((1, q_lora)),
            _const_spec((1, kv_lora)),
            _const_spec((q_lora, n_heads * LANES)),
            _const_spec((q_lora, n_heads * LANES)),
            _const_spec((kv_lora, n_heads * LANES)),
            _const_spec((kv_lora, n_heads * v_dim)),
            _const_spec((4, LANES)),
            _const_spec((1, LANES)),
            _const_spec((1, LANES)),
        ],
        out_specs=[
            pl.BlockSpec((1, n_heads, tm, LANES), head_row),
            pl.BlockSpec((1, n_heads, tm, LANES), head_row),
            pl.BlockSpec((1, n_pair, tm, LANES), head_row),
            pl.BlockSpec((1, tm, width), row),
        ],
        out_shape=[
            jax.ShapeDtypeStruct((b, n_heads, s, LANES), BF16),
            jax.ShapeDtypeStruct((b, n_heads, s, LANES), BF16),
            jax.ShapeDtypeStruct((b, n_pair, s, LANES), BF16),
            jax.ShapeDtypeStruct((b, s, width), BF16),
        ],
        compiler_params=_params(("arbitrary", "arbitrary")),
        name="mla_proj",
    )(x, pos3, g.reshape(1, d), w_in_p, q_lat_norm.reshape(1, q_lora), kv_lat_norm.reshape(1, kv_lora),
      w_uq_p, w_uq_rot, w_uk_p, w_uv, gains, inv, sgn)

    qrow = lambda bi, i: (bi, i, 0)
    qhead = lambda bi, i: (bi, 0, i, 0)
    full = lambda bi, i: (bi, 0, 0, 0)
    return pl.pallas_call(
        functools.partial(_mla_attn_kernel, tq=tq, seq=s, n_blocks=s // tq, n_pair=n_pair),
        grid=(b, s // tq),
        in_specs=[
            pl.BlockSpec(memory_space=pltpu.SMEM),
            pl.BlockSpec((1, n_heads, tq, LANES), qhead),
            pl.BlockSpec((1, n_heads, s, LANES), full),
            pl.BlockSpec((1, n_pair, s, LANES), full),
            pl.BlockSpec((1, tq, width), qrow),
            pl.BlockSpec((1, tq, d), qrow),
            _single((width, d), lambda bi, i: (0, 0)),
        ],
        out_specs=pl.BlockSpec((1, tq, d), qrow),
        out_shape=jax.ShapeDtypeStruct((b, s, d), F32),
        scratch_shapes=[
            pltpu.VMEM((n_pair, tq, LANES), F32),
        ],
        compiler_params=_params(("arbitrary", "arbitrary")),
        name="mla_attn",
    )(_logits_bounded(q_norm, k_norm, qk_dim, qk_dim ** -0.5 * LOG2E), q, k, v, z, x, w_out.astype(BF16))


def kernel(x, positions, a_norm, a_w_in, a_conv_w, a_conv_b, a_w_out, b_norm, b_w_in, b_q_norm, b_k_norm, b_w_out, c_norm, c_w_in, c_q_lat_norm, c_kv_lat_norm, c_w_uq, c_w_ukv, c_q_norm, c_k_norm, c_w_out):
    depth = a_norm.shape[0] + b_norm.shape[0] + c_norm.shape[0]
    pos3 = positions.reshape(positions.shape + (1,))
    for i in range(depth):
        kind, j = i % 3, i // 3
        if kind == 0:
            x = _conv_layer(x, j, a_norm[j], a_w_in, a_conv_w[j], a_conv_b[j], a_w_out)
        elif kind == 1:
            x = _dsa_layer(x, pos3, b_norm[j], b_w_in[j], b_q_norm[j], b_k_norm[j], b_w_out[j])
        else:
            x = _mla_layer(x, pos3, c_norm[j], c_w_in[j], c_q_lat_norm[j], c_kv_lat_norm[j],
                           c_w_uq[j], c_w_ukv[j], c_q_norm[j], c_k_norm[j], c_w_out[j])
    return x
```

```python
import functools
import math

import jax
import jax.numpy as jnp
from jax import lax
from jax.experimental import pallas as pl
from jax.experimental.pallas import tpu as pltpu

EPS = 1e-6
ROPE_THETA = 10000.0
CHUNK_SHIFT = 6
TOPK_MAX = 256
LANES = 128
NEG_BIG = -1e30
INT_MIN = -(2 ** 31)
F32_MAX = 3.4028234663852886e38
LOG2E = math.log2(math.e)
SEARCH_ROW_GROUPS = 4
LOGIT_BOUND = 64.0
VMEM_LIMIT = 56 * 1024 * 1024

BF16 = jnp.bfloat16
F32 = jnp.float32


def _dot(a, b):
    return jnp.dot(a, b, preferred_element_type=F32)


def _dot_nt(a, b):
    return lax.dot_general(a, b, (((1,), (1,)), ((), ())), preferred_element_type=F32)


def _rms(x, n):
    return x * lax.rsqrt(jnp.sum(x * x, axis=-1, keepdims=True) * (1.0 / n) + EPS)


def _silu(z):
    return z * jax.nn.sigmoid(z)


def _rope_tile(t, cos_t, sin_t, first_mask, half):
    partner = jnp.where(first_mask, pltpu.roll(t, LANES - half, 1), pltpu.roll(t, half, 1))
    return t * cos_t + partner * sin_t


def _params(sem):
    return pltpu.CompilerParams(dimension_semantics=sem, vmem_limit_bytes=VMEM_LIMIT)


def _single(shape, index_map):
    return pl.BlockSpec(shape, index_map, pipeline_mode=pl.Buffered(1))


def _const_spec(shape):
    nd = len(shape)
    return _single(shape, lambda *_: (0,) * nd)


def _diag_visible(tq):
    r = lax.broadcasted_iota(jnp.int32, (tq, tq), 0)
    c = lax.broadcasted_iota(jnp.int32, (tq, tq), 1)
    return (c >> CHUNK_SHIFT) <= (r >> CHUNK_SHIFT)


def _softmax_pv(logit, v, *, bounded):
    if not bounded:
        logit = logit - jnp.max(logit, axis=-1, keepdims=True)
    p = jnp.exp2(logit)
    den = jnp.sum(p, axis=-1, keepdims=True)
    return _dot(p.astype(BF16), v) / den


def _logits_bounded(q_gain, k_gain, n, scale):
    bound = 1.02 * n * scale * jnp.max(jnp.abs(q_gain)) * jnp.max(jnp.abs(k_gain))
    return (bound <= LOGIT_BOUND).astype(jnp.int32).reshape(1, 1)


def _per_block_variants(qb, n_blocks, body):
    for i in range(n_blocks):
        pl.when(qb == i)(functools.partial(body, i))


def _conv_kernel(x_ref, g_ref, w_in_f32_ref, cw_ref, cb_ref, w_out_f32_ref, o_ref, u_scr, w_in_ref, w_out_ref,
                 *, tm, width):
    j = pl.program_id(1)

    @pl.when(jnp.logical_and(pl.program_id(0) == 0, j == 0))
    def _():
        w_in_ref[...] = w_in_f32_ref[...].astype(BF16)
        w_out_ref[...] = w_out_f32_ref[...].astype(BF16)

    x = x_ref[0]
    d = x.shape[-1]
    xb = (_rms(x, d) * g_ref[...]).astype(BF16)

    @pl.when(j == 0)
    def _():
        u_scr[0:8, :] = jnp.zeros((8, width), F32)

    @pl.when(j > 0)
    def _():
        u_scr[0:8, :] = u_scr[tm:tm + 8, :]

    bg = _dot(xb, w_in_ref[:, 0 * width:1 * width])
    cg = _dot(xb, w_in_ref[:, 1 * width:2 * width])
    hv = _dot(xb, w_in_ref[:, 2 * width:3 * width])
    z = _dot(xb, w_in_ref[:, 3 * width:4 * width])
    u = cg * hv
    u_scr[8:tm + 8, :] = u
    y = (cw_ref[2:3, :] * u + cw_ref[1:2, :] * u_scr[7:7 + tm, :]
         + cw_ref[0:1, :] * u_scr[6:6 + tm, :] + cb_ref[...])
    g = (bg * y * _silu(z)).astype(BF16)
    o_ref[0] = x + _dot(g, w_out_ref[...])


def _conv_layer(x, layer, g, w_in_all, cw, cb, w_out_all, *, tm=256):
    b, s, d = x.shape
    width = w_out_all.shape[1]
    return pl.pallas_call(
        functools.partial(_conv_kernel, tm=tm, width=width),
        grid=(b, s // tm),
        in_specs=[
            pl.BlockSpec((1, tm, d), lambda bi, i: (bi, i, 0)),
            _const_spec((1, d)),
            _single((None, d, 4 * width), lambda bi, i: (layer, 0, 0)),
            _const_spec((3, width)),
            _const_spec((1, width)),
            _single((None, width, d), lambda bi, i: (layer, 0, 0)),
        ],
        out_specs=pl.BlockSpec((1, tm, d), lambda bi, i: (bi, i, 0)),
        out_shape=jax.ShapeDtypeStruct((b, s, d), F32),
        scratch_shapes=[
            pltpu.VMEM((tm + 8, width), F32),
            pltpu.VMEM((d, 4 * width), BF16),
            pltpu.VMEM((width, d), BF16),
        ],
        compiler_params=_params(("arbitrary", "arbitrary")),
        name="conv_mixer",
    )(x, g.reshape(1, d), w_in_all, cw, cb.reshape(1, width), w_out_all)


def _dsa_proj_kernel(x_ref, pos_ref, g_ref, w_main_ref, w_qi_ref, w_kiwi_ref, qg_ref, kg_ref,
                     inv_ref, sgn_ref,
                     q_ref, k_ref, v_ref, z_ref, qi_ref, ki_ref, wi_ref,
                     *, width, n_pair, n_qi_tile, scale, wi_scale):
    x = x_ref[0]
    d = x.shape[-1]
    tm = x.shape[0]
    xb = (_rms(x, d) * g_ref[...]).astype(BF16)
    pos = pos_ref[0].astype(F32)
    ang_a = pos * inv_ref[0:1, :]
    ang_b = pos * inv_ref[1:2, :]
    cos_a, sin_a = jnp.cos(ang_a), jnp.sin(ang_a) * sgn_ref[0:1, :]
    cos_b, sin_b = jnp.cos(ang_b), jnp.sin(ang_b) * sgn_ref[1:2, :]
    lane = lax.broadcasted_iota(jnp.int32, (tm, LANES), 1)
    lo = lane < 64
    first_b = (lane & 63) < 16
    head0 = (lane & 32) == 0

    q = _dot(xb, w_main_ref[:, 0 * width:1 * width])
    k = _dot(xb, w_main_ref[:, 1 * width:2 * width])
    v = _dot(xb, w_main_ref[:, 2 * width:3 * width])
    z = _dot(xb, w_main_ref[:, 3 * width:4 * width])
    z_ref[0] = z.astype(BF16)

    def head_norm_rope(t, gain):
        sq = t * t
        s0 = jnp.sum(jnp.where(head0, sq, 0.0), axis=-1, keepdims=True)
        s1 = jnp.sum(jnp.where(head0, 0.0, sq), axis=-1, keepdims=True)
        ms = jnp.where(head0, s0, s1) * (1.0 / 64)
        t = t * lax.rsqrt(ms + EPS) * gain
        return t * cos_a + pltpu.roll(t, 64, 1) * sin_a

    for j in range(n_pair):
        sl = slice(j * LANES, (j + 1) * LANES)
        qt = head_norm_rope(q[:, sl], qg_ref[...]) * scale
        q_ref[0, 2 * j] = jnp.where(head0, qt, 0.0).astype(BF16)
        q_ref[0, 2 * j + 1] = jnp.where(head0, 0.0, qt).astype(BF16)
        k_ref[0, j] = head_norm_rope(k[:, sl], kg_ref[...]).astype(BF16)
        v_ref[0, j] = v[:, sl].astype(BF16)

    qi = _dot(xb, w_qi_ref[...])
    for j in range(n_qi_tile):
        t = _rope_tile(qi[:, j * LANES:(j + 1) * LANES], cos_b, sin_b, first_b, 16)
        qi_ref[0, 2 * j] = jnp.where(lo, t, 0.0).astype(BF16)
        qi_ref[0, 2 * j + 1] = jnp.where(lo, 0.0, t).astype(BF16)

    kiwi = _dot(xb, w_kiwi_ref[...])
    ki_ref[0] = _rope_tile(kiwi[:, 0:LANES], cos_b, sin_b, first_b, 16).astype(BF16)
    wi_ref[0] = kiwi[:, LANES:2 * LANES] * wi_scale


def _key_to_float(key):
    return pltpu.bitcast(jnp.where(key >= 0, key, key ^ 0x7FFFFFFF), F32)


def _search_init(tq):
    rg = tq // SEARCH_ROW_GROUPS
    return tuple(jnp.full((rg, 1), INT_MIN, jnp.int32) for _ in range(SEARCH_ROW_GROUPS))


def _search_step(step, bases, score_scr, nt, topk):
    rg = bases[0].shape[0]
    bit = jnp.left_shift(jnp.int32(1), 31 - step)
    out = []
    for g, base_g in enumerate(bases):
        cand = base_g + bit
        hits = jnp.where(score_scr[0:nt, g * rg:(g + 1) * rg, :] >= _key_to_float(cand), 1.0, 0.0)
        n_ge = jnp.sum(jnp.sum(hits, axis=0), axis=1, keepdims=True)
        out.append(jnp.where(n_ge >= float(topk), cand, base_g))
    return tuple(out)


def _select_bias(bases, score_scr, bias_scr, slot, nt, tq, topk):
    topk_f = float(topk)

    def count(pred_fn):
        hits = jnp.where(pred_fn(score_scr[0:nt]), 1.0, 0.0)
        return jnp.sum(jnp.sum(hits, axis=0), axis=1, keepdims=True)

    thr = _key_to_float(jnp.concatenate(bases, axis=0))
    lowest = jnp.full((tq, 1), -F32_MAX, F32)
    thr = jnp.where(count(lambda s: s >= lowest) < topk_f, lowest, thr)
    excess = jnp.max(count(lambda s: s >= thr)) > topk_f

    def plain(_):
        bias_scr[slot, 0:nt] = jnp.where(score_scr[0:nt] >= thr, 0.0, NEG_BIG)
        return 0

    def index_ordered_ties(_):
        need = topk_f - count(lambda s: s > thr)
        r = lax.broadcasted_iota(jnp.int32, (LANES, LANES), 0)
        c = lax.broadcasted_iota(jnp.int32, (LANES, LANES), 1)
        upper = jnp.where(r < c, 1.0, 0.0).astype(BF16)

        def tile_step(t, carry):
            s = score_scr[t]
            tie_f = jnp.where(s == thr, 1.0, 0.0)
            before = _dot(tie_f.astype(BF16), upper) + carry
            tie_bias = jnp.where(s == thr, jnp.where(before < need, 0.0, NEG_BIG), NEG_BIG)
            bias_scr[slot, t] = jnp.where(s > thr, 0.0, tie_bias)
            return carry + jnp.sum(tie_f, axis=1, keepdims=True)

        lax.fori_loop(0, nt, tile_step, jnp.zeros((tq, 1), F32))
        return 0

    lax.cond(excess, index_ordered_ties, plain, 0)


def _dsa_attn_kernel(bounded_ref, q_ref, qi_ref, wi_ref, k_ref, v_ref, ki_ref, z_ref, x_ref, w_out_ref,
                     o_ref, score_scr, bias_scr, o_scr, wi_scr, *, tq, n_blocks, topk, n_idx_heads, n_pair,
                     unroll_small, unroll_large):
    qb = pl.program_id(1)
    bounded = bounded_ref[0, 0] != 0
    lo = lax.broadcasted_iota(jnp.int32, (tq, LANES), 1) < 64
    tq_tiles = tq // LANES
    n_tiles = score_scr.shape[0]
    steps_per_pair = 32 // n_pair

    def pair_out(j, nvis, slot, is_bounded):
        kt = k_ref[0, j, 0:nvis, :]
        vt = v_ref[0, j, 0:nvis, :]
        qq = q_ref[0, pl.ds(2 * j, 2)].reshape(2 * tq, LANES)
        logit = _dot_nt(qq, kt)
        bias = jnp.concatenate([bias_scr[slot, t] for t in range(nvis // LANES)], axis=1)
        o_even = _softmax_pv(logit[0:tq] + bias, vt, bounded=is_bounded)
        o_odd = _softmax_pv(logit[tq:2 * tq] + bias, vt, bounded=is_bounded)
        return jnp.where(lo, o_even, o_odd)

    def diag_bias_tiles(slot, nt):
        diag_vis = _diag_visible(tq)
        for d in range(tq_tiles):
            bias_scr[slot, nt - tq_tiles + d] = jnp.where(diag_vis[:, d * LANES:(d + 1) * LANES], 0.0, NEG_BIG)

    wi = wi_ref[0]
    for h in range(n_idx_heads):
        wi_scr[h] = jnp.broadcast_to(wi[:, h:h + 1], (tq, LANES))

    def body(i):
        cur, nxt = i % 2, (i + 1) % 2
        nvis = (i + 1) * tq
        if i == 0:
            diag_bias_tiles(cur, nvis // LANES)
        nvis2 = nvis + tq
        nt2 = nvis2 // LANES
        search_next = i + 1 < n_blocks and nvis2 > topk
        if i + 1 < n_blocks and not search_next:
            diag_bias_tiles(nxt, nt2)
        if search_next:
            score_scr[0:nt2] = jnp.zeros((nt2, tq, LANES), F32)

            def idx_head(h, _):
                rel = jnp.maximum(_dot_nt(qi_ref[0, h], ki_ref[0, 0:nvis2, :]), 0.0)
                w = wi_scr[h]
                for t in range(nt2):
                    score_scr[t] += w * rel[:, t * LANES:(t + 1) * LANES]
                return 0

            lax.fori_loop(0, n_idx_heads, idx_head, 0, unroll=2)
            diag_vis = _diag_visible(tq)
            for d in range(tq_tiles):
                t = nt2 - tq_tiles + d
                score_scr[t] = jnp.where(diag_vis[:, d * LANES:(d + 1) * LANES], score_scr[t], -jnp.inf)

        def pair_step(j, bases):
            o_scr[j] = pair_out(j, nvis, cur, True)
            for s_ in range(steps_per_pair if search_next else 0):
                bases = _search_step(j * steps_per_pair + s_, bases, score_scr, nt2, topk)
            return bases

        bases = lax.fori_loop(0, n_pair, pair_step, _search_init(tq) if search_next else (),
                              unroll=unroll_small if 2 * i < n_blocks else unroll_large)
        if search_next:
            _select_bias(bases, score_scr, bias_scr, nxt, nt2, tq, topk)

    _per_block_variants(qb, n_blocks, body)

    @pl.when(jnp.logical_not(bounded))
    def _():
        slot = qb % 2

        def mask_tile(t, _):
            bias_scr[slot, t] = jnp.full((tq, LANES), NEG_BIG, F32)
            return 0

        lax.fori_loop((qb + 1) * tq_tiles, n_tiles, mask_tile, 0)

        def pair_step(j, _):
            o_scr[j] = pair_out(j, n_tiles * LANES, slot, False)
            return 0

        lax.fori_loop(0, n_pair, pair_step, 0)

    o = jnp.concatenate([o_scr[j] for j in range(n_pair)], axis=1)
    g = (o * _silu(z_ref[0].astype(F32))).astype(BF16)
    o_ref[0] = x_ref[0] + _dot(g, w_out_ref[...])


def _rope_inv(half):
    return ROPE_THETA ** (-jnp.arange(half, dtype=F32) / half)


def _dsa_layer(x, pos3, g, w_in, q_norm, k_norm, w_out, *, tm=512, tq=256):
    b, s, d = x.shape
    width = w_out.shape[0]
    head_dim = q_norm.shape[0]
    assert head_dim == 64 and width % LANES == 0
    n_pair = width // LANES
    idx_dim = 64
    n_idx_heads = (w_in.shape[1] - 4 * width - idx_dim) // (idx_dim + 1)
    assert 4 * width + n_idx_heads * idx_dim + idx_dim + n_idx_heads == w_in.shape[1]
    n_qi_tile = n_idx_heads * idx_dim // LANES
    c0 = 4 * width
    c1 = c0 + n_idx_heads * idx_dim
    c2 = c1 + idx_dim
    half = head_dim // 2

    def pair_split(t):
        t4 = t.reshape(t.shape[:-1] + (n_pair, 2, 2, half))
        return jnp.swapaxes(t4, -3, -2).reshape(t.shape)

    w_main = jnp.concatenate([pair_split(w_in[:, :width]), pair_split(w_in[:, width:2 * width]),
                              w_in[:, 2 * width:c0]], axis=1).astype(BF16)
    w_qi = w_in[:, c0:c1].astype(BF16)
    w_ki = w_in[:, c1:c2]
    w_wi = jnp.pad(w_in[:, c2:], ((0, 0), (0, LANES - n_idx_heads)))
    w_kiwi = jnp.concatenate([w_ki, w_ki, w_wi], axis=1).astype(BF16)

    inv32 = _rope_inv(32)
    inv16 = _rope_inv(16)
    zeros32 = jnp.zeros((32,), F32)
    inv_a = jnp.tile(inv32, 4)
    inv_b = jnp.tile(jnp.concatenate([inv16, inv16, zeros32]), 2)
    inv = jnp.stack([inv_a, inv_b])
    sgn_a = jnp.concatenate([-jnp.ones((64,), F32), jnp.ones((64,), F32)])
    sgn_b = jnp.tile(jnp.concatenate([-jnp.ones((16,), F32), jnp.ones((48,), F32)]), 2)
    sgn = jnp.stack([sgn_a, sgn_b])

    def pair_gain(gn):
        return jnp.concatenate([gn[:half], gn[:half], gn[half:], gn[half:]]).reshape(1, LANES)

    qg = pair_gain(q_norm)
    kg = pair_gain(k_norm)

    topk = min(TOPK_MAX, s // 4)
    assert s % tq == 0 and tq % (1 << CHUNK_SHIFT) == 0
    n_heads = 2 * n_pair
    row = lambda bi, i: (bi, i, 0)
    head_row = lambda bi, i: (bi, 0, i, 0)
    q, k, v, z, qi, ki, wi = pl.pallas_call(
        functools.partial(_dsa_proj_kernel, width=width, n_pair=n_pair, n_qi_tile=n_qi_tile,
                          scale=head_dim ** -0.5 * LOG2E,
                          wi_scale=n_idx_heads ** -0.5 * idx_dim ** -0.5),
        grid=(b, s // tm),
        in_specs=[
            pl.BlockSpec((1, tm, d), row),
            pl.BlockSpec((1, tm, 1), row),
            _const_spec((1, d)),
            _const_spec((d, c0)),
            _const_spec((d, c1 - c0)),
            _const_spec((d, 3 * LANES)),
            _const_spec((1, LANES)),
            _const_spec((1, LANES)),
            _const_spec((2, LANES)),
            _const_spec((2, LANES)),
        ],
        out_specs=[
            pl.BlockSpec((1, n_heads, tm, LANES), head_row),
            pl.BlockSpec((1, n_pair, tm, LANES), head_row),
            pl.BlockSpec((1, n_pair, tm, LANES), head_row),
            pl.BlockSpec((1, tm, width), row),
            pl.BlockSpec((1, n_idx_heads, tm, LANES), head_row),
            pl.BlockSpec((1, tm, LANES), row),
            pl.BlockSpec((1, tm, LANES), row),
        ],
        out_shape=[
            jax.ShapeDtypeStruct((b, n_heads, s, LANES), BF16),
            jax.ShapeDtypeStruct((b, n_pair, s, LANES), BF16),
            jax.ShapeDtypeStruct((b, n_pair, s, LANES), BF16),
            jax.ShapeDtypeStruct((b, s, width), BF16),
            jax.ShapeDtypeStruct((b, n_idx_heads, s, LANES), BF16),
            jax.ShapeDtypeStruct((b, s, LANES), BF16),
            jax.ShapeDtypeStruct((b, s, LANES), F32),
        ],
        compiler_params=_params(("arbitrary", "arbitrary")),
        name="dsa_proj",
    )(x, pos3, g.reshape(1, d), w_main, w_qi, w_kiwi, qg, kg, inv, sgn)

    qrow = lambda bi, i: (bi, i, 0)
    qhead = lambda bi, i: (bi, 0, i, 0)
    full = lambda bi, i: (bi, 0, 0, 0)
    n_blocks = s // tq
    assert tq <= topk and 32 % n_pair == 0
    nxt_head = lambda bi, i: (bi, 0, jnp.minimum(i + 1, n_blocks - 1), 0)
    nxt_row = lambda bi, i: (bi, jnp.minimum(i + 1, n_blocks - 1), 0)
    return pl.pallas_call(
        functools.partial(_dsa_attn_kernel, tq=tq, n_blocks=n_blocks, topk=topk,
                          n_idx_heads=n_idx_heads, n_pair=n_pair, unroll_small=2, unroll_large=1),
        grid=(b, n_blocks),
        in_specs=[
            pl.BlockSpec(memory_space=pltpu.SMEM),
            pl.BlockSpec((1, n_heads, tq, LANES), qhead),
            pl.BlockSpec((1, n_idx_heads, tq, LANES), nxt_head),
            pl.BlockSpec((1, tq, LANES), nxt_row),
            pl.BlockSpec((1, n_pair, s, LANES), full),
            pl.BlockSpec((1, n_pair, s, LANES), full),
            pl.BlockSpec((1, s, LANES), lambda bi, i: (bi, 0, 0)),
            pl.BlockSpec((1, tq, width), qrow),
            pl.BlockSpec((1, tq, d), qrow),
            _single((width, d), lambda bi, i: (0, 0)),
        ],
        out_specs=pl.BlockSpec((1, tq, d), qrow),
        out_shape=jax.ShapeDtypeStruct((b, s, d), F32),
        scratch_shapes=[
            pltpu.VMEM((s // LANES, tq, LANES), F32),
            pltpu.VMEM((2, s // LANES, tq, LANES), F32),
            pltpu.VMEM((n_pair, tq, LANES), F32),
            pltpu.VMEM((n_idx_heads, tq, LANES), F32),
        ],
        compiler_params=_params(("arbitrary", "arbitrary")),
        name="dsa_attn",
    )(_logits_bounded(q_norm, k_norm, head_dim, head_dim ** -0.5 * LOG2E),
      q, qi, wi, k, v, ki, z, x, w_out.astype(BF16))


def _mla_proj_kernel(x_ref, pos_ref, g_ref, w_in_ref, qlat_g_ref, kvlat_g_ref, w_uq_ref, w_uq_rot_ref,
                     w_uk_ref, w_uv_ref, gains_ref, inv_ref, sgn_ref,
                     q_ref, k_ref, v_ref, z_ref,
                     *, q_lora, kv_lora, n_heads, qk_dim, scale):
    x = x_ref[0]
    d = x.shape[-1]
    xb = (_rms(x, d) * g_ref[...]).astype(BF16)
    pos = pos_ref[0].astype(F32)
    ang = pos * inv_ref[...]
    cos_t, sin_t = jnp.cos(ang), jnp.sin(ang) * sgn_ref[...]
    cos_q = cos_t * (gains_ref[0:1, :] * scale)
    sin_q = sin_t * (gains_ref[1:2, :] * scale)
    kg = gains_ref[2:3, :]

    c1 = q_lora + kv_lora
    cq = _dot(xb, w_in_ref[:, 0:q_lora])
    ckv = _dot(xb, w_in_ref[:, q_lora:c1])
    kr = _dot(xb, w_in_ref[:, c1:c1 + LANES])
    kr_rot = _dot(xb, w_in_ref[:, c1 + LANES:c1 + 2 * LANES])
    z = _dot(xb, w_in_ref[:, c1 + 2 * LANES:])
    z_ref[0] = z.astype(BF16)

    cq_b = (_rms(cq, q_lora) * qlat_g_ref[...]).astype(BF16)
    q = _dot(cq_b, w_uq_ref[...])
    q_rot = _dot(cq_b, w_uq_rot_ref[...])
    ckv_b = (_rms(ckv, kv_lora) * kvlat_g_ref[...]).astype(BF16)
    kn = _dot(ckv_b, w_uk_ref[...])
    v = _dot(ckv_b, w_uv_ref[...])

    k_rope = kr * (kg * cos_t) + kr_rot * (gains_ref[3:4, :] * sin_t)
    kr_ss = jnp.sum(kr * kr, axis=-1, keepdims=True)
    inv_n = 1.0 / qk_dim
    for h in range(n_heads):
        sl = slice(h * LANES, (h + 1) * LANES)
        qh = q[:, sl]
        q_s = lax.rsqrt(jnp.sum(qh * qh, axis=-1, keepdims=True) * inv_n + EPS)
        q_ref[0, h] = (q_s * (qh * cos_q + q_rot[:, sl] * sin_q)).astype(BF16)
        kh = kn[:, sl]
        k_s = lax.rsqrt((jnp.sum(kh * kh, axis=-1, keepdims=True) + kr_ss) * inv_n + EPS)
        k_ref[0, h] = (k_s * (kh * kg + k_rope)).astype(BF16)
    for j in range(n_heads // 2):
        v_ref[0, j] = v[:, j * LANES:(j + 1) * LANES].astype(BF16)


def _mla_attn_kernel(bounded_ref, q_ref, k_ref, v_ref, z_ref, x_ref, w_out_ref, o_ref, o_scr,
                     *, tq, seq, n_blocks, n_pair):
    qb = pl.program_id(1)
    bounded = bounded_ref[0, 0] != 0
    lo = lax.broadcasted_iota(jnp.int32, (tq, LANES), 1) < 64

    def body(i):
        nvis = (i + 1) * tq
        diag_bias = jnp.where(_diag_visible(tq), 0.0, NEG_BIG)

        def head_out(h, vt):
            logit = _dot_nt(q_ref[0, h], k_ref[0, h, 0:nvis, :])
            last = logit[:, nvis - tq:nvis] + diag_bias
            if nvis > tq:
                logit = jnp.concatenate([logit[:, 0:nvis - tq], last], axis=1)
            else:
                logit = last
            return _softmax_pv(logit, vt, bounded=True)

        def pair_step(j, _):
            vt = v_ref[0, j, 0:nvis, :]
            o_scr[j] = jnp.where(lo, head_out(2 * j, vt), head_out(2 * j + 1, vt))
            return 0

        lax.fori_loop(0, n_pair, pair_step, 0, unroll=4)

    @pl.when(bounded)
    def _():
        _per_block_variants(qb, n_blocks, body)

    @pl.when(jnp.logical_not(bounded))
    def _():
        s_idx = lax.broadcasted_iota(jnp.int32, (tq, seq), 1)
        q_chunk = (qb * tq + lax.broadcasted_iota(jnp.int32, (tq, seq), 0)) >> CHUNK_SHIFT
        bias = jnp.where((s_idx >> CHUNK_SHIFT) <= q_chunk, 0.0, NEG_BIG)

        def pair_step(j, _):
            vt = v_ref[0, j]
            outs = [_softmax_pv(_dot_nt(q_ref[0, 2 * j + e], k_ref[0, 2 * j + e]) + bias, vt, bounded=False)
                    for e in range(2)]
            o_scr[j] = jnp.where(lo, outs[0], outs[1])
            return 0

        lax.fori_loop(0, n_pair, pair_step, 0)

    o = jnp.concatenate([o_scr[j] for j in range(n_pair)], axis=1)
    g = (o * _silu(z_ref[0].astype(F32))).astype(BF16)
    o_ref[0] = x_ref[0] + _dot(g, w_out_ref[...])


def _mla_layer(x, pos3, g, w_in, q_lat_norm, kv_lat_norm, w_uq, w_ukv, q_norm, k_norm, w_out,
               *, tm=512, tq=256):
    b, s, d = x.shape
    width = w_out.shape[0]
    q_lora = q_lat_norm.shape[0]
    kv_lora = kv_lat_norm.shape[0]
    qk_dim = q_norm.shape[0]
    v_dim = 64
    nope = 64
    rope_dim = qk_dim - nope
    n_heads = width // v_dim
    assert nope + rope_dim <= LANES and w_ukv.shape[1] == n_heads * (nope + v_dim)
    assert w_in.shape[1] == q_lora + kv_lora + rope_dim + width
    assert s % tq == 0 and tq % (1 << CHUNK_SHIFT) == 0
    pad = LANES - qk_dim

    rope_half = rope_dim // 2

    def rope_lanes(t):
        return jnp.pad(t, [(0, 0)] * (t.ndim - 1) + [(nope, LANES - nope - rope_dim)])

    def partner(t):
        return jnp.concatenate([t[..., rope_half:], t[..., :rope_half]], axis=-1)

    c1 = q_lora + kv_lora
    w_kr = w_in[:, c1:c1 + rope_dim]
    w_in_p = jnp.concatenate([w_in[:, :c1], rope_lanes(w_kr), rope_lanes(partner(w_kr)),
                              w_in[:, c1 + rope_dim:]], axis=1).astype(BF16)
    w_uq3 = w_uq.reshape(q_lora, n_heads, qk_dim)
    w_uq_p = jnp.pad(w_uq3, ((0, 0), (0, 0), (0, pad))).reshape(q_lora, n_heads * LANES).astype(BF16)
    w_uq_rot = rope_lanes(partner(w_uq3[:, :, nope:])).reshape(q_lora, n_heads * LANES).astype(BF16)
    w_ukv3 = w_ukv.reshape(kv_lora, n_heads, nope + v_dim)
    w_uk_p = jnp.pad(w_ukv3[:, :, :nope], ((0, 0), (0, 0), (0, LANES - nope)))
    w_uk_p = w_uk_p.reshape(kv_lora, n_heads * LANES).astype(BF16)
    w_uv = w_ukv3[:, :, nope:].reshape(kv_lora, n_heads * v_dim).astype(BF16)
    gains = jnp.stack([jnp.pad(q_norm, (0, pad)), rope_lanes(partner(q_norm[nope:])),
                       jnp.pad(k_norm, (0, pad)), rope_lanes(partner(k_norm[nope:]))])

    inv_h = _rope_inv(rope_half)
    inv = jnp.concatenate([jnp.zeros((nope,), F32), inv_h, inv_h,
                           jnp.zeros((LANES - nope - rope_dim,), F32)]).reshape(1, LANES)
    sgn = jnp.concatenate([jnp.ones((nope,), F32), -jnp.ones((rope_half,), F32),
                           jnp.ones((LANES - nope - rope_half,), F32)]).reshape(1, LANES)

    n_pair = n_heads // 2
    row = lambda bi, i: (bi, i, 0)
    head_row = lambda bi, i: (bi, 0, i, 0)
    n_in = w_in_p.shape[1]
    q, k, v, z = pl.pallas_call(
        functools.partial(_mla_proj_kernel, q_lora=q_lora, kv_lora=kv_lora, n_heads=n_heads,
                          qk_dim=qk_dim, scale=qk_dim ** -0.5 * LOG2E),
        grid=(b, s // tm),
        in_specs=[
            pl.BlockSpec((1, tm, d), row),
            pl.BlockSpec((1, tm, 1), row),
            _const_spec((1, d)),
            _const_spec((d, n_in)),
            _const_spec((1, q_lora)),
            _const_spec((1, kv_lora)),
            _const_spec((q_lora, n_heads * LANES)),
            _const_spec((q_lora, n_heads * LANES)),
            _const_spec((kv_lora, n_heads * LANES)),
            _const_spec((kv_lora, n_heads * v_dim)),
            _const_spec((4, LANES)),
            _const_spec((1, LANES)),
            _const_spec((1, LANES)),
        ],
        out_specs=[
            pl.BlockSpec((1, n_heads, tm, LANES), head_row),
            pl.BlockSpec((1, n_heads, tm, LANES), head_row),
            pl.BlockSpec((1, n_pair, tm, LANES), head_row),
            pl.BlockSpec((1, tm, width), row),
        ],
        out_shape=[
            jax.ShapeDtypeStruct((b, n_heads, s, LANES), BF16),
            jax.ShapeDtypeStruct((b, n_heads, s, LANES), BF16),
            jax.ShapeDtypeStruct((b, n_pair, s, LANES), BF16),
            jax.ShapeDtypeStruct((b, s, width), BF16),
        ],
        compiler_params=_params(("arbitrary", "arbitrary")),
        name="mla_proj",
    )(x, pos3, g.reshape(1, d), w_in_p, q_lat_norm.reshape(1, q_lora), kv_lat_norm.reshape(1, kv_lora),
      w_uq_p, w_uq_rot, w_uk_p, w_uv, gains, inv, sgn)

    qrow = lambda bi, i: (bi, i, 0)
    qhead = lambda bi, i: (bi, 0, i, 0)
    full = lambda bi, i: (bi, 0, 0, 0)
    return pl.pallas_call(
        functools.partial(_mla_attn_kernel, tq=tq, seq=s, n_blocks=s // tq, n_pair=n_pair),
        grid=(b, s // tq),
        in_specs=[
            pl.BlockSpec(memory_space=pltpu.SMEM),
            pl.BlockSpec((1, n_heads, tq, LANES), qhead),
            pl.BlockSpec((1, n_heads, s, LANES), full),
            pl.BlockSpec((1, n_pair, s, LANES), full),
            pl.BlockSpec((1, tq, width), qrow),
            pl.BlockSpec((1, tq, d), qrow),
            _single((width, d), lambda bi, i: (0, 0)),
        ],
        out_specs=pl.BlockSpec((1, tq, d), qrow),
        out_shape=jax.ShapeDtypeStruct((b, s, d), F32),
        scratch_shapes=[
            pltpu.VMEM((n_pair, tq, LANES), F32),
        ],
        compiler_params=_params(("arbitrary", "arbitrary")),
        name="mla_attn",
    )(_logits_bounded(q_norm, k_norm, qk_dim, qk_dim ** -0.5 * LOG2E), q, k, v, z, x, w_out.astype(BF16))


def kernel(x, positions, a_norm, a_w_in, a_conv_w, a_conv_b, a_w_out, b_norm, b_w_in, b_q_norm, b_k_norm, b_w_out, c_norm, c_w_in, c_q_lat_norm, c_kv_lat_norm, c_w_uq, c_w_ukv, c_q_norm, c_k_norm, c_w_out):
    depth = a_norm.shape[0] + b_norm.shape[0] + c_norm.shape[0]
    pos3 = positions.reshape(positions.shape + (1,))
    for i in range(depth):
        kind, j = i % 3, i // 3
        if kind == 0:
            x = _conv_layer(x, j, a_norm[j], a_w_in, a_conv_w[j], a_conv_b[j], a_w_out)
        elif kind == 1:
            x = _dsa_layer(x, pos3, b_norm[j], b_w_in[j], b_q_norm[j], b_k_norm[j], b_w_out[j])
        else:
            x = _mla_layer(x, pos3, c_norm[j], c_w_in[j], c_q_lat_norm[j], c_kv_lat_norm[j],
                           c_w_uq[j], c_w_ukv[j], c_q_norm[j], c_k_norm[j], c_w_out[j])
    return x
```

```python
import functools
import math

import jax
import jax.numpy as jnp
from jax import lax
from jax.experimental import pallas as pl
from jax.experimental.pallas import tpu as pltpu

EPS = 1e-6
ROPE_THETA = 10000.0
CHUNK_SHIFT = 6
TOPK_MAX = 256
LANES = 128
NEG_BIG = -1e30
INT_MIN = -(2 ** 31)
F32_MAX = 3.4028234663852886e38
LOG2E = math.log2(math.e)
SEARCH_ROW_GROUPS = 4
LOGIT_BOUND = 64.0
VMEM_LIMIT = 56 * 1024 * 1024

BF16 = jnp.bfloat16
F32 = jnp.float32


def _dot(a, b):
    return jnp.dot(a, b, preferred_element_type=F32)


def _dot_nt(a, b):
    return lax.dot_general(a, b, (((1,), (1,)), ((), ())), preferred_element_type=F32)


def _rms(x, n):
    return x * lax.rsqrt(jnp.sum(x * x, axis=-1, keepdims=True) * (1.0 / n) + EPS)


def _silu(z):
    return z * jax.nn.sigmoid(z)


def _rope_tile(t, cos_t, sin_t, first_mask, half):
    partner = jnp.where(first_mask, pltpu.roll(t, LANES - half, 1), pltpu.roll(t, half, 1))
    return t * cos_t + partner * sin_t


def _params(sem):
    return pltpu.CompilerParams(dimension_semantics=sem, vmem_limit_bytes=VMEM_LIMIT)


def _single(shape, index_map):
    return pl.BlockSpec(shape, index_map, pipeline_mode=pl.Buffered(1))


def _const_spec(shape):
    nd = len(shape)
    return _single(shape, lambda *_: (0,) * nd)


def _diag_visible(tq):
    r = lax.broadcasted_iota(jnp.int32, (tq, tq), 0)
    c = lax.broadcasted_iota(jnp.int32, (tq, tq), 1)
    return (c >> CHUNK_SHIFT) <= (r >> CHUNK_SHIFT)


def _softmax_pv(logit, v, *, bounded):
    if not bounded:
        logit = logit - jnp.max(logit, axis=-1, keepdims=True)
    p = jnp.exp2(logit)
    den = jnp.sum(p, axis=-1, keepdims=True)
    return _dot(p.astype(BF16), v) / den


def _logits_bounded(q_gain, k_gain, n, scale):
    bound = 1.02 * n * scale * jnp.max(jnp.abs(q_gain)) * jnp.max(jnp.abs(k_gain))
    return (bound <= LOGIT_BOUND).astype(jnp.int32).reshape(1, 1)


def _per_block_variants(qb, n_blocks, body):
    for i in range(n_blocks):
        pl.when(qb == i)(functools.partial(body, i))


def _conv_kernel(x_ref, g_ref, w_in_f32_ref, cw_ref, cb_ref, w_out_f32_ref, o_ref, u_scr, w_in_ref, w_out_ref,
                 *, tm, width):
    j = pl.program_id(1)

    @pl.when(jnp.logical_and(pl.program_id(0) == 0, j == 0))
    def _():
        w_in_ref[...] = w_in_f32_ref[...].astype(BF16)
        w_out_ref[...] = w_out_f32_ref[...].astype(BF16)

    x = x_ref[0]
    d = x.shape[-1]
    xb = (_rms(x, d) * g_ref[...]).astype(BF16)

    @pl.when(j == 0)
    def _():
        u_scr[0:8, :] = jnp.zeros((8, width), F32)

    @pl.when(j > 0)
    def _():
        u_scr[0:8, :] = u_scr[tm:tm + 8, :]

    bg = _dot(xb, w_in_ref[:, 0 * width:1 * width])
    cg = _dot(xb, w_in_ref[:, 1 * width:2 * width])
    hv = _dot(xb, w_in_ref[:, 2 * width:3 * width])
    z = _dot(xb, w_in_ref[:, 3 * width:4 * width])
    u = cg * hv
    u_scr[8:tm + 8, :] = u
    y = (cw_ref[2:3, :] * u + cw_ref[1:2, :] * u_scr[7:7 + tm, :]
         + cw_ref[0:1, :] * u_scr[6:6 + tm, :] + cb_ref[...])
    g = (bg * y * _silu(z)).astype(BF16)
    o_ref[0] = x + _dot(g, w_out_ref[...])


def _conv_layer(x, layer, g, w_in_all, cw, cb, w_out_all, *, tm=256):
    b, s, d = x.shape
    width = w_out_all.shape[1]
    return pl.pallas_call(
        functools.partial(_conv_kernel, tm=tm, width=width),
        grid=(b, s // tm),
        in_specs=[
            pl.BlockSpec((1, tm, d), lambda bi, i: (bi, i, 0)),
            _const_spec((1, d)),
            _single((None, d, 4 * width), lambda bi, i: (layer, 0, 0)),
            _const_spec((3, width)),
            _const_spec((1, width)),
            _single((None, width, d), lambda bi, i: (layer, 0, 0)),
        ],
        out_specs=pl.BlockSpec((1, tm, d), lambda bi, i: (bi, i, 0)),
        out_shape=jax.ShapeDtypeStruct((b, s, d), F32),
        scratch_shapes=[
            pltpu.VMEM((tm + 8, width), F32),
            pltpu.VMEM((d, 4 * width), BF16),
            pltpu.VMEM((width, d), BF16),
        ],
        compiler_params=_params(("arbitrary", "arbitrary")),
        name="conv_mixer",
    )(x, g.reshape(1, d), w_in_all, cw, cb.reshape(1, width), w_out_all)


def _dsa_proj_kernel(x_ref, pos_ref, g_ref, w_main_ref, w_qi_ref, w_kiwi_ref, qg_ref, kg_ref,
                     inv_ref, sgn_ref,
                     q_ref, k_ref, v_ref, z_ref, qi_ref, ki_ref, wi_ref,
                     *, width, n_pair, n_qi_tile, scale, wi_scale):
    x = x_ref[0]
    d = x.shape[-1]
    tm = x.shape[0]
    xb = (_rms(x, d) * g_ref[...]).astype(BF16)
    pos = pos_ref[0].astype(F32)
    ang_a = pos * inv_ref[0:1, :]
    ang_b = pos * inv_ref[1:2, :]
    cos_a, sin_a = jnp.cos(ang_a), jnp.sin(ang_a) * sgn_ref[0:1, :]
    cos_b, sin_b = jnp.cos(ang_b), jnp.sin(ang_b) * sgn_ref[1:2, :]
    lane = lax.broadcasted_iota(jnp.int32, (tm, LANES), 1)
    lo = lane < 64
    first_b = (lane & 63) < 16
    head0 = (lane & 32) == 0

    q = _dot(xb, w_main_ref[:, 0 * width:1 * width])
    k = _dot(xb, w_main_ref[:, 1 * width:2 * width])
    v = _dot(xb, w_main_ref[:, 2 * width:3 * width])
    z = _dot(xb, w_main_ref[:, 3 * width:4 * width])
    z_ref[0] = z.astype(BF16)

    def head_norm_rope(t, gain):
        sq = t * t
        s0 = jnp.sum(jnp.where(head0, sq, 0.0), axis=-1, keepdims=True)
        s1 = jnp.sum(jnp.where(head0, 0.0, sq), axis=-1, keepdims=True)
        ms = jnp.where(head0, s0, s1) * (1.0 / 64)
        t = t * lax.rsqrt(ms + EPS) * gain
        return t * cos_a + pltpu.roll(t, 64, 1) * sin_a

    for j in range(n_pair):
        sl = slice(j * LANES, (j + 1) * LANES)
        qt = head_norm_rope(q[:, sl], qg_ref[...]) * scale
        q_ref[0, 2 * j] = jnp.where(head0, qt, 0.0).astype(BF16)
        q_ref[0, 2 * j + 1] = jnp.where(head0, 0.0, qt).astype(BF16)
        k_ref[0, j] = head_norm_rope(k[:, sl], kg_ref[...]).astype(BF16)
        v_ref[0, j] = v[:, sl].astype(BF16)

    qi = _dot(xb, w_qi_ref[...])
    for j in range(n_qi_tile):
        t = _rope_tile(qi[:, j * LANES:(j + 1) * LANES], cos_b, sin_b, first_b, 16)
        qi_ref[0, 2 * j] = jnp.where(lo, t, 0.0).astype(BF16)
        qi_ref[0, 2 * j + 1] = jnp.where(lo, 0.0, t).astype(BF16)

    kiwi = _dot(xb, w_kiwi_ref[...])
    ki_ref[0] = _rope_tile(kiwi[:, 0:LANES], cos_b, sin_b, first_b, 16).astype(BF16)
    wi_ref[0] = kiwi[:, LANES:2 * LANES] * wi_scale


def _key_to_float(key):
    return pltpu.bitcast(jnp.where(key >= 0, key, key ^ 0x7FFFFFFF), F32)


def _search_init(tq):
    rg = tq // SEARCH_ROW_GROUPS
    return tuple(jnp.full((rg, 1), INT_MIN, jnp.int32) for _ in range(SEARCH_ROW_GROUPS))


def _search_step(step, bases, score_scr, nt, topk):
    rg = bases[0].shape[0]
    bit = jnp.left_shift(jnp.int32(1), 31 - step)
    out = []
    for g, base_g in enumerate(bases):
        cand = base_g + bit
        hits = jnp.where(score_scr[0:nt, g * rg:(g + 1) * rg, :] >= _key_to_float(cand), 1.0, 0.0)
        n_ge = jnp.sum(jnp.sum(hits, axis=0), axis=1, keepdims=True)
        out.append(jnp.where(n_ge >= float(topk), cand, base_g))
    return tuple(out)


def _select_bias(bases, score_scr, bias_scr, slot, nt, tq, topk):
    topk_f = float(topk)

    def count(pred_fn):
        hits = jnp.where(pred_fn(score_scr[0:nt]), 1.0, 0.0)
        return jnp.sum(jnp.sum(hits, axis=0), axis=1, keepdims=True)

    thr = _key_to_float(jnp.concatenate(bases, axis=0))
    lowest = jnp.full((tq, 1), -F32_MAX, F32)
    thr = jnp.where(count(lambda s: s >= lowest) < topk_f, lowest, thr)
    excess = jnp.max(count(lambda s: s >= thr)) > topk_f

    def plain(_):
        bias_scr[slot, 0:nt] = jnp.where(score_scr[0:nt] >= thr, 0.0, NEG_BIG)
        return 0

    def index_ordered_ties(_):
        need = topk_f - count(lambda s: s > thr)
        r = lax.broadcasted_iota(jnp.int32, (LANES, LANES), 0)
        c = lax.broadcasted_iota(jnp.int32, (LANES, LANES), 1)
        upper = jnp.where(r < c, 1.0, 0.0).astype(BF16)

        def tile_step(t, carry):
            s = score_scr[t]
            tie_f = jnp.where(s == thr, 1.0, 0.0)
            before = _dot(tie_f.astype(BF16), upper) + carry
            tie_bias = jnp.where(s == thr, jnp.where(before < need, 0.0, NEG_BIG), NEG_BIG)
            bias_scr[slot, t] = jnp.where(s > thr, 0.0, tie_bias)
            return carry + jnp.sum(tie_f, axis=1, keepdims=True)

        lax.fori_loop(0, nt, tile_step, jnp.zeros((tq, 1), F32))
        return 0

    lax.cond(excess, index_ordered_ties, plain, 0)


def _dsa_attn_kernel(bounded_ref, q_ref, qi_ref, wi_ref, k_ref, v_ref, ki_ref, z_ref, x_ref, w_out_ref,
                     o_ref, score_scr, bias_scr, o_scr, wi_scr, *, tq, n_blocks, topk, n_idx_heads, n_pair,
                     unroll_small, unroll_large):
    qb = pl.program_id(1)
    bounded = bounded_ref[0, 0] != 0
    lo = lax.broadcasted_iota(jnp.int32, (tq, LANES), 1) < 64
    tq_tiles = tq // LANES
    n_tiles = score_scr.shape[0]
    steps_per_pair = 32 // n_pair

    def pair_out(j, nvis, slot, is_bounded):
        kt = k_ref[0, j, 0:nvis, :]
        vt = v_ref[0, j, 0:nvis, :]
        qq = q_ref[0, pl.ds(2 * j, 2)].reshape(2 * tq, LANES)
        logit = _dot_nt(qq, kt)
        bias = jnp.concatenate([bias_scr[slot, t] for t in range(nvis // LANES)], axis=1)
        o_even = _softmax_pv(logit[0:tq] + bias, vt, bounded=is_bounded)
        o_odd = _softmax_pv(logit[tq:2 * tq] + bias, vt, bounded=is_bounded)
        return jnp.where(lo, o_even, o_odd)

    def diag_bias_tiles(slot, nt):
        diag_vis = _diag_visible(tq)
        for d in range(tq_tiles):
            bias_scr[slot, nt - tq_tiles + d] = jnp.where(diag_vis[:, d * LANES:(d + 1) * LANES], 0.0, NEG_BIG)

    wi = wi_ref[0]
    for h in range(n_idx_heads):
        wi_scr[h] = jnp.broadcast_to(wi[:, h:h + 1], (tq, LANES))

    def body(i):
        cur, nxt = i % 2, (i + 1) % 2
        nvis = (i + 1) * tq
        if i == 0:
            diag_bias_tiles(cur, nvis // LANES)
        nvis2 = nvis + tq
        nt2 = nvis2 // LANES
        search_next = i + 1 < n_blocks and nvis2 > topk
        if i + 1 < n_blocks and not search_next:
            diag_bias_tiles(nxt, nt2)
        if search_next:
            score_scr[0:nt2] = jnp.zeros((nt2, tq, LANES), F32)

            def idx_head(h, _):
                rel = jnp.maximum(_dot_nt(qi_ref[0, h], ki_ref[0, 0:nvis2, :]), 0.0)
                w = wi_scr[h]
                for t in range(nt2):
                    score_scr[t] += w * rel[:, t * LANES:(t + 1) * LANES]
                return 0

            lax.fori_loop(0, n_idx_heads, idx_head, 0, unroll=2)
            diag_vis = _diag_visible(tq)
            for d in range(tq_tiles):
                t = nt2 - tq_tiles + d
                score_scr[t] = jnp.where(diag_vis[:, d * LANES:(d + 1) * LANES], score_scr[t], -jnp.inf)

        def pair_step(j, bases):
            o_scr[j] = pair_out(j, nvis, cur, True)
            for s_ in range(steps_per_pair if search_next else 0):
                bases = _search_step(j * steps_per_pair + s_, bases, score_scr, nt2, topk)
            return bases

        bases = lax.fori_loop(0, n_pair, pair_step, _search_init(tq) if search_next else (),
                              unroll=unroll_small if 2 * i < n_blocks else unroll_large)
        if search_next:
            _select_bias(bases, score_scr, bias_scr, nxt, nt2, tq, topk)

    _per_block_variants(qb, n_blocks, body)

    @pl.when(jnp.logical_not(bounded))
    def _():
        slot = qb % 2

        def mask_tile(t, _):
            bias_scr[slot, t] = jnp.full((tq, LANES), NEG_BIG, F32)
            return 0

        lax.fori_loop((qb + 1) * tq_tiles, n_tiles, mask_tile, 0)

        def pair_step(j, _):
            o_scr[j] = pair_out(j, n_tiles * LANES, slot, False)
            return 0

        lax.fori_loop(0, n_pair, pair_step, 0)

    o = jnp.concatenate([o_scr[j] for j in range(n_pair)], axis=1)
    g = (o * _silu(z_ref[0].astype(F32))).astype(BF16)
    o_ref[0] = x_ref[0] + _dot(g, w_out_ref[...])


def _rope_inv(half):
    return ROPE_THETA ** (-jnp.arange(half, dtype=F32) / half)


def _dsa_layer(x, pos3, g, w_in, q_norm, k_norm, w_out, *, tm=256, tq=256):
    b, s, d = x.shape
    width = w_out.shape[0]
    head_dim = q_norm.shape[0]
    assert head_dim == 64 and width % LANES == 0
    n_pair = width // LANES
    idx_dim = 64
    n_idx_heads = (w_in.shape[1] - 4 * width - idx_dim) // (idx_dim + 1)
    assert 4 * width + n_idx_heads * idx_dim + idx_dim + n_idx_heads == w_in.shape[1]
    n_qi_tile = n_idx_heads * idx_dim // LANES
    c0 = 4 * width
    c1 = c0 + n_idx_heads * idx_dim
    c2 = c1 + idx_dim
    half = head_dim // 2

    def pair_split(t):
        t4 = t.reshape(t.shape[:-1] + (n_pair, 2, 2, half))
        return jnp.swapaxes(t4, -3, -2).reshape(t.shape)

    w_main = jnp.concatenate([pair_split(w_in[:, :width]), pair_split(w_in[:, width:2 * width]),
                              w_in[:, 2 * width:c0]], axis=1).astype(BF16)
    w_qi = w_in[:, c0:c1].astype(BF16)
    w_ki = w_in[:, c1:c2]
    w_wi = jnp.pad(w_in[:, c2:], ((0, 0), (0, LANES - n_idx_heads)))
    w_kiwi = jnp.concatenate([w_ki, w_ki, w_wi], axis=1).astype(BF16)

    inv32 = _rope_inv(32)
    inv16 = _rope_inv(16)
    zeros32 = jnp.zeros((32,), F32)
    inv_a = jnp.tile(inv32, 4)
    inv_b = jnp.tile(jnp.concatenate([inv16, inv16, zeros32]), 2)
    inv = jnp.stack([inv_a, inv_b])
    sgn_a = jnp.concatenate([-jnp.ones((64,), F32), jnp.ones((64,), F32)])
    sgn_b = jnp.tile(jnp.concatenate([-jnp.ones((16,), F32), jnp.ones((48,), F32)]), 2)
    sgn = jnp.stack([sgn_a, sgn_b])

    def pair_gain(gn):
        return jnp.concatenate([gn[:half], gn[:half], gn[half:], gn[half:]]).reshape(1, LANES)

    qg = pair_gain(q_norm)
    kg = pair_gain(k_norm)

    topk = min(TOPK_MAX, s // 4)
    assert s % tq == 0 and tq % (1 << CHUNK_SHIFT) == 0
    n_heads = 2 * n_pair
    row = lambda bi, i: (bi, i, 0)
    head_row = lambda bi, i: (bi, 0, i, 0)
    q, k, v, z, qi, ki, wi = pl.pallas_call(
        functools.partial(_dsa_proj_kernel, width=width, n_pair=n_pair, n_qi_tile=n_qi_tile,
                          scale=head_dim ** -0.5 * LOG2E,
                          wi_scale=n_idx_heads ** -0.5 * idx_dim ** -0.5),
        grid=(b, s // tm),
        in_specs=[
            pl.BlockSpec((1, tm, d), row),
            pl.BlockSpec((1, tm, 1), row),
            _const_spec((1, d)),
            _const_spec((d, c0)),
            _const_spec((d, c1 - c0)),
            _const_spec((d, 2 * LANES)),
            _const_spec((1, LANES)),
            _const_spec((1, LANES)),
            _const_spec((2, LANES)),
            _const_spec((2, LANES)),
        ],
        out_specs=[
            pl.BlockSpec((1, n_heads, tm, LANES), head_row),
            pl.BlockSpec((1, n_pair, tm, LANES), head_row),
            pl.BlockSpec((1, n_pair, tm, LANES), head_row),
            pl.BlockSpec((1, tm, width), row),
            pl.BlockSpec((1, n_idx_heads, tm, LANES), head_row),
            pl.BlockSpec((1, tm, LANES), row),
            pl.BlockSpec((1, tm, LANES), row),
        ],
        out_shape=[
            jax.ShapeDtypeStruct((b, n_heads, s, LANES), BF16),
            jax.ShapeDtypeStruct((b, n_pair, s, LANES), BF16),
            jax.ShapeDtypeStruct((b, n_pair, s, LANES), BF16),
            jax.ShapeDtypeStruct((b, s, width), BF16),
            jax.ShapeDtypeStruct((b, n_idx_heads, s, LANES), BF16),
            jax.ShapeDtypeStruct((b, s, LANES), BF16),
            jax.ShapeDtypeStruct((b, s, LANES), F32),
        ],
        compiler_params=_params(("arbitrary", "arbitrary")),
        name="dsa_proj",
    )(x, pos3, g.reshape(1, d), w_main, w_qi, w_kiwi, qg, kg, inv, sgn)

    qrow = lambda bi, i: (bi, i, 0)
    qhead = lambda bi, i: (bi, 0, i, 0)
    full = lambda bi, i: (bi, 0, 0, 0)
    n_blocks = s // tq
    assert tq <= topk and 32 % n_pair == 0
    nxt_head = lambda bi, i: (bi, 0, jnp.minimum(i + 1, n_blocks - 1), 0)
    nxt_row = lambda bi, i: (bi, jnp.minimum(i + 1, n_blocks - 1), 0)
    return pl.pallas_call(
        functools.partial(_dsa_attn_kernel, tq=tq, n_blocks=n_blocks, topk=topk,
                          n_idx_heads=n_idx_heads, n_pair=n_pair, unroll_small=2, unroll_large=1),
        grid=(b, n_blocks),
        in_specs=[
            pl.BlockSpec(memory_space=pltpu.SMEM),
            pl.BlockSpec((1, n_heads, tq, LANES), qhead),
            pl.BlockSpec((1, n_idx_heads, tq, LANES), nxt_head),
            pl.BlockSpec((1, tq, LANES), nxt_row),
            pl.BlockSpec((1, n_pair, s, LANES), full),
            pl.BlockSpec((1, n_pair, s, LANES), full),
            pl.BlockSpec((1, s, LANES), lambda bi, i: (bi, 0, 0)),
            pl.BlockSpec((1, tq, width), qrow),
            pl.BlockSpec((1, tq, d), qrow),
            _single((width, d), lambda bi, i: (0, 0)),
        ],
        out_specs=pl.BlockSpec((1, tq, d), qrow),
        out_shape=jax.ShapeDtypeStruct((b, s, d), F32),
        scratch_shapes=[
            pltpu.VMEM((s // LANES, tq, LANES), F32),
            pltpu.VMEM((2, s // LANES, tq, LANES), F32),
            pltpu.VMEM((n_pair, tq, LANES), F32),
            pltpu.VMEM((n_idx_heads, tq, LANES), F32),
        ],
        compiler_params=_params(("arbitrary", "arbitrary")),
        name="dsa_attn",
    )(_logits_bounded(q_norm, k_norm, head_dim, head_dim ** -0.5 * LOG2E),
      q, qi, wi, k, v, ki, z, x, w_out.astype(BF16))


def _mla_proj_kernel(x_ref, pos_ref, g_ref, w_in_ref, qlat_g_ref, kvlat_g_ref, w_uq_ref, w_uq_rot_ref,
                     w_uk_ref, w_uv_ref, gains_ref, inv_ref, sgn_ref,
                     q_ref, k_ref, v_ref, z_ref,
                     *, q_lora, kv_lora, n_heads, qk_dim, scale):
    x = x_ref[0]
    d = x.shape[-1]
    xb = (_rms(x, d) * g_ref[...]).astype(BF16)
    pos = pos_ref[0].astype(F32)
    ang = pos * inv_ref[...]
    cos_t, sin_t = jnp.cos(ang), jnp.sin(ang) * sgn_ref[...]
    cos_q = cos_t * (gains_ref[0:1, :] * scale)
    sin_q = sin_t * (gains_ref[1:2, :] * scale)
    kg = gains_ref[2:3, :]

    c1 = q_lora + kv_lora
    cq = _dot(xb, w_in_ref[:, 0:q_lora])
    ckv = _dot(xb, w_in_ref[:, q_lora:c1])
    kr = _dot(xb, w_in_ref[:, c1:c1 + LANES])
    kr_rot = _dot(xb, w_in_ref[:, c1 + LANES:c1 + 2 * LANES])
    z = _dot(xb, w_in_ref[:, c1 + 2 * LANES:])
    z_ref[0] = z.astype(BF16)

    cq_b = (_rms(cq, q_lora) * qlat_g_ref[...]).astype(BF16)
    q = _dot(cq_b, w_uq_ref[...])
    q_rot = _dot(cq_b, w_uq_rot_ref[...])
    ckv_b = (_rms(ckv, kv_lora) * kvlat_g_ref[...]).astype(BF16)
    kn = _dot(ckv_b, w_uk_ref[...])
    v = _dot(ckv_b, w_uv_ref[...])

    k_rope = kr * (kg * cos_t) + kr_rot * (gains_ref[3:4, :] * sin_t)
    kr_ss = jnp.sum(kr * kr, axis=-1, keepdims=True)
    inv_n = 1.0 / qk_dim
    for h in range(n_heads):
        sl = slice(h * LANES, (h + 1) * LANES)
        qh = q[:, sl]
        q_s = lax.rsqrt(jnp.sum(qh * qh, axis=-1, keepdims=True) * inv_n + EPS)
        q_ref[0, h] = (q_s * (qh * cos_q + q_rot[:, sl] * sin_q)).astype(BF16)
        kh = kn[:, sl]
        k_s = lax.rsqrt((jnp.sum(kh * kh, axis=-1, keepdims=True) + kr_ss) * inv_n + EPS)
        k_ref[0, h] = (k_s * (kh * kg + k_rope)).astype(BF16)
    for j in range(n_heads // 2):
        v_ref[0, j] = v[:, j * LANES:(j + 1) * LANES].astype(BF16)


def _mla_attn_kernel(bounded_ref, q_ref, k_ref, v_ref, z_ref, x_ref, w_out_ref, o_ref, o_scr,
                     *, tq, seq, n_blocks, n_pair):
    qb = pl.program_id(1)
    bounded = bounded_ref[0, 0] != 0
    lo = lax.broadcasted_iota(jnp.int32, (tq, LANES), 1) < 64

    def body(i):
        nvis = (i + 1) * tq
        diag_bias = jnp.where(_diag_visible(tq), 0.0, NEG_BIG)

        def head_out(h, vt):
            logit = _dot_nt(q_ref[0, h], k_ref[0, h, 0:nvis, :])
            last = logit[:, nvis - tq:nvis] + diag_bias
            if nvis > tq:
                logit = jnp.concatenate([logit[:, 0:nvis - tq], last], axis=1)
            else:
                logit = last
            return _softmax_pv(logit, vt, bounded=True)

        def pair_step(j, _):
            vt = v_ref[0, j, 0:nvis, :]
            o_scr[j] = jnp.where(lo, head_out(2 * j, vt), head_out(2 * j + 1, vt))
            return 0

        lax.fori_loop(0, n_pair, pair_step, 0, unroll=4)

    @pl.when(bounded)
    def _():
        _per_block_variants(qb, n_blocks, body)

    @pl.when(jnp.logical_not(bounded))
    def _():
        s_idx = lax.broadcasted_iota(jnp.int32, (tq, seq), 1)
        q_chunk = (qb * tq + lax.broadcasted_iota(jnp.int32, (tq, seq), 0)) >> CHUNK_SHIFT
        bias = jnp.where((s_idx >> CHUNK_SHIFT) <= q_chunk, 0.0, NEG_BIG)

        def pair_step(j, _):
            vt = v_ref[0, j]
            outs = [_softmax_pv(_dot_nt(q_ref[0, 2 * j + e], k_ref[0, 2 * j + e]) + bias, vt, bounded=False)
                    for e in range(2)]
            o_scr[j] = jnp.where(lo, outs[0], outs[1])
            return 0

        lax.fori_loop(0, n_pair, pair_step, 0)

    o = jnp.concatenate([o_scr[j] for j in range(n_pair)], axis=1)
    g = (o * _silu(z_ref[0].astype(F32))).astype(BF16)
    o_ref[0] = x_ref[0] + _dot(g, w_out_ref[...])


def _mla_layer(x, pos3, g, w_in, q_lat_norm, kv_lat_norm, w_uq, w_ukv, q_norm, k_norm, w_out,
               *, tm=512, tq=256):
    b, s, d = x.shape
    width = w_out.shape[0]
    q_lora = q_lat_norm.shape[0]
    kv_lora = kv_lat_norm.shape[0]
    qk_dim = q_norm.shape[0]
    v_dim = 64
    nope = 64
    rope_dim = qk_dim - nope
    n_heads = width // v_dim
    assert nope + rope_dim <= LANES and w_ukv.shape[1] == n_heads * (nope + v_dim)
    assert w_in.shape[1] == q_lora + kv_lora + rope_dim + width
    assert s % tq == 0 and tq % (1 << CHUNK_SHIFT) == 0
    pad = LANES - qk_dim

    rope_half = rope_dim // 2

    def rope_lanes(t):
        return jnp.pad(t, [(0, 0)] * (t.ndim - 1) + [(nope, LANES - nope - rope_dim)])

    def partner(t):
        return jnp.concatenate([t[..., rope_half:], t[..., :rope_half]], axis=-1)

    c1 = q_lora + kv_lora
    w_kr = w_in[:, c1:c1 + rope_dim]
    w_in_p = jnp.concatenate([w_in[:, :c1], rope_lanes(w_kr), rope_lanes(partner(w_kr)),
                              w_in[:, c1 + rope_dim:]], axis=1).astype(BF16)
    w_uq3 = w_uq.reshape(q_lora, n_heads, qk_dim)
    w_uq_p = jnp.pad(w_uq3, ((0, 0), (0, 0), (0, pad))).reshape(q_lora, n_heads * LANES).astype(BF16)
    w_uq_rot = rope_lanes(partner(w_uq3[:, :, nope:])).reshape(q_lora, n_heads * LANES).astype(BF16)
    w_ukv3 = w_ukv.reshape(kv_lora, n_heads, nope + v_dim)
    w_uk_p = jnp.pad(w_ukv3[:, :, :nope], ((0, 0), (0, 0), (0, LANES - nope)))
    w_uk_p = w_uk_p.reshape(kv_lora, n_heads * LANES).astype(BF16)
    w_uv = w_ukv3[:, :, nope:].reshape(kv_lora, n_heads * v_dim).astype(BF16)
    gains = jnp.stack([jnp.pad(q_norm, (0, pad)), rope_lanes(partner(q_norm[nope:])),
                       jnp.pad(k_norm, (0, pad)), rope_lanes(partner(k_norm[nope:]))])

    inv_h = _rope_inv(rope_half)
    inv = jnp.concatenate([jnp.zeros((nope,), F32), inv_h, inv_h,
                           jnp.zeros((LANES - nope - rope_dim,), F32)]).reshape(1, LANES)
    sgn = jnp.concatenate([jnp.ones((nope,), F32), -jnp.ones((rope_half,), F32),
                           jnp.ones((LANES - nope - rope_half,), F32)]).reshape(1, LANES)

    n_pair = n_heads // 2
    row = lambda bi, i: (bi, i, 0)
    head_row = lambda bi, i: (bi, 0, i, 0)
    n_in = w_in_p.shape[1]
    q, k, v, z = pl.pallas_call(
        functools.partial(_mla_proj_kernel, q_lora=q_lora, kv_lora=kv_lora, n_heads=n_heads,
                          qk_dim=qk_dim, scale=qk_dim ** -0.5 * LOG2E),
        grid=(b, s // tm),
        in_specs=[
            pl.BlockSpec((1, tm, d), row),
            pl.BlockSpec((1, tm, 1), row),
            _const_spec((1, d)),
            _const_spec((d, n_in)),
            _const_spec((1, q_lora)),
            _const_spec((1, kv_lora)),
            _const_spec((q_lora, n_heads * LANES)),
            _const_spec((q_lora, n_heads * LANES)),
            _const_spec((kv_lora, n_heads * LANES)),
            _const_spec((kv_lora, n_heads * v_dim)),
            _const_spec((4, LANES)),
            _const_spec((1, LANES)),
            _const_spec((1, LANES)),
        ],
        out_specs=[
            pl.BlockSpec((1, n_heads, tm, LANES), head_row),
            pl.BlockSpec((1, n_heads, tm, LANES), head_row),
            pl.BlockSpec((1, n_pair, tm, LANES), head_row),
            pl.BlockSpec((1, tm, width), row),
        ],
        out_shape=[
            jax.ShapeDtypeStruct((b, n_heads, s, LANES), BF16),
            jax.ShapeDtypeStruct((b, n_heads, s, LANES), BF16),
            jax.ShapeDtypeStruct((b, n_pair, s, LANES), BF16),
            jax.ShapeDtypeStruct((b, s, width), BF16),
        ],
        compiler_params=_params(("arbitrary", "arbitrary")),
        name="mla_proj",
    )(x, pos3, g.reshape(1, d), w_in_p, q_lat_norm.reshape(1, q_lora), kv_lat_norm.reshape(1, kv_lora),
      w_uq_p, w_uq_rot, w_uk_p, w_uv, gains, inv, sgn)

    qrow = lambda bi, i: (bi, i, 0)
    qhead = lambda bi, i: (bi, 0, i, 0)
    full = lambda bi, i: (bi, 0, 0, 0)
    return pl.pallas_call(
        functools.partial(_mla_attn_kernel, tq=tq, seq=s, n_blocks=s // tq, n_pair=n_pair),
        grid=(b, s // tq),
        in_specs=[
            pl.BlockSpec(memory_space=pltpu.SMEM),
            pl.BlockSpec((1, n_heads, tq, LANES), qhead),
            pl.BlockSpec((1, n_heads, s, LANES), full),
            pl.BlockSpec((1, n_pair, s, LANES), full),
            pl.BlockSpec((1, tq, width), qrow),
            pl.BlockSpec((1, tq, d), qrow),
            _single((width, d), lambda bi, i: (0, 0)),
        ],
        out_specs=pl.BlockSpec((1, tq, d), qrow),
        out_shape=jax.ShapeDtypeStruct((b, s, d), F32),
        scratch_shapes=[
            pltpu.VMEM((n_pair, tq, LANES), F32),
        ],
        compiler_params=_params(("arbitrary", "arbitrary")),
        name="mla_attn",
    )(_logits_bounded(q_norm, k_norm, qk_dim, qk_dim ** -0.5 * LOG2E), q, k, v, z, x, w_out.astype(BF16))


def kernel(x, positions, a_norm, a_w_in, a_conv_w, a_conv_b, a_w_out, b_norm, b_w_in, b_q_norm, b_k_norm, b_w_out, c_norm, c_w_in, c_q_lat_norm, c_kv_lat_norm, c_w_uq, c_w_ukv, c_q_norm, c_k_norm, c_w_out):
    depth = a_norm.shape[0] + b_norm.shape[0] + c_norm.shape[0]
    pos3 = positions.reshape(positions.shape + (1,))
    for i in range(depth):
        kind, j = i % 3, i // 3
        if kind == 0:
            x = _conv_layer(x, j, a_norm[j], a_w_in, a_conv_w[j], a_conv_b[j], a_w_out)
        elif kind == 1:
            x = _dsa_layer(x, pos3, b_norm[j], b_w_in[j], b_q_norm[j], b_k_norm[j], b_w_out[j])
        else:
            x = _mla_layer(x, pos3, c_norm[j], c_w_in[j], c_q_lat_norm[j], c_kv_lat_norm[j],
                           c_w_uq[j], c_w_ukv[j], c_q_norm[j], c_k_norm[j], c_w_out[j])
    return x
```

```python
import functools
import math

import jax
import jax.numpy as jnp
from jax import lax
from jax.experimental import pallas as pl
from jax.experimental.pallas import tpu as pltpu

EPS = 1e-6
ROPE_THETA = 10000.0
CHUNK_SHIFT = 6
TOPK_MAX = 256
LANES = 128
NEG_BIG = -1e30
INT_MIN = -(2 ** 31)
F32_MAX = 3.4028234663852886e38
LOG2E = math.log2(math.e)
SEARCH_ROW_GROUPS = 4
LOGIT_BOUND = 64.0
VMEM_LIMIT = 56 * 1024 * 1024

BF16 = jnp.bfloat16
F32 = jnp.float32


def _dot(a, b):
    return jnp.dot(a, b, preferred_element_type=F32)


def _dot_nt(a, b):
    return lax.dot_general(a, b, (((1,), (1,)), ((), ())), preferred_element_type=F32)


def _rms(x, n):
    return x * lax.rsqrt(jnp.sum(x * x, axis=-1, keepdims=True) * (1.0 / n) + EPS)


def _silu(z):
    return z * jax.nn.sigmoid(z)


def _rope_tile(t, cos_t, sin_t, first_mask, half):
    partner = jnp.where(first_mask, pltpu.roll(t, LANES - half, 1), pltpu.roll(t, half, 1))
    return t * cos_t + partner * sin_t


def _params(sem):
    return pltpu.CompilerParams(dimension_semantics=sem, vmem_limit_bytes=VMEM_LIMIT)


def _single(shape, index_map):
    return pl.BlockSpec(shape, index_map, pipeline_mode=pl.Buffered(1))


def _const_spec(shape):
    nd = len(shape)
    return _single(shape, lambda *_: (0,) * nd)


def _diag_visible(tq):
    r = lax.broadcasted_iota(jnp.int32, (tq, tq), 0)
    c = lax.broadcasted_iota(jnp.int32, (tq, tq), 1)
    return (c >> CHUNK_SHIFT) <= (r >> CHUNK_SHIFT)


def _softmax_pv(logit, v, *, bounded):
    if not bounded:
        logit = logit - jnp.max(logit, axis=-1, keepdims=True)
    p = jnp.exp2(logit)
    den = jnp.sum(p, axis=-1, keepdims=True)
    return _dot(p.astype(BF16), v) / den


def _logits_bounded(q_gain, k_gain, n, scale):
    bound = 1.02 * n * scale * jnp.max(jnp.abs(q_gain)) * jnp.max(jnp.abs(k_gain))
    return (bound <= LOGIT_BOUND).astype(jnp.int32).reshape(1, 1)


def _per_block_variants(qb, n_blocks, body):
    for i in range(n_blocks):
        pl.when(qb == i)(functools.partial(body, i))


def _conv_kernel(x_ref, g_ref, w_in_f32_ref, cw_ref, cb_ref, w_out_f32_ref, o_ref, u_scr, w_in_ref, w_out_ref,
                 *, tm, width):
    j = pl.program_id(1)

    @pl.when(jnp.logical_and(pl.program_id(0) == 0, j == 0))
    def _():
        w_in_ref[...] = w_in_f32_ref[...].astype(BF16)
        w_out_ref[...] = w_out_f32_ref[...].astype(BF16)

    x = x_ref[0]
    d = x.shape[-1]
    xb = (_rms(x, d) * g_ref[...]).astype(BF16)

    @pl.when(j == 0)
    def _():
        u_scr[0:8, :] = jnp.zeros((8, width), F32)

    @pl.when(j > 0)
    def _():
        u_scr[0:8, :] = u_scr[tm:tm + 8, :]

    bg = _dot(xb, w_in_ref[:, 0 * width:1 * width])
    cg = _dot(xb, w_in_ref[:, 1 * width:2 * width])
    hv = _dot(xb, w_in_ref[:, 2 * width:3 * width])
    z = _dot(xb, w_in_ref[:, 3 * width:4 * width])
    u = cg * hv
    u_scr[8:tm + 8, :] = u
    y = (cw_ref[2:3, :] * u + cw_ref[1:2, :] * u_scr[7:7 + tm, :]
         + cw_ref[0:1, :] * u_scr[6:6 + tm, :] + cb_ref[...])
    g = (bg * y * _silu(z)).astype(BF16)
    o_ref[0] = x + _dot(g, w_out_ref[...])


def _conv_layer(x, layer, g, w_in_all, cw, cb, w_out_all, *, tm=256):
    b, s, d = x.shape
    width = w_out_all.shape[1]
    return pl.pallas_call(
        functools.partial(_conv_kernel, tm=tm, width=width),
        grid=(b, s // tm),
        in_specs=[
            pl.BlockSpec((1, tm, d), lambda bi, i: (bi, i, 0)),
            _const_spec((1, d)),
            _single((None, d, 4 * width), lambda bi, i: (layer, 0, 0)),
            _const_spec((3, width)),
            _const_spec((1, width)),
            _single((None, width, d), lambda bi, i: (layer, 0, 0)),
        ],
        out_specs=pl.BlockSpec((1, tm, d), lambda bi, i: (bi, i, 0)),
        out_shape=jax.ShapeDtypeStruct((b, s, d), F32),
        scratch_shapes=[
            pltpu.VMEM((tm + 8, width), F32),
            pltpu.VMEM((d, 4 * width), BF16),
            pltpu.VMEM((width, d), BF16),
        ],
        compiler_params=_params(("arbitrary", "arbitrary")),
        name="conv_mixer",
    )(x, g.reshape(1, d), w_in_all, cw, cb.reshape(1, width), w_out_all)


def _dsa_proj_kernel(x_ref, pos_ref, g_ref, w_main_ref, w_qi_ref, w_kiwi_ref, qg_ref, kg_ref,
                     inv_ref, sgn_ref,
                     q_ref, k_ref, v_ref, z_ref, qi_ref, ki_ref, wi_ref,
                     *, width, n_pair, n_qi_tile, scale, wi_scale):
    x = x_ref[0]
    d = x.shape[-1]
    tm = x.shape[0]
    xb = (_rms(x, d) * g_ref[...]).astype(BF16)
    pos = pos_ref[0].astype(F32)
    ang_a = pos * inv_ref[0:1, :]
    ang_b = pos * inv_ref[1:2, :]
    cos_a, sin_a = jnp.cos(ang_a), jnp.sin(ang_a) * sgn_ref[0:1, :]
    cos_b, sin_b = jnp.cos(ang_b), jnp.sin(ang_b) * sgn_ref[1:2, :]
    lane = lax.broadcasted_iota(jnp.int32, (tm, LANES), 1)
    lo = lane < 64
    first_b = (lane & 63) < 16
    head0 = (lane & 32) == 0

    q = _dot(xb, w_main_ref[:, 0 * width:1 * width])
    k = _dot(xb, w_main_ref[:, 1 * width:2 * width])
    v = _dot(xb, w_main_ref[:, 2 * width:3 * width])
    z = _dot(xb, w_main_ref[:, 3 * width:4 * width])
    z_ref[0] = z.astype(BF16)

    def head_norm_rope(t, gain):
        sq = t * t
        s0 = jnp.sum(jnp.where(head0, sq, 0.0), axis=-1, keepdims=True)
        s1 = jnp.sum(jnp.where(head0, 0.0, sq), axis=-1, keepdims=True)
        ms = jnp.where(head0, s0, s1) * (1.0 / 64)
        t = t * lax.rsqrt(ms + EPS) * gain
        return t * cos_a + pltpu.roll(t, 64, 1) * sin_a

    for j in range(n_pair):
        sl = slice(j * LANES, (j + 1) * LANES)
        qt = head_norm_rope(q[:, sl], qg_ref[...]) * scale
        q_ref[0, 2 * j] = jnp.where(head0, qt, 0.0).astype(BF16)
        q_ref[0, 2 * j + 1] = jnp.where(head0, 0.0, qt).astype(BF16)
        k_ref[0, j] = head_norm_rope(k[:, sl], kg_ref[...]).astype(BF16)
        v_ref[0, j] = v[:, sl].astype(BF16)

    qi = _dot(xb, w_qi_ref[...])
    for j in range(n_qi_tile):
        t = _rope_tile(qi[:, j * LANES:(j + 1) * LANES], cos_b, sin_b, first_b, 16)
        qi_ref[0, 2 * j] = jnp.where(lo, t, 0.0).astype(BF16)
        qi_ref[0, 2 * j + 1] = jnp.where(lo, 0.0, t).astype(BF16)

    kiwi = _dot(xb, w_kiwi_ref[...])
    ki_ref[0] = _rope_tile(kiwi[:, 0:LANES], cos_b, sin_b, first_b, 16).astype(BF16)
    wi_ref[0] = kiwi[:, LANES:2 * LANES] * wi_scale


def _key_to_float(key):
    return pltpu.bitcast(jnp.where(key >= 0, key, key ^ 0x7FFFFFFF), F32)


def _search_init(tq):
    rg = tq // SEARCH_ROW_GROUPS
    return tuple(jnp.full((rg, 1), INT_MIN, jnp.int32) for _ in range(SEARCH_ROW_GROUPS))


def _search_step(step, bases, score_scr, nt, topk):
    rg = bases[0].shape[0]
    bit = jnp.left_shift(jnp.int32(1), 31 - step)
    out = []
    for g, base_g in enumerate(bases):
        cand = base_g + bit
        hits = jnp.where(score_scr[0:nt, g * rg:(g + 1) * rg, :] >= _key_to_float(cand), 1.0, 0.0)
        n_ge = jnp.sum(jnp.sum(hits, axis=0), axis=1, keepdims=True)
        out.append(jnp.where(n_ge >= float(topk), cand, base_g))
    return tuple(out)


def _select_bias(bases, score_scr, bias_scr, slot, nt, tq, topk):
    topk_f = float(topk)

    def count(pred_fn):
        hits = jnp.where(pred_fn(score_scr[0:nt]), 1.0, 0.0)
        return jnp.sum(jnp.sum(hits, axis=0), axis=1, keepdims=True)

    thr = _key_to_float(jnp.concatenate(bases, axis=0))
    lowest = jnp.full((tq, 1), -F32_MAX, F32)
    thr = jnp.where(count(lambda s: s >= lowest) < topk_f, lowest, thr)
    excess = jnp.max(count(lambda s: s >= thr)) > topk_f

    def plain(_):
        bias_scr[slot, 0:nt] = jnp.where(score_scr[0:nt] >= thr, 0.0, NEG_BIG)
        return 0

    def index_ordered_ties(_):
        need = topk_f - count(lambda s: s > thr)
        r = lax.broadcasted_iota(jnp.int32, (LANES, LANES), 0)
        c = lax.broadcasted_iota(jnp.int32, (LANES, LANES), 1)
        upper = jnp.where(r < c, 1.0, 0.0).astype(BF16)

        def tile_step(t, carry):
            s = score_scr[t]
            tie_f = jnp.where(s == thr, 1.0, 0.0)
            before = _dot(tie_f.astype(BF16), upper) + carry
            tie_bias = jnp.where(s == thr, jnp.where(before < need, 0.0, NEG_BIG), NEG_BIG)
            bias_scr[slot, t] = jnp.where(s > thr, 0.0, tie_bias)
            return carry + jnp.sum(tie_f, axis=1, keepdims=True)

        lax.fori_loop(0, nt, tile_step, jnp.zeros((tq, 1), F32))
        return 0

    lax.cond(excess, index_ordered_ties, plain, 0)


def _dsa_attn_kernel(bounded_ref, q_ref, qi_ref, wi_ref, k_ref, v_ref, ki_ref, z_ref, x_ref, w_out_ref,
                     o_ref, score_scr, bias_scr, o_scr, wi_scr, *, tq, n_blocks, topk, n_idx_heads, n_pair,
                     unroll_small, unroll_large):
    qb = pl.program_id(1)
    bounded = bounded_ref[0, 0] != 0
    lo = lax.broadcasted_iota(jnp.int32, (tq, LANES), 1) < 64
    tq_tiles = tq // LANES
    n_tiles = score_scr.shape[0]
    steps_per_pair = 32 // n_pair

    def pair_out(j, nvis, slot, is_bounded):
        kt = k_ref[0, j, 0:nvis, :]
        vt = v_ref[0, j, 0:nvis, :]
        qq = q_ref[0, pl.ds(2 * j, 2)].reshape(2 * tq, LANES)
        logit = _dot_nt(qq, kt)
        bias = jnp.concatenate([bias_scr[slot, t] for t in range(nvis // LANES)], axis=1)
        o_even = _softmax_pv(logit[0:tq] + bias, vt, bounded=is_bounded)
        o_odd = _softmax_pv(logit[tq:2 * tq] + bias, vt, bounded=is_bounded)
        return jnp.where(lo, o_even, o_odd)

    def diag_bias_tiles(slot, nt):
        diag_vis = _diag_visible(tq)
        for d in range(tq_tiles):
            bias_scr[slot, nt - tq_tiles + d] = jnp.where(diag_vis[:, d * LANES:(d + 1) * LANES], 0.0, NEG_BIG)

    wi = wi_ref[0]
    for h in range(n_idx_heads):
        wi_scr[h] = jnp.broadcast_to(wi[:, h:h + 1], (tq, LANES))

    def body(i):
        cur, nxt = i % 2, (i + 1) % 2
        nvis = (i + 1) * tq
        if i == 0:
            diag_bias_tiles(cur, nvis // LANES)
        nvis2 = nvis + tq
        nt2 = nvis2 // LANES
        search_next = i + 1 < n_blocks and nvis2 > topk
        if i + 1 < n_blocks and not search_next:
            diag_bias_tiles(nxt, nt2)
        if search_next:
            score_scr[0:nt2] = jnp.zeros((nt2, tq, LANES), F32)

            def idx_head(h, _):
                rel = jnp.maximum(_dot_nt(qi_ref[0, h], ki_ref[0, 0:nvis2, :]), 0.0)
                w = wi_scr[h]
                for t in range(nt2):
                    score_scr[t] += w * rel[:, t * LANES:(t + 1) * LANES]
                return 0

            lax.fori_loop(0, n_idx_heads, idx_head, 0, unroll=2)
            diag_vis = _diag_visible(tq)
            for d in range(tq_tiles):
                t = nt2 - tq_tiles + d
                score_scr[t] = jnp.where(diag_vis[:, d * LANES:(d + 1) * LANES], score_scr[t], -jnp.inf)

        def pair_step(j, bases):
            o_scr[j] = pair_out(j, nvis, cur, True)
            for s_ in range(steps_per_pair if search_next else 0):
                bases = _search_step(j * steps_per_pair + s_, bases, score_scr, nt2, topk)
            return bases

        bases = lax.fori_loop(0, n_pair, pair_step, _search_init(tq) if search_next else (),
                              unroll=unroll_small if 2 * i < n_blocks else unroll_large)
        if search_next:
            _select_bias(bases, score_scr, bias_scr, nxt, nt2, tq, topk)

    _per_block_variants(qb, n_blocks, body)

    @pl.when(jnp.logical_not(bounded))
    def _():
        slot = qb % 2

        def mask_tile(t, _):
            bias_scr[slot, t] = jnp.full((tq, LANES), NEG_BIG, F32)
            return 0

        lax.fori_loop((qb + 1) * tq_tiles, n_tiles, mask_tile, 0)

        def pair_step(j, _):
            o_scr[j] = pair_out(j, n_tiles * LANES, slot, False)
            return 0

        lax.fori_loop(0, n_pair, pair_step, 0)

    o = jnp.concatenate([o_scr[j] for j in range(n_pair)], axis=1)
    g = (o * _silu(z_ref[0].astype(F32))).astype(BF16)
    o_ref[0] = x_ref[0] + _dot(g, w_out_ref[...])


def _rope_inv(half):
    return ROPE_THETA ** (-jnp.arange(half, dtype=F32) / half)


def _dsa_layer(x, pos3, g, w_in, q_norm, k_norm, w_out, *, tm=256, tq=256):
    b, s, d = x.shape
    width = w_out.shape[0]
    head_dim = q_norm.shape[0]
    assert head_dim == 64 and width % LANES == 0
    n_pair = width // LANES
    idx_dim = 64
    n_idx_heads = (w_in.shape[1] - 4 * width - idx_dim) // (idx_dim + 1)
    assert 4 * width + n_idx_heads * idx_dim + idx_dim + n_idx_heads == w_in.shape[1]
    n_qi_tile = n_idx_heads * idx_dim // LANES
    c0 = 4 * width
    c1 = c0 + n_idx_heads * idx_dim
    c2 = c1 + idx_dim
    half = head_dim // 2

    def pair_split(t):
        t4 = t.reshape(t.shape[:-1] + (n_pair, 2, 2, half))
        return jnp.swapaxes(t4, -3, -2).reshape(t.shape)

    w_main = jnp.concatenate([pair_split(w_in[:, :width]), pair_split(w_in[:, width:2 * width]),
                              w_in[:, 2 * width:c0]], axis=1).astype(BF16)
    w_qi = w_in[:, c0:c1].astype(BF16)
    w_ki = w_in[:, c1:c2]
    w_wi = jnp.pad(w_in[:, c2:], ((0, 0), (0, LANES - n_idx_heads)))
    w_kiwi = jnp.concatenate([w_ki, w_ki, w_wi], axis=1).astype(BF16)

    inv32 = _rope_inv(32)
    inv16 = _rope_inv(16)
    zeros32 = jnp.zeros((32,), F32)
    inv_a = jnp.tile(inv32, 4)
    inv_b = jnp.tile(jnp.concatenate([inv16, inv16, zeros32]), 2)
    inv = jnp.stack([inv_a, inv_b])
    sgn_a = jnp.concatenate([-jnp.ones((64,), F32), jnp.ones((64,), F32)])
    sgn_b = jnp.tile(jnp.concatenate([-jnp.ones((16,), F32), jnp.ones((48,), F32)]), 2)
    sgn = jnp.stack([sgn_a, sgn_b])

    def pair_gain(gn):
        return jnp.concatenate([gn[:half], gn[:half], gn[half:], gn[half:]]).reshape(1, LANES)

    qg = pair_gain(q_norm)
    kg = pair_gain(k_norm)

    topk = min(TOPK_MAX, s // 4)
    assert s % tq == 0 and tq % (1 << CHUNK_SHIFT) == 0
    n_heads = 2 * n_pair
    row = lambda bi, i: (bi, i, 0)
    head_row = lambda bi, i: (bi, 0, i, 0)
    q, k, v, z, qi, ki, wi = pl.pallas_call(
        functools.partial(_dsa_proj_kernel, width=width, n_pair=n_pair, n_qi_tile=n_qi_tile,
                          scale=head_dim ** -0.5 * LOG2E,
                          wi_scale=n_idx_heads ** -0.5 * idx_dim ** -0.5),
        grid=(b, s // tm),
        in_specs=[
            pl.BlockSpec((1, tm, d), row),
            pl.BlockSpec((1, tm, 1), row),
            _const_spec((1, d)),
            _const_spec((d, c0)),
            _const_spec((d, c1 - c0)),
            _const_spec((d, 2 * LANES)),
            _const_spec((1, LANES)),
            _const_spec((1, LANES)),
            _const_spec((2, LANES)),
            _const_spec((2, LANES)),
        ],
        out_specs=[
            pl.BlockSpec((1, n_heads, tm, LANES), head_row),
            pl.BlockSpec((1, n_pair, tm, LANES), head_row),
            pl.BlockSpec((1, n_pair, tm, LANES), head_row),
            pl.BlockSpec((1, tm, width), row),
            pl.BlockSpec((1, n_idx_heads, tm, LANES), head_row),
            pl.BlockSpec((1, tm, LANES), row),
            pl.BlockSpec((1, tm, LANES), row),
        ],
        out_shape=[
            jax.ShapeDtypeStruct((b, n_heads, s, LANES), BF16),
            jax.ShapeDtypeStruct((b, n_pair, s, LANES), BF16),
            jax.ShapeDtypeStruct((b, n_pair, s, LANES), BF16),
            jax.ShapeDtypeStruct((b, s, width), BF16),
            jax.ShapeDtypeStruct((b, n_idx_heads, s, LANES), BF16),
            jax.ShapeDtypeStruct((b, s, LANES), BF16),
            jax.ShapeDtypeStruct((b, s, LANES), F32),
        ],
        compiler_params=_params(("arbitrary", "arbitrary")),
        name="dsa_proj",
    )(x, pos3, g.reshape(1, d), w_main, w_qi, w_kiwi, qg, kg, inv, sgn)

    qrow = lambda bi, i: (bi, i, 0)
    qhead = lambda bi, i: (bi, 0, i, 0)
    full = lambda bi, i: (bi, 0, 0, 0)
    n_blocks = s // tq
    assert tq <= topk and 32 % n_pair == 0
    nxt_head = lambda bi, i: (bi, 0, jnp.minimum(i + 1, n_blocks - 1), 0)
    nxt_row = lambda bi, i: (bi, jnp.minimum(i + 1, n_blocks - 1), 0)
    return pl.pallas_call(
        functools.partial(_dsa_attn_kernel, tq=tq, n_blocks=n_blocks, topk=topk,
                          n_idx_heads=n_idx_heads, n_pair=n_pair, unroll_small=2, unroll_large=1),
        grid=(b, n_blocks),
        in_specs=[
            pl.BlockSpec(memory_space=pltpu.SMEM),
            pl.BlockSpec((1, n_heads, tq, LANES), qhead),
            pl.BlockSpec((1, n_idx_heads, tq, LANES), nxt_head),
            pl.BlockSpec((1, tq, LANES), nxt_row),
            pl.BlockSpec((1, n_pair, s, LANES), full),
            pl.BlockSpec((1, n_pair, s, LANES), full),
            pl.BlockSpec((1, s, LANES), lambda bi, i: (bi, 0, 0)),
            pl.BlockSpec((1, tq, width), qrow),
            pl.BlockSpec((1, tq, d), qrow),
            _single((width, d), lambda bi, i: (0, 0)),
        ],
        out_specs=pl.BlockSpec((1, tq, d), qrow),
        out_shape=jax.ShapeDtypeStruct((b, s, d), F32),
        scratch_shapes=[
            pltpu.VMEM((s // LANES, tq, LANES), F32),
            pltpu.VMEM((2, s // LANES, tq, LANES), F32),
            pltpu.VMEM((n_pair, tq, LANES), F32),
            pltpu.VMEM((n_idx_heads, tq, LANES), F32),
        ],
        compiler_params=_params(("arbitrary", "arbitrary")),
        name="dsa_attn",
    )(_logits_bounded(q_norm, k_norm, head_dim, head_dim ** -0.5 * LOG2E),
      q, qi, wi, k, v, ki, z, x, w_out.astype(BF16))


def _mla_proj_kernel(x_ref, pos_ref, g_ref, w_in_ref, qlat_g_ref, kvlat_g_ref, w_uq_ref, w_uq_rot_ref,
                     w_uk_ref, w_uv_ref, gains_ref, inv_ref, sgn_ref,
                     q_ref, k_ref, v_ref, z_ref,
                     *, q_lora, kv_lora, n_heads, qk_dim, scale):
    x = x_ref[0]
    d = x.shape[-1]
    xb = (_rms(x, d) * g_ref[...]).astype(BF16)
    pos = pos_ref[0].astype(F32)
    ang = pos * inv_ref[...]
    cos_t, sin_t = jnp.cos(ang), jnp.sin(ang) * sgn_ref[...]
    cos_q = cos_t * (gains_ref[0:1, :] * scale)
    sin_q = sin_t * (gains_ref[1:2, :] * scale)
    kg = gains_ref[2:3, :]

    c1 = q_lora + kv_lora
    cq = _dot(xb, w_in_ref[:, 0:q_lora])
    ckv = _dot(xb, w_in_ref[:, q_lora:c1])
    kr = _dot(xb, w_in_ref[:, c1:c1 + LANES])
    kr_rot = _dot(xb, w_in_ref[:, c1 + LANES:c1 + 2 * LANES])
    z = _dot(xb, w_in_ref[:, c1 + 2 * LANES:])
    z_ref[0] = z.astype(BF16)

    cq_b = (_rms(cq, q_lora) * qlat_g_ref[...]).astype(BF16)
    q = _dot(cq_b, w_uq_ref[...])
    q_rot = _dot(cq_b, w_uq_rot_ref[...])
    ckv_b = (_rms(ckv, kv_lora) * kvlat_g_ref[...]).astype(BF16)
    kn = _dot(ckv_b, w_uk_ref[...])
    v = _dot(ckv_b, w_uv_ref[...])

    k_rope = kr * (kg * cos_t) + kr_rot * (gains_ref[3:4, :] * sin_t)
    kr_ss = jnp.sum(kr * kr, axis=-1, keepdims=True)
    inv_n = 1.0 / qk_dim
    for h in range(n_heads):
        sl = slice(h * LANES, (h + 1) * LANES)
        qh = q[:, sl]
        q_s = lax.rsqrt(jnp.sum(qh * qh, axis=-1, keepdims=True) * inv_n + EPS)
        q_ref[0, h] = (q_s * (qh * cos_q + q_rot[:, sl] * sin_q)).astype(BF16)
        kh = kn[:, sl]
        k_s = lax.rsqrt((jnp.sum(kh * kh, axis=-1, keepdims=True) + kr_ss) * inv_n + EPS)
        k_ref[0, h] = (k_s * (kh * kg + k_rope)).astype(BF16)
    for j in range(n_heads // 2):
        v_ref[0, j] = v[:, j * LANES:(j + 1) * LANES].astype(BF16)


def _mla_attn_kernel(bounded_ref, q_ref, k_ref, v_ref, z_ref, x_ref, w_out_ref, o_ref, o_scr,
                     *, tq, seq, n_blocks, n_pair):
    qb = pl.program_id(1)
    bounded = bounded_ref[0, 0] != 0
    lo = lax.broadcasted_iota(jnp.int32, (tq, LANES), 1) < 64

    def body(i):
        nvis = (i + 1) * tq
        diag_bias = jnp.where(_diag_visible(tq), 0.0, NEG_BIG)

        def head_out(h, vt):
            logit = _dot_nt(q_ref[0, h], k_ref[0, h, 0:nvis, :])
            last = logit[:, nvis - tq:nvis] + diag_bias
            if nvis > tq:
                logit = jnp.concatenate([logit[:, 0:nvis - tq], last], axis=1)
            else:
                logit = last
            return _softmax_pv(logit, vt, bounded=True)

        def pair_step(j, _):
            vt = v_ref[0, j, 0:nvis, :]
            o_scr[j] = jnp.where(lo, head_out(2 * j, vt), head_out(2 * j + 1, vt))
            return 0

        lax.fori_loop(0, n_pair, pair_step, 0, unroll=4)

    @pl.when(bounded)
    def _():
        _per_block_variants(qb, n_blocks, body)

    @pl.when(jnp.logical_not(bounded))
    def _():
        s_idx = lax.broadcasted_iota(jnp.int32, (tq, seq), 1)
        q_chunk = (qb * tq + lax.broadcasted_iota(jnp.int32, (tq, seq), 0)) >> CHUNK_SHIFT
        bias = jnp.where((s_idx >> CHUNK_SHIFT) <= q_chunk, 0.0, NEG_BIG)

        def pair_step(j, _):
            vt = v_ref[0, j]
            outs = [_softmax_pv(_dot_nt(q_ref[0, 2 * j + e], k_ref[0, 2 * j + e]) + bias, vt, bounded=False)
                    for e in range(2)]
            o_scr[j] = jnp.where(lo, outs[0], outs[1])
            return 0

        lax.fori_loop(0, n_pair, pair_step, 0)

    o = jnp.concatenate([o_scr[j] for j in range(n_pair)], axis=1)
    g = (o * _silu(z_ref[0].astype(F32))).astype(BF16)
    o_ref[0] = x_ref[0] + _dot(g, w_out_ref[...])


def _mla_layer(x, pos3, g, w_in, q_lat_norm, kv_lat_norm, w_uq, w_ukv, q_norm, k_norm, w_out,
               *, tm=256, tq=256):
    b, s, d = x.shape
    width = w_out.shape[0]
    q_lora = q_lat_norm.shape[0]
    kv_lora = kv_lat_norm.shape[0]
    qk_dim = q_norm.shape[0]
    v_dim = 64
    nope = 64
    rope_dim = qk_dim - nope
    n_heads = width // v_dim
    assert nope + rope_dim <= LANES and w_ukv.shape[1] == n_heads * (nope + v_dim)
    assert w_in.shape[1] == q_lora + kv_lora + rope_dim + width
    assert s % tq == 0 and tq % (1 << CHUNK_SHIFT) == 0
    pad = LANES - qk_dim

    rope_half = rope_dim // 2

    def rope_lanes(t):
        return jnp.pad(t, [(0, 0)] * (t.ndim - 1) + [(nope, LANES - nope - rope_dim)])

    def partner(t):
        return jnp.concatenate([t[..., rope_half:], t[..., :rope_half]], axis=-1)

    c1 = q_lora + kv_lora
    w_kr = w_in[:, c1:c1 + rope_dim]
    w_in_p = jnp.concatenate([w_in[:, :c1], rope_lanes(w_kr), rope_lanes(partner(w_kr)),
                              w_in[:, c1 + rope_dim:]], axis=1).astype(BF16)
    w_uq3 = w_uq.reshape(q_lora, n_heads, qk_dim)
    w_uq_p = jnp.pad(w_uq3, ((0, 0), (0, 0), (0, pad))).reshape(q_lora, n_heads * LANES).astype(BF16)
    w_uq_rot = rope_lanes(partner(w_uq3[:, :, nope:])).reshape(q_lora, n_heads * LANES).astype(BF16)
    w_ukv3 = w_ukv.reshape(kv_lora, n_heads, nope + v_dim)
    w_uk_p = jnp.pad(w_ukv3[:, :, :nope], ((0, 0), (0, 0), (0, LANES - nope)))
    w_uk_p = w_uk_p.reshape(kv_lora, n_heads * LANES).astype(BF16)
    w_uv = w_ukv3[:, :, nope:].reshape(kv_lora, n_heads * v_dim).astype(BF16)
    gains = jnp.stack([jnp.pad(q_norm, (0, pad)), rope_lanes(partner(q_norm[nope:])),
                       jnp.pad(k_norm, (0, pad)), rope_lanes(partner(k_norm[nope:]))])

    inv_h = _rope_inv(rope_half)
    inv = jnp.concatenate([jnp.zeros((nope,), F32), inv_h, inv_h,
                           jnp.zeros((LANES - nope - rope_dim,), F32)]).reshape(1, LANES)
    sgn = jnp.concatenate([jnp.ones((nope,), F32), -jnp.ones((rope_half,), F32),
                           jnp.ones((LANES - nope - rope_half,), F32)]).reshape(1, LANES)

    n_pair = n_heads // 2
    row = lambda bi, i: (bi, i, 0)
    head_row = lambda bi, i: (bi, 0, i, 0)
    n_in = w_in_p.shape[1]
    q, k, v, z = pl.pallas_call(
        functools.partial(_mla_proj_kernel, q_lora=q_lora, kv_lora=kv_lora, n_heads=n_heads,
                          qk_dim=qk_dim, scale=qk_dim ** -0.5 * LOG2E),
        grid=(b, s // tm),
        in_specs=[
            pl.BlockSpec((1, tm, d), row),
            pl.BlockSpec((1, tm, 1), row),
            _const_spec((1, d)),
            _const_spec((d, n_in)),
            _const_spec((1, q_lora)),
            _const_spec((1, kv_lora)),
            _const_spec((q_lora, n_heads * LANES)),
            _const_spec((q_lora, n_heads * LANES)),
            _const_spec((kv_lora, n_heads * LANES)),
            _const_spec((kv_lora, n_heads * v_dim)),
            _const_spec((4, LANES)),
            _const_spec((1, LANES)),
            _const_spec((1, LANES)),
        ],
        out_specs=[
            pl.BlockSpec((1, n_heads, tm, LANES), head_row),
            pl.BlockSpec((1, n_heads, tm, LANES), head_row),
            pl.BlockSpec((1, n_pair, tm, LANES), head_row),
            pl.BlockSpec((1, tm, width), row),
        ],
        out_shape=[
            jax.ShapeDtypeStruct((b, n_heads, s, LANES), BF16),
            jax.ShapeDtypeStruct((b, n_heads, s, LANES), BF16),
            jax.ShapeDtypeStruct((b, n_pair, s, LANES), BF16),
            jax.ShapeDtypeStruct((b, s, width), BF16),
        ],
        compiler_params=_params(("arbitrary", "arbitrary")),
        name="mla_proj",
    )(x, pos3, g.reshape(1, d), w_in_p, q_lat_norm.reshape(1, q_lora), kv_lat_norm.reshape(1, kv_lora),
      w_uq_p, w_uq_rot, w_uk_p, w_uv, gains, inv, sgn)

    qrow = lambda bi, i: (bi, i, 0)
    qhead = lambda bi, i: (bi, 0, i, 0)
    full = lambda bi, i: (bi, 0, 0, 0)
    return pl.pallas_call(
        functools.partial(_mla_attn_kernel, tq=tq, seq=s, n_blocks=s // tq, n_pair=n_pair),
        grid=(b, s // tq),
        in_specs=[
            pl.BlockSpec(memory_space=pltpu.SMEM),
            pl.BlockSpec((1, n_heads, tq, LANES), qhead),
            pl.BlockSpec((1, n_heads, s, LANES), full),
            pl.BlockSpec((1, n_pair, s, LANES), full),
            pl.BlockSpec((1, tq, width), qrow),
            pl.BlockSpec((1, tq, d), qrow),
            _single((width, d), lambda bi, i: (0, 0)),
        ],
        out_specs=pl.BlockSpec((1, tq, d), qrow),
        out_shape=jax.ShapeDtypeStruct((b, s, d), F32),
        scratch_shapes=[
            pltpu.VMEM((n_pair, tq, LANES), F32),
        ],
        compiler_params=_params(("arbitrary", "arbitrary")),
        name="mla_attn",
    )(_logits_bounded(q_norm, k_norm, qk_dim, qk_dim ** -0.5 * LOG2E), q, k, v, z, x, w_out.astype(BF16))


def kernel(x, positions, a_norm, a_w_in, a_conv_w, a_conv_b, a_w_out, b_norm, b_w_in, b_q_norm, b_k_norm, b_w_out, c_norm, c_w_in, c_q_lat_norm, c_kv_lat_norm, c_w_uq, c_w_ukv, c_q_norm, c_k_norm, c_w_out):
    depth = a_norm.shape[0] + b_norm.shape[0] + c_norm.shape[0]
    pos3 = positions.reshape(positions.shape + (1,))
    for i in range(depth):
        kind, j = i % 3, i // 3
        if kind == 0:
            x = _conv_layer(x, j, a_norm[j], a_w_in, a_conv_w[j], a_conv_b[j], a_w_out)
        elif kind == 1:
            x = _dsa_layer(x, pos3, b_norm[j], b_w_in[j], b_q_norm[j], b_k_norm[j], b_w_out[j])
        else:
            x = _mla_layer(x, pos3, c_norm[j], c_w_in[j], c_q_lat_norm[j], c_kv_lat_norm[j],
                           c_w_uq[j], c_w_ukv[j], c_q_norm[j], c_k_norm[j], c_w_out[j])
    return x
```

```python
import functools
import math

import jax
import jax.numpy as jnp
from jax import lax
from jax.experimental import pallas as pl
from jax.experimental.pallas import tpu as pltpu

EPS = 1e-6
ROPE_THETA = 10000.0
CHUNK_SHIFT = 6
TOPK_MAX = 256
LANES = 128
NEG_BIG = -1e30
INT_MIN = -(2 ** 31)
F32_MAX = 3.4028234663852886e38
LOG2E = math.log2(math.e)
SEARCH_ROW_GROUPS = 4
LOGIT_BOUND = 64.0
VMEM_LIMIT = 56 * 1024 * 1024

BF16 = jnp.bfloat16
F32 = jnp.float32


def _dot(a, b):
    return jnp.dot(a, b, preferred_element_type=F32)


def _dot_nt(a, b):
    return lax.dot_general(a, b, (((1,), (1,)), ((), ())), preferred_element_type=F32)


def _rms(x, n):
    return x * lax.rsqrt(jnp.sum(x * x, axis=-1, keepdims=True) * (1.0 / n) + EPS)


def _silu(z):
    return z * jax.nn.sigmoid(z)


def _rope_tile(t, cos_t, sin_t, first_mask, half):
    partner = jnp.where(first_mask, pltpu.roll(t, LANES - half, 1), pltpu.roll(t, half, 1))
    return t * cos_t + partner * sin_t


def _params(sem):
    return pltpu.CompilerParams(dimension_semantics=sem, vmem_limit_bytes=VMEM_LIMIT)


def _single(shape, index_map):
    return pl.BlockSpec(shape, index_map, pipeline_mode=pl.Buffered(1))


def _const_spec(shape):
    nd = len(shape)
    return _single(shape, lambda *_: (0,) * nd)


def _diag_visible(tq):
    r = lax.broadcasted_iota(jnp.int32, (tq, tq), 0)
    c = lax.broadcasted_iota(jnp.int32, (tq, tq), 1)
    return (c >> CHUNK_SHIFT) <= (r >> CHUNK_SHIFT)


def _softmax_pv(logit, v, *, bounded):
    if not bounded:
        logit = logit - jnp.max(logit, axis=-1, keepdims=True)
    p = jnp.exp2(logit).astype(BF16)
    pv = _dot(p, jnp.concatenate([v, jnp.ones_like(v)], axis=1))
    return pv[:, 0:LANES] / pv[:, LANES:2 * LANES]


def _logits_bounded(q_gain, k_gain, n, scale):
    bound = 1.02 * n * scale * jnp.max(jnp.abs(q_gain)) * jnp.max(jnp.abs(k_gain))
    return (bound <= LOGIT_BOUND).astype(jnp.int32).reshape(1, 1)


def _per_block_variants(qb, n_blocks, body):
    for i in range(n_blocks):
        pl.when(qb == i)(functools.partial(body, i))


def _conv_kernel(x_ref, g_ref, w_in_f32_ref, cw_ref, cb_ref, w_out_f32_ref, o_ref, u_scr, w_in_ref, w_out_ref,
                 *, tm, width):
    j = pl.program_id(1)

    @pl.when(jnp.logical_and(pl.program_id(0) == 0, j == 0))
    def _():
        w_in_ref[...] = w_in_f32_ref[...].astype(BF16)
        w_out_ref[...] = w_out_f32_ref[...].astype(BF16)

    x = x_ref[0]
    d = x.shape[-1]
    xb = (_rms(x, d) * g_ref[...]).astype(BF16)

    @pl.when(j == 0)
    def _():
        u_scr[0:8, :] = jnp.zeros((8, width), F32)

    @pl.when(j > 0)
    def _():
        u_scr[0:8, :] = u_scr[tm:tm + 8, :]

    bg = _dot(xb, w_in_ref[:, 0 * width:1 * width])
    cg = _dot(xb, w_in_ref[:, 1 * width:2 * width])
    hv = _dot(xb, w_in_ref[:, 2 * width:3 * width])
    z = _dot(xb, w_in_ref[:, 3 * width:4 * width])
    u = cg * hv
    u_scr[8:tm + 8, :] = u
    y = (cw_ref[2:3, :] * u + cw_ref[1:2, :] * u_scr[7:7 + tm, :]
         + cw_ref[0:1, :] * u_scr[6:6 + tm, :] + cb_ref[...])
    g = (bg * y * _silu(z)).astype(BF16)
    o_ref[0] = x + _dot(g, w_out_ref[...])


def _conv_layer(x, layer, g, w_in_all, cw, cb, w_out_all, *, tm=256):
    b, s, d = x.shape
    width = w_out_all.shape[1]
    return pl.pallas_call(
        functools.partial(_conv_kernel, tm=tm, width=width),
        grid=(b, s // tm),
        in_specs=[
            pl.BlockSpec((1, tm, d), lambda bi, i: (bi, i, 0)),
            _const_spec((1, d)),
            _single((None, d, 4 * width), lambda bi, i: (layer, 0, 0)),
            _const_spec((3, width)),
            _const_spec((1, width)),
            _single((None, width, d), lambda bi, i: (layer, 0, 0)),
        ],
        out_specs=pl.BlockSpec((1, tm, d), lambda bi, i: (bi, i, 0)),
        out_shape=jax.ShapeDtypeStruct((b, s, d), F32),
        scratch_shapes=[
            pltpu.VMEM((tm + 8, width), F32),
            pltpu.VMEM((d, 4 * width), BF16),
            pltpu.VMEM((width, d), BF16),
        ],
        compiler_params=_params(("arbitrary", "arbitrary")),
        name="conv_mixer",
    )(x, g.reshape(1, d), w_in_all, cw, cb.reshape(1, width), w_out_all)


def _dsa_proj_kernel(x_ref, pos_ref, g_ref, w_main_ref, w_qi_ref, w_kiwi_ref, qg_ref, kg_ref,
                     inv_ref, sgn_ref,
                     q_ref, k_ref, v_ref, z_ref, qi_ref, ki_ref, wi_ref,
                     *, width, n_pair, n_qi_tile, scale, wi_scale):
    x = x_ref[0]
    d = x.shape[-1]
    tm = x.shape[0]
    xb = (_rms(x, d) * g_ref[...]).astype(BF16)
    pos = pos_ref[0].astype(F32)
    ang_a = pos * inv_ref[0:1, :]
    ang_b = pos * inv_ref[1:2, :]
    cos_a, sin_a = jnp.cos(ang_a), jnp.sin(ang_a) * sgn_ref[0:1, :]
    cos_b, sin_b = jnp.cos(ang_b), jnp.sin(ang_b) * sgn_ref[1:2, :]
    lane = lax.broadcasted_iota(jnp.int32, (tm, LANES), 1)
    lo = lane < 64
    first_b = (lane & 63) < 16
    head0 = (lane & 32) == 0

    q = _dot(xb, w_main_ref[:, 0 * width:1 * width])
    k = _dot(xb, w_main_ref[:, 1 * width:2 * width])
    v = _dot(xb, w_main_ref[:, 2 * width:3 * width])
    z = _dot(xb, w_main_ref[:, 3 * width:4 * width])
    z_ref[0] = z.astype(BF16)

    def head_norm_rope(t, gain):
        sq = t * t
        s0 = jnp.sum(jnp.where(head0, sq, 0.0), axis=-1, keepdims=True)
        s1 = jnp.sum(jnp.where(head0, 0.0, sq), axis=-1, keepdims=True)
        ms = jnp.where(head0, s0, s1) * (1.0 / 64)
        t = t * lax.rsqrt(ms + EPS) * gain
        return t * cos_a + pltpu.roll(t, 64, 1) * sin_a

    for j in range(n_pair):
        sl = slice(j * LANES, (j + 1) * LANES)
        qt = head_norm_rope(q[:, sl], qg_ref[...]) * scale
        q_ref[0, 2 * j] = jnp.where(head0, qt, 0.0).astype(BF16)
        q_ref[0, 2 * j + 1] = jnp.where(head0, 0.0, qt).astype(BF16)
        k_ref[0, j] = head_norm_rope(k[:, sl], kg_ref[...]).astype(BF16)
        v_ref[0, j] = v[:, sl].astype(BF16)

    qi = _dot(xb, w_qi_ref[...])
    for j in range(n_qi_tile):
        t = _rope_tile(qi[:, j * LANES:(j + 1) * LANES], cos_b, sin_b, first_b, 16)
        qi_ref[0, 2 * j] = jnp.where(lo, t, 0.0).astype(BF16)
        qi_ref[0, 2 * j + 1] = jnp.where(lo, 0.0, t).astype(BF16)

    kiwi = _dot(xb, w_kiwi_ref[...])
    ki_ref[0] = _rope_tile(kiwi[:, 0:LANES], cos_b, sin_b, first_b, 16).astype(BF16)
    wi_ref[0] = kiwi[:, LANES:2 * LANES] * wi_scale


def _key_to_float(key):
    return pltpu.bitcast(jnp.where(key >= 0, key, key ^ 0x7FFFFFFF), F32)


def _search_init(tq):
    rg = tq // SEARCH_ROW_GROUPS
    return tuple(jnp.full((rg, 1), INT_MIN, jnp.int32) for _ in range(SEARCH_ROW_GROUPS))


def _search_step(step, bases, score_scr, nt, topk):
    rg = bases[0].shape[0]
    bit = jnp.left_shift(jnp.int32(1), 31 - step)
    out = []
    for g, base_g in enumerate(bases):
        cand = base_g + bit
        hits = jnp.where(score_scr[0:nt, g * rg:(g + 1) * rg, :] >= _key_to_float(cand), 1.0, 0.0)
        n_ge = jnp.sum(jnp.sum(hits, axis=0), axis=1, keepdims=True)
        out.append(jnp.where(n_ge >= float(topk), cand, base_g))
    return tuple(out)


def _select_bias(bases, score_scr, bias_scr, slot, nt, tq, topk):
    topk_f = float(topk)

    def count(pred_fn):
        hits = jnp.where(pred_fn(score_scr[0:nt]), 1.0, 0.0)
        return jnp.sum(jnp.sum(hits, axis=0), axis=1, keepdims=True)

    thr = _key_to_float(jnp.concatenate(bases, axis=0))
    lowest = jnp.full((tq, 1), -F32_MAX, F32)
    thr = jnp.where(count(lambda s: s >= lowest) < topk_f, lowest, thr)
    excess = jnp.max(count(lambda s: s >= thr)) > topk_f

    def plain(_):
        bias_scr[slot, 0:nt] = jnp.where(score_scr[0:nt] >= thr, 0.0, NEG_BIG)
        return 0

    def index_ordered_ties(_):
        need = topk_f - count(lambda s: s > thr)
        r = lax.broadcasted_iota(jnp.int32, (LANES, LANES), 0)
        c = lax.broadcasted_iota(jnp.int32, (LANES, LANES), 1)
        upper = jnp.where(r < c, 1.0, 0.0).astype(BF16)

        def tile_step(t, carry):
            s = score_scr[t]
            tie_f = jnp.where(s == thr, 1.0, 0.0)
            before = _dot(tie_f.astype(BF16), upper) + carry
            tie_bias = jnp.where(s == thr, jnp.where(before < need, 0.0, NEG_BIG), NEG_BIG)
            bias_scr[slot, t] = jnp.where(s > thr, 0.0, tie_bias)
            return carry + jnp.sum(tie_f, axis=1, keepdims=True)

        lax.fori_loop(0, nt, tile_step, jnp.zeros((tq, 1), F32))
        return 0

    lax.cond(excess, index_ordered_ties, plain, 0)


def _dsa_attn_kernel(bounded_ref, q_ref, qi_ref, wi_ref, k_ref, v_ref, ki_ref, z_ref, x_ref, w_out_ref,
                     o_ref, score_scr, bias_scr, o_scr, wi_scr, *, tq, n_blocks, topk, n_idx_heads, n_pair,
                     unroll_small, unroll_large):
    qb = pl.program_id(1)
    bounded = bounded_ref[0, 0] != 0
    lo = lax.broadcasted_iota(jnp.int32, (tq, LANES), 1) < 64
    tq_tiles = tq // LANES
    n_tiles = score_scr.shape[0]
    steps_per_pair = 32 // n_pair

    def pair_out(j, nvis, slot, is_bounded):
        kt = k_ref[0, j, 0:nvis, :]
        vt = v_ref[0, j, 0:nvis, :]
        qq = q_ref[0, pl.ds(2 * j, 2)].reshape(2 * tq, LANES)
        logit = _dot_nt(qq, kt)
        bias = jnp.concatenate([bias_scr[slot, t] for t in range(nvis // LANES)], axis=1)
        o_even = _softmax_pv(logit[0:tq] + bias, vt, bounded=is_bounded)
        o_odd = _softmax_pv(logit[tq:2 * tq] + bias, vt, bounded=is_bounded)
        return jnp.where(lo, o_even, o_odd)

    def diag_bias_tiles(slot, nt):
        diag_vis = _diag_visible(tq)
        for d in range(tq_tiles):
            bias_scr[slot, nt - tq_tiles + d] = jnp.where(diag_vis[:, d * LANES:(d + 1) * LANES], 0.0, NEG_BIG)

    wi = wi_ref[0]
    for h in range(n_idx_heads):
        wi_scr[h] = jnp.broadcast_to(wi[:, h:h + 1], (tq, LANES))

    def body(i):
        cur, nxt = i % 2, (i + 1) % 2
        nvis = (i + 1) * tq
        if i == 0:
            diag_bias_tiles(cur, nvis // LANES)
        nvis2 = nvis + tq
        nt2 = nvis2 // LANES
        search_next = i + 1 < n_blocks and nvis2 > topk
        if i + 1 < n_blocks and not search_next:
            diag_bias_tiles(nxt, nt2)
        if search_next:
            score_scr[0:nt2] = jnp.zeros((nt2, tq, LANES), F32)

            def idx_head(h, _):
                rel = jnp.maximum(_dot_nt(qi_ref[0, h], ki_ref[0, 0:nvis2, :]), 0.0)
                w = wi_scr[h]
                for t in range(nt2):
                    score_scr[t] += w * rel[:, t * LANES:(t + 1) * LANES]
                return 0

            lax.fori_loop(0, n_idx_heads, idx_head, 0, unroll=2)
            diag_vis = _diag_visible(tq)
            for d in range(tq_tiles):
                t = nt2 - tq_tiles + d
                score_scr[t] = jnp.where(diag_vis[:, d * LANES:(d + 1) * LANES], score_scr[t], -jnp.inf)

        def pair_step(j, bases):
            o_scr[j] = pair_out(j, nvis, cur, True)
            for s_ in range(steps_per_pair if search_next else 0):
                bases = _search_step(j * steps_per_pair + s_, bases, score_scr, nt2, topk)
            return bases

        bases = lax.fori_loop(0, n_pair, pair_step, _search_init(tq) if search_next else (),
                              unroll=unroll_small if 2 * i < n_blocks else unroll_large)
        if search_next:
            _select_bias(bases, score_scr, bias_scr, nxt, nt2, tq, topk)

    _per_block_variants(qb, n_blocks, body)

    @pl.when(jnp.logical_not(bounded))
    def _():
        slot = qb % 2

        def mask_tile(t, _):
            bias_scr[slot, t] = jnp.full((tq, LANES), NEG_BIG, F32)
            return 0

        lax.fori_loop((qb + 1) * tq_tiles, n_tiles, mask_tile, 0)

        def pair_step(j, _):
            o_scr[j] = pair_out(j, n_tiles * LANES, slot, False)
            return 0

        lax.fori_loop(0, n_pair, pair_step, 0)

    o = jnp.concatenate([o_scr[j] for j in range(n_pair)], axis=1)
    g = (o * _silu(z_ref[0].astype(F32))).astype(BF16)
    o_ref[0] = x_ref[0] + _dot(g, w_out_ref[...])


def _rope_inv(half):
    return ROPE_THETA ** (-jnp.arange(half, dtype=F32) / half)


def _dsa_layer(x, pos3, g, w_in, q_norm, k_norm, w_out, *, tm=256, tq=256):
    b, s, d = x.shape
    width = w_out.shape[0]
    head_dim = q_norm.shape[0]
    assert head_dim == 64 and width % LANES == 0
    n_pair = width // LANES
    idx_dim = 64
    n_idx_heads = (w_in.shape[1] - 4 * width - idx_dim) // (idx_dim + 1)
    assert 4 * width + n_idx_heads * idx_dim + idx_dim + n_idx_heads == w_in.shape[1]
    n_qi_tile = n_idx_heads * idx_dim // LANES
    c0 = 4 * width
    c1 = c0 + n_idx_heads * idx_dim
    c2 = c1 + idx_dim
    half = head_dim // 2

    def pair_split(t):
        t4 = t.reshape(t.shape[:-1] + (n_pair, 2, 2, half))
        return jnp.swapaxes(t4, -3, -2).reshape(t.shape)

    w_main = jnp.concatenate([pair_split(w_in[:, :width]), pair_split(w_in[:, width:2 * width]),
                              w_in[:, 2 * width:c0]], axis=1).astype(BF16)
    w_qi = w_in[:, c0:c1].astype(BF16)
    w_ki = w_in[:, c1:c2]
    w_wi = jnp.pad(w_in[:, c2:], ((0, 0), (0, LANES - n_idx_heads)))
    w_kiwi = jnp.concatenate([w_ki, w_ki, w_wi], axis=1).astype(BF16)

    inv32 = _rope_inv(32)
    inv16 = _rope_inv(16)
    zeros32 = jnp.zeros((32,), F32)
    inv_a = jnp.tile(inv32, 4)
    inv_b = jnp.tile(jnp.concatenate([inv16, inv16, zeros32]), 2)
    inv = jnp.stack([inv_a, inv_b])
    sgn_a = jnp.concatenate([-jnp.ones((64,), F32), jnp.ones((64,), F32)])
    sgn_b = jnp.tile(jnp.concatenate([-jnp.ones((16,), F32), jnp.ones((48,), F32)]), 2)
    sgn = jnp.stack([sgn_a, sgn_b])

    def pair_gain(gn):
        return jnp.concatenate([gn[:half], gn[:half], gn[half:], gn[half:]]).reshape(1, LANES)

    qg = pair_gain(q_norm)
    kg = pair_gain(k_norm)

    topk = min(TOPK_MAX, s // 4)
    assert s % tq == 0 and tq % (1 << CHUNK_SHIFT) == 0
    n_heads = 2 * n_pair
    row = lambda bi, i: (bi, i, 0)
    head_row = lambda bi, i: (bi, 0, i, 0)
    q, k, v, z, qi, ki, wi = pl.pallas_call(
        functools.partial(_dsa_proj_kernel, width=width, n_pair=n_pair, n_qi_tile=n_qi_tile,
                          scale=head_dim ** -0.5 * LOG2E,
                          wi_scale=n_idx_heads ** -0.5 * idx_dim ** -0.5),
        grid=(b, s // tm),
        in_specs=[
            pl.BlockSpec((1, tm, d), row),
            pl.BlockSpec((1, tm, 1), row),
            _const_spec((1, d)),
            _const_spec((d, c0)),
            _const_spec((d, c1 - c0)),
            _const_spec((d, 2 * LANES)),
            _const_spec((1, LANES)),
            _const_spec((1, LANES)),
            _const_spec((2, LANES)),
            _const_spec((2, LANES)),
        ],
        out_specs=[
            pl.BlockSpec((1, n_heads, tm, LANES), head_row),
            pl.BlockSpec((1, n_pair, tm, LANES), head_row),
            pl.BlockSpec((1, n_pair, tm, LANES), head_row),
            pl.BlockSpec((1, tm, width), row),
            pl.BlockSpec((1, n_idx_heads, tm, LANES), head_row),
            pl.BlockSpec((1, tm, LANES), row),
            pl.BlockSpec((1, tm, LANES), row),
        ],
        out_shape=[
            jax.ShapeDtypeStruct((b, n_heads, s, LANES), BF16),
            jax.ShapeDtypeStruct((b, n_pair, s, LANES), BF16),
            jax.ShapeDtypeStruct((b, n_pair, s, LANES), BF16),
            jax.ShapeDtypeStruct((b, s, width), BF16),
            jax.ShapeDtypeStruct((b, n_idx_heads, s, LANES), BF16),
            jax.ShapeDtypeStruct((b, s, LANES), BF16),
            jax.ShapeDtypeStruct((b, s, LANES), F32),
        ],
        compiler_params=_params(("arbitrary", "arbitrary")),
        name="dsa_proj",
    )(x, pos3, g.reshape(1, d), w_main, w_qi, w_kiwi, qg, kg, inv, sgn)

    qrow = lambda bi, i: (bi, i, 0)
    qhead = lambda bi, i: (bi, 0, i, 0)
    full = lambda bi, i: (bi, 0, 0, 0)
    n_blocks = s // tq
    assert tq <= topk and 32 % n_pair == 0
    nxt_head = lambda bi, i: (bi, 0, jnp.minimum(i + 1, n_blocks - 1), 0)
    nxt_row = lambda bi, i: (bi, jnp.minimum(i + 1, n_blocks - 1), 0)
    return pl.pallas_call(
        functools.partial(_dsa_attn_kernel, tq=tq, n_blocks=n_blocks, topk=topk,
                          n_idx_heads=n_idx_heads, n_pair=n_pair, unroll_small=2, unroll_large=1),
        grid=(b, n_blocks),
        in_specs=[
            pl.BlockSpec(memory_space=pltpu.SMEM),
            pl.BlockSpec((1, n_heads, tq, LANES), qhead),
            pl.BlockSpec((1, n_idx_heads, tq, LANES), nxt_head),
            pl.BlockSpec((1, tq, LANES), nxt_row),
            pl.BlockSpec((1, n_pair, s, LANES), full),
            pl.BlockSpec((1, n_pair, s, LANES), full),
            pl.BlockSpec((1, s, LANES), lambda bi, i: (bi, 0, 0)),
            pl.BlockSpec((1, tq, width), qrow),
            pl.BlockSpec((1, tq, d), qrow),
            _single((width, d), lambda bi, i: (0, 0)),
        ],
        out_specs=pl.BlockSpec((1, tq, d), qrow),
        out_shape=jax.ShapeDtypeStruct((b, s, d), F32),
        scratch_shapes=[
            pltpu.VMEM((s // LANES, tq, LANES), F32),
            pltpu.VMEM((2, s // LANES, tq, LANES), F32),
            pltpu.VMEM((n_pair, tq, LANES), F32),
            pltpu.VMEM((n_idx_heads, tq, LANES), F32),
        ],
        compiler_params=_params(("arbitrary", "arbitrary")),
        name="dsa_attn",
    )(_logits_bounded(q_norm, k_norm, head_dim, head_dim ** -0.5 * LOG2E),
      q, qi, wi, k, v, ki, z, x, w_out.astype(BF16))


def _mla_proj_kernel(x_ref, pos_ref, g_ref, w_in_ref, qlat_g_ref, kvlat_g_ref, w_uq_ref, w_uq_rot_ref,
                     w_uk_ref, w_uv_ref, gains_ref, inv_ref, sgn_ref,
                     q_ref, k_ref, v_ref, z_ref,
                     *, q_lora, kv_lora, n_heads, qk_dim, scale):
    x = x_ref[0]
    d = x.shape[-1]
    xb = (_rms(x, d) * g_ref[...]).astype(BF16)
    pos = pos_ref[0].astype(F32)
    ang = pos * inv_ref[...]
    cos_t, sin_t = jnp.cos(ang), jnp.sin(ang) * sgn_ref[...]
    cos_q = cos_t * (gains_ref[0:1, :] * scale)
    sin_q = sin_t * (gains_ref[1:2, :] * scale)
    kg = gains_ref[2:3, :]

    c1 = q_lora + kv_lora
    cq = _dot(xb, w_in_ref[:, 0:q_lora])
    ckv = _dot(xb, w_in_ref[:, q_lora:c1])
    kr = _dot(xb, w_in_ref[:, c1:c1 + LANES])
    kr_rot = _dot(xb, w_in_ref[:, c1 + LANES:c1 + 2 * LANES])
    z = _dot(xb, w_in_ref[:, c1 + 2 * LANES:])
    z_ref[0] = z.astype(BF16)

    cq_b = (_rms(cq, q_lora) * qlat_g_ref[...]).astype(BF16)
    q = _dot(cq_b, w_uq_ref[...])
    q_rot = _dot(cq_b, w_uq_rot_ref[...])
    ckv_b = (_rms(ckv, kv_lora) * kvlat_g_ref[...]).astype(BF16)
    kn = _dot(ckv_b, w_uk_ref[...])
    v = _dot(ckv_b, w_uv_ref[...])

    k_rope = kr * (kg * cos_t) + kr_rot * (gains_ref[3:4, :] * sin_t)
    kr_ss = jnp.sum(kr * kr, axis=-1, keepdims=True)
    inv_n = 1.0 / qk_dim
    for h in range(n_heads):
        sl = slice(h * LANES, (h + 1) * LANES)
        qh = q[:, sl]
        q_s = lax.rsqrt(jnp.sum(qh * qh, axis=-1, keepdims=True) * inv_n + EPS)
        q_ref[0, h] = (q_s * (qh * cos_q + q_rot[:, sl] * sin_q)).astype(BF16)
        kh = kn[:, sl]
        k_s = lax.rsqrt((jnp.sum(kh * kh, axis=-1, keepdims=True) + kr_ss) * inv_n + EPS)
        k_ref[0, h] = (k_s * (kh * kg + k_rope)).astype(BF16)
    for j in range(n_heads // 2):
        v_ref[0, j] = v[:, j * LANES:(j + 1) * LANES].astype(BF16)


def _mla_attn_kernel(bounded_ref, q_ref, k_ref, v_ref, z_ref, x_ref, w_out_ref, o_ref, o_scr,
                     *, tq, seq, n_blocks, n_pair):
    qb = pl.program_id(1)
    bounded = bounded_ref[0, 0] != 0
    lo = lax.broadcasted_iota(jnp.int32, (tq, LANES), 1) < 64

    def body(i):
        nvis = (i + 1) * tq
        diag_bias = jnp.where(_diag_visible(tq), 0.0, NEG_BIG)

        def head_out(h, vt):
            logit = _dot_nt(q_ref[0, h], k_ref[0, h, 0:nvis, :])
            last = logit[:, nvis - tq:nvis] + diag_bias
            if nvis > tq:
                logit = jnp.concatenate([logit[:, 0:nvis - tq], last], axis=1)
            else:
                logit = last
            return _softmax_pv(logit, vt, bounded=True)

        def pair_step(j, _):
            vt = v_ref[0, j, 0:nvis, :]
            o_scr[j] = jnp.where(lo, head_out(2 * j, vt), head_out(2 * j + 1, vt))
            return 0

        lax.fori_loop(0, n_pair, pair_step, 0, unroll=4)

    @pl.when(bounded)
    def _():
        _per_block_variants(qb, n_blocks, body)

    @pl.when(jnp.logical_not(bounded))
    def _():
        s_idx = lax.broadcasted_iota(jnp.int32, (tq, seq), 1)
        q_chunk = (qb * tq + lax.broadcasted_iota(jnp.int32, (tq, seq), 0)) >> CHUNK_SHIFT
        bias = jnp.where((s_idx >> CHUNK_SHIFT) <= q_chunk, 0.0, NEG_BIG)

        def pair_step(j, _):
            vt = v_ref[0, j]
            outs = [_softmax_pv(_dot_nt(q_ref[0, 2 * j + e], k_ref[0, 2 * j + e]) + bias, vt, bounded=False)
                    for e in range(2)]
            o_scr[j] = jnp.where(lo, outs[0], outs[1])
            return 0

        lax.fori_loop(0, n_pair, pair_step, 0)

    o = jnp.concatenate([o_scr[j] for j in range(n_pair)], axis=1)
    g = (o * _silu(z_ref[0].astype(F32))).astype(BF16)
    o_ref[0] = x_ref[0] + _dot(g, w_out_ref[...])


def _mla_layer(x, pos3, g, w_in, q_lat_norm, kv_lat_norm, w_uq, w_ukv, q_norm, k_norm, w_out,
               *, tm=256, tq=256):
    b, s, d = x.shape
    width = w_out.shape[0]
    q_lora = q_lat_norm.shape[0]
    kv_lora = kv_lat_norm.shape[0]
    qk_dim = q_norm.shape[0]
    v_dim = 64
    nope = 64
    rope_dim = qk_dim - nope
    n_heads = width // v_dim
    assert nope + rope_dim <= LANES and w_ukv.shape[1] == n_heads * (nope + v_dim)
    assert w_in.shape[1] == q_lora + kv_lora + rope_dim + width
    assert s % tq == 0 and tq % (1 << CHUNK_SHIFT) == 0
    pad = LANES - qk_dim

    rope_half = rope_dim // 2

    def rope_lanes(t):
        return jnp.pad(t, [(0, 0)] * (t.ndim - 1) + [(nope, LANES - nope - rope_dim)])

    def partner(t):
        return jnp.concatenate([t[..., rope_half:], t[..., :rope_half]], axis=-1)

    c1 = q_lora + kv_lora
    w_kr = w_in[:, c1:c1 + rope_dim]
    w_in_p = jnp.concatenate([w_in[:, :c1], rope_lanes(w_kr), rope_lanes(partner(w_kr)),
                              w_in[:, c1 + rope_dim:]], axis=1).astype(BF16)
    w_uq3 = w_uq.reshape(q_lora, n_heads, qk_dim)
    w_uq_p = jnp.pad(w_uq3, ((0, 0), (0, 0), (0, pad))).reshape(q_lora, n_heads * LANES).astype(BF16)
    w_uq_rot = rope_lanes(partner(w_uq3[:, :, nope:])).reshape(q_lora, n_heads * LANES).astype(BF16)
    w_ukv3 = w_ukv.reshape(kv_lora, n_heads, nope + v_dim)
    w_uk_p = jnp.pad(w_ukv3[:, :, :nope], ((0, 0), (0, 0), (0, LANES - nope)))
    w_uk_p = w_uk_p.reshape(kv_lora, n_heads * LANES).astype(BF16)
    w_uv = w_ukv3[:, :, nope:].reshape(kv_lora, n_heads * v_dim).astype(BF16)
    gains = jnp.stack([jnp.pad(q_norm, (0, pad)), rope_lanes(partner(q_norm[nope:])),
                       jnp.pad(k_norm, (0, pad)), rope_lanes(partner(k_norm[nope:]))])

    inv_h = _rope_inv(rope_half)
    inv = jnp.concatenate([jnp.zeros((nope,), F32), inv_h, inv_h,
                           jnp.zeros((LANES - nope - rope_dim,), F32)]).reshape(1, LANES)
    sgn = jnp.concatenate([jnp.ones((nope,), F32), -jnp.ones((rope_half,), F32),
                           jnp.ones((LANES - nope - rope_half,), F32)]).reshape(1, LANES)

    n_pair = n_heads // 2
    row = lambda bi, i: (bi, i, 0)
    head_row = lambda bi, i: (bi, 0, i, 0)
    n_in = w_in_p.shape[1]
    q, k, v, z = pl.pallas_call(
        functools.partial(_mla_proj_kernel, q_lora=q_lora, kv_lora=kv_lora, n_heads=n_heads,
                          qk_dim=qk_dim, scale=qk_dim ** -0.5 * LOG2E),
        grid=(b, s // tm),
        in_specs=[
            pl.BlockSpec((1, tm, d), row),
            pl.BlockSpec((1, tm, 1), row),
            _const_spec((1, d)),
            _const_spec((d, n_in)),
            _const_spec((1, q_lora)),
            _const_spec((1, kv_lora)),
            _const_spec((q_lora, n_heads * LANES)),
            _const_spec((q_lora, n_heads * LANES)),
            _const_spec((kv_lora, n_heads * LANES)),
            _const_spec((kv_lora, n_heads * v_dim)),
            _const_spec((4, LANES)),
            _const_spec((1, LANES)),
            _const_spec((1, LANES)),
        ],
        out_specs=[
            pl.BlockSpec((1, n_heads, tm, LANES), head_row),
            pl.BlockSpec((1, n_heads, tm, LANES), head_row),
            pl.BlockSpec((1, n_pair, tm, LANES), head_row),
            pl.BlockSpec((1, tm, width), row),
        ],
        out_shape=[
            jax.ShapeDtypeStruct((b, n_heads, s, LANES), BF16),
            jax.ShapeDtypeStruct((b, n_heads, s, LANES), BF16),
            jax.ShapeDtypeStruct((b, n_pair, s, LANES), BF16),
            jax.ShapeDtypeStruct((b, s, width), BF16),
        ],
        compiler_params=_params(("arbitrary", "arbitrary")),
        name="mla_proj",
    )(x, pos3, g.reshape(1, d), w_in_p, q_lat_norm.reshape(1, q_lora), kv_lat_norm.reshape(1, kv_lora),
      w_uq_p, w_uq_rot, w_uk_p, w_uv, gains, inv, sgn)

    qrow = lambda bi, i: (bi, i, 0)
    qhead = lambda bi, i: (bi, 0, i, 0)
    full = lambda bi, i: (bi, 0, 0, 0)
    return pl.pallas_call(
        functools.partial(_mla_attn_kernel, tq=tq, seq=s, n_blocks=s // tq, n_pair=n_pair),
        grid=(b, s // tq),
        in_specs=[
            pl.BlockSpec(memory_space=pltpu.SMEM),
            pl.BlockSpec((1, n_heads, tq, LANES), qhead),
            pl.BlockSpec((1, n_heads, s, LANES), full),
            pl.BlockSpec((1, n_pair, s, LANES), full),
            pl.BlockSpec((1, tq, width), qrow),
            pl.BlockSpec((1, tq, d), qrow),
            _single((width, d), lambda bi, i: (0, 0)),
        ],
        out_specs=pl.BlockSpec((1, tq, d), qrow),
        out_shape=jax.ShapeDtypeStruct((b, s, d), F32),
        scratch_shapes=[
            pltpu.VMEM((n_pair, tq, LANES), F32),
        ],
        compiler_params=_params(("arbitrary", "arbitrary")),
        name="mla_attn",
    )(_logits_bounded(q_norm, k_norm, qk_dim, qk_dim ** -0.5 * LOG2E), q, k, v, z, x, w_out.astype(BF16))


def kernel(x, positions, a_norm, a_w_in, a_conv_w, a_conv_b, a_w_out, b_norm, b_w_in, b_q_norm, b_k_norm, b_w_out, c_norm, c_w_in, c_q_lat_norm, c_kv_lat_norm, c_w_uq, c_w_ukv, c_q_norm, c_k_norm, c_w_out):
    depth = a_norm.shape[0] + b_norm.shape[0] + c_norm.shape[0]
    pos3 = positions.reshape(positions.shape + (1,))
    for i in range(depth):
        kind, j = i % 3, i // 3
        if kind == 0:
            x = _conv_layer(x, j, a_norm[j], a_w_in, a_conv_w[j], a_conv_b[j], a_w_out)
        elif kind == 1:
            x = _dsa_layer(x, pos3, b_norm[j], b_w_in[j], b_q_norm[j], b_k_norm[j], b_w_out[j])
        else:
            x = _mla_layer(x, pos3, c_norm[j], c_w_in[j], c_q_lat_norm[j], c_kv_lat_norm[j],
                           c_w_uq[j], c_w_ukv[j], c_q_norm[j], c_k_norm[j], c_w_out[j])
    return x
```

```python
import functools
import math

import jax
import jax.numpy as jnp
from jax import lax
from jax.experimental import pallas as pl
from jax.experimental.pallas import tpu as pltpu

EPS = 1e-6
ROPE_THETA = 10000.0
CHUNK_SHIFT = 6
TOPK_MAX = 256
LANES = 128
NEG_BIG = -1e30
INT_MIN = -(2 ** 31)
F32_MAX = 3.4028234663852886e38
LOG2E = math.log2(math.e)
SEARCH_ROW_GROUPS = 4
LOGIT_BOUND = 64.0
VMEM_LIMIT = 56 * 1024 * 1024

BF16 = jnp.bfloat16
F32 = jnp.float32


def _dot(a, b):
    return jnp.dot(a, b, preferred_element_type=F32)


def _dot_nt(a, b):
    return lax.dot_general(a, b, (((1,), (1,)), ((), ())), preferred_element_type=F32)


def _rms(x, n):
    return x * lax.rsqrt(jnp.sum(x * x, axis=-1, keepdims=True) * (1.0 / n) + EPS)


def _silu(z):
    return z * jax.nn.sigmoid(z)


def _rope_tile(t, cos_t, sin_t, first_mask, half):
    partner = jnp.where(first_mask, pltpu.roll(t, LANES - half, 1), pltpu.roll(t, half, 1))
    return t * cos_t + partner * sin_t


def _params(sem):
    return pltpu.CompilerParams(dimension_semantics=sem, vmem_limit_bytes=VMEM_LIMIT)


def _single(shape, index_map):
    return pl.BlockSpec(shape, index_map, pipeline_mode=pl.Buffered(1))


def _const_spec(shape):
    nd = len(shape)
    return _single(shape, lambda *_: (0,) * nd)


def _diag_visible(tq):
    r = lax.broadcasted_iota(jnp.int32, (tq, tq), 0)
    c = lax.broadcasted_iota(jnp.int32, (tq, tq), 1)
    return (c >> CHUNK_SHIFT) <= (r >> CHUNK_SHIFT)


def _softmax_pv(logit, v, *, bounded):
    if not bounded:
        logit = logit - jnp.max(logit, axis=-1, keepdims=True)
    p = jnp.exp2(logit).astype(BF16)
    pv = _dot(p, jnp.concatenate([v, jnp.ones_like(v)], axis=1))
    return pv[:, 0:LANES] / pv[:, LANES:2 * LANES]


def _logits_bounded(q_gain, k_gain, n, scale):
    bound = 1.02 * n * scale * jnp.max(jnp.abs(q_gain)) * jnp.max(jnp.abs(k_gain))
    return (bound <= LOGIT_BOUND).astype(jnp.int32).reshape(1, 1)


def _per_block_variants(qb, n_blocks, body):
    for i in range(n_blocks):
        pl.when(qb == i)(functools.partial(body, i))


def _conv_kernel(x_ref, g_ref, w_in_f32_ref, cw_ref, cb_ref, w_out_f32_ref, o_ref, u_scr, w_in_ref, w_out_ref,
                 *, tm, width):
    j = pl.program_id(1)

    @pl.when(jnp.logical_and(pl.program_id(0) == 0, j == 0))
    def _():
        w_in_ref[...] = w_in_f32_ref[...].astype(BF16)
        w_out_ref[...] = w_out_f32_ref[...].astype(BF16)

    x = x_ref[0]
    d = x.shape[-1]
    xb = (_rms(x, d) * g_ref[...]).astype(BF16)

    @pl.when(j == 0)
    def _():
        u_scr[0:8, :] = jnp.zeros((8, width), F32)

    @pl.when(j > 0)
    def _():
        u_scr[0:8, :] = u_scr[tm:tm + 8, :]

    bg = _dot(xb, w_in_ref[:, 0 * width:1 * width])
    cg = _dot(xb, w_in_ref[:, 1 * width:2 * width])
    hv = _dot(xb, w_in_ref[:, 2 * width:3 * width])
    z = _dot(xb, w_in_ref[:, 3 * width:4 * width])
    u = cg * hv
    u_scr[8:tm + 8, :] = u
    y = (cw_ref[2:3, :] * u + cw_ref[1:2, :] * u_scr[7:7 + tm, :]
         + cw_ref[0:1, :] * u_scr[6:6 + tm, :] + cb_ref[...])
    g = (bg * y * _silu(z)).astype(BF16)
    o_ref[0] = x + _dot(g, w_out_ref[...])


def _conv_layer(x, layer, g, w_in_all, cw, cb, w_out_all, *, tm=256):
    b, s, d = x.shape
    width = w_out_all.shape[1]
    return pl.pallas_call(
        functools.partial(_conv_kernel, tm=tm, width=width),
        grid=(b, s // tm),
        in_specs=[
            pl.BlockSpec((1, tm, d), lambda bi, i: (bi, i, 0)),
            _const_spec((1, d)),
            _single((None, d, 4 * width), lambda bi, i: (layer, 0, 0)),
            _const_spec((3, width)),
            _const_spec((1, width)),
            _single((None, width, d), lambda bi, i: (layer, 0, 0)),
        ],
        out_specs=pl.BlockSpec((1, tm, d), lambda bi, i: (bi, i, 0)),
        out_shape=jax.ShapeDtypeStruct((b, s, d), F32),
        scratch_shapes=[
            pltpu.VMEM((tm + 8, width), F32),
            pltpu.VMEM((d, 4 * width), BF16),
            pltpu.VMEM((width, d), BF16),
        ],
        compiler_params=_params(("arbitrary", "arbitrary")),
        name="conv_mixer",
    )(x, g.reshape(1, d), w_in_all, cw, cb.reshape(1, width), w_out_all)


def _dsa_proj_kernel(x_ref, pos_ref, g_ref, w_main_ref, w_qi_ref, w_kiwi_ref, qg_ref, kg_ref,
                     inv_ref, sgn_ref,
                     q_ref, k_ref, v_ref, z_ref, qi_ref, ki_ref, wi_ref,
                     *, width, n_pair, n_qi_tile, scale, wi_scale):
    x = x_ref[0]
    d = x.shape[-1]
    tm = x.shape[0]
    xb = (_rms(x, d) * g_ref[...]).astype(BF16)
    pos = pos_ref[0].astype(F32)
    ang_a = pos * inv_ref[0:1, :]
    ang_b = pos * inv_ref[1:2, :]
    cos_a, sin_a = jnp.cos(ang_a), jnp.sin(ang_a) * sgn_ref[0:1, :]
    cos_b, sin_b = jnp.cos(ang_b), jnp.sin(ang_b) * sgn_ref[1:2, :]
    lane = lax.broadcasted_iota(jnp.int32, (tm, LANES), 1)
    lo = lane < 64
    first_b = (lane & 63) < 16
    head0 = (lane & 32) == 0

    q = _dot(xb, w_main_ref[:, 0 * width:1 * width])
    k = _dot(xb, w_main_ref[:, 1 * width:2 * width])
    v = _dot(xb, w_main_ref[:, 2 * width:3 * width])
    z = _dot(xb, w_main_ref[:, 3 * width:4 * width])
    z_ref[0] = z.astype(BF16)

    def head_norm_rope(t, gain):
        sq = t * t
        s0 = jnp.sum(jnp.where(head0, sq, 0.0), axis=-1, keepdims=True)
        s1 = jnp.sum(jnp.where(head0, 0.0, sq), axis=-1, keepdims=True)
        ms = jnp.where(head0, s0, s1) * (1.0 / 64)
        t = t * lax.rsqrt(ms + EPS) * gain
        return t * cos_a + pltpu.roll(t, 64, 1) * sin_a

    for j in range(n_pair):
        sl = slice(j * LANES, (j + 1) * LANES)
        qt = head_norm_rope(q[:, sl], qg_ref[...]) * scale
        q_ref[0, 2 * j] = jnp.where(head0, qt, 0.0).astype(BF16)
        q_ref[0, 2 * j + 1] = jnp.where(head0, 0.0, qt).astype(BF16)
        k_ref[0, j] = head_norm_rope(k[:, sl], kg_ref[...]).astype(BF16)
        v_ref[0, j] = v[:, sl].astype(BF16)

    qi = _dot(xb, w_qi_ref[...])
    for j in range(n_qi_tile):
        t = _rope_tile(qi[:, j * LANES:(j + 1) * LANES], cos_b, sin_b, first_b, 16)
        qi_ref[0, 2 * j] = jnp.where(lo, t, 0.0).astype(BF16)
        qi_ref[0, 2 * j + 1] = jnp.where(lo, 0.0, t).astype(BF16)

    kiwi = _dot(xb, w_kiwi_ref[...])
    ki_ref[0] = _rope_tile(kiwi[:, 0:LANES], cos_b, sin_b, first_b, 16).astype(BF16)
    wi_ref[0] = kiwi[:, LANES:2 * LANES] * wi_scale


def _key_to_float(key):
    return pltpu.bitcast(jnp.where(key >= 0, key, key ^ 0x7FFFFFFF), F32)


def _search_init(tq):
    rg = tq // SEARCH_ROW_GROUPS
    return tuple(jnp.full((rg, 1), INT_MIN, jnp.int32) for _ in range(SEARCH_ROW_GROUPS))


def _search_step(step, bases, score_scr, nt, topk):
    rg = bases[0].shape[0]
    bit = jnp.left_shift(jnp.int32(1), 31 - step)
    out = []
    for g, base_g in enumerate(bases):
        cand = base_g + bit
        hits = jnp.where(score_scr[0:nt, g * rg:(g + 1) * rg, :] >= _key_to_float(cand), 1.0, 0.0)
        n_ge = jnp.sum(jnp.sum(hits, axis=0), axis=1, keepdims=True)
        out.append(jnp.where(n_ge >= float(topk), cand, base_g))
    return tuple(out)


def _select_bias(bases, score_scr, bias_scr, slot, nt, tq, topk):
    topk_f = float(topk)

    def count(pred_fn):
        hits = jnp.where(pred_fn(score_scr[0:nt]), 1.0, 0.0)
        return jnp.sum(jnp.sum(hits, axis=0), axis=1, keepdims=True)

    thr = _key_to_float(jnp.concatenate(bases, axis=0))
    lowest = jnp.full((tq, 1), -F32_MAX, F32)
    thr = jnp.where(count(lambda s: s >= lowest) < topk_f, lowest, thr)
    excess = jnp.max(count(lambda s: s >= thr)) > topk_f

    def plain(_):
        bias_scr[slot, 0:nt] = jnp.where(score_scr[0:nt] >= thr, 0.0, NEG_BIG)
        return 0

    def index_ordered_ties(_):
        need = topk_f - count(lambda s: s > thr)
        r = lax.broadcasted_iota(jnp.int32, (LANES, LANES), 0)
        c = lax.broadcasted_iota(jnp.int32, (LANES, LANES), 1)
        upper = jnp.where(r < c, 1.0, 0.0).astype(BF16)

        def tile_step(t, carry):
            s = score_scr[t]
            tie_f = jnp.where(s == thr, 1.0, 0.0)
            before = _dot(tie_f.astype(BF16), upper) + carry
            tie_bias = jnp.where(s == thr, jnp.where(before < need, 0.0, NEG_BIG), NEG_BIG)
            bias_scr[slot, t] = jnp.where(s > thr, 0.0, tie_bias)
            return carry + jnp.sum(tie_f, axis=1, keepdims=True)

        lax.fori_loop(0, nt, tile_step, jnp.zeros((tq, 1), F32))
        return 0

    lax.cond(excess, index_ordered_ties, plain, 0)


def _dsa_attn_kernel(bounded_ref, q_ref, qi_ref, wi_ref, k_ref, v_ref, ki_ref, z_ref, x_ref, w_out_ref,
                     o_ref, score_scr, bias_scr, o_scr, wi_scr, *, tq, n_blocks, topk, n_idx_heads, n_pair,
                     unroll_small, unroll_large):
    qb = pl.program_id(1)
    bounded = bounded_ref[0, 0] != 0
    lo = lax.broadcasted_iota(jnp.int32, (tq, LANES), 1) < 64
    tq_tiles = tq // LANES
    n_tiles = score_scr.shape[0]
    steps_per_pair = 32 // n_pair

    def pair_out(j, nvis, slot, is_bounded):
        kt = k_ref[0, j, 0:nvis, :]
        vt = v_ref[0, j, 0:nvis, :]
        qq = q_ref[0, pl.ds(2 * j, 2)].reshape(2 * tq, LANES)
        logit = _dot_nt(qq, kt)
        bias = jnp.concatenate([bias_scr[slot, t] for t in range(nvis // LANES)], axis=1)
        o_even = _softmax_pv(logit[0:tq] + bias, vt, bounded=is_bounded)
        o_odd = _softmax_pv(logit[tq:2 * tq] + bias, vt, bounded=is_bounded)
        return jnp.where(lo, o_even, o_odd)

    def diag_bias_tiles(slot, nt):
        diag_vis = _diag_visible(tq)
        for d in range(tq_tiles):
            bias_scr[slot, nt - tq_tiles + d] = jnp.where(diag_vis[:, d * LANES:(d + 1) * LANES], 0.0, NEG_BIG)

    wi = wi_ref[0]
    for h in range(n_idx_heads):
        wi_scr[h] = jnp.broadcast_to(wi[:, h:h + 1], (tq, LANES))

    def body(i):
        cur, nxt = i % 2, (i + 1) % 2
        nvis = (i + 1) * tq
        if i == 0:
            diag_bias_tiles(cur, nvis // LANES)
        nvis2 = nvis + tq
        nt2 = nvis2 // LANES
        search_next = i + 1 < n_blocks and nvis2 > topk
        if i + 1 < n_blocks and not search_next:
            diag_bias_tiles(nxt, nt2)
        if search_next:
            score_scr[0:nt2] = jnp.zeros((nt2, tq, LANES), F32)

            def idx_head(h, _):
                rel = jnp.maximum(_dot_nt(qi_ref[0, h], ki_ref[0, 0:nvis2, :]), 0.0)
                w = wi_scr[h]
                for t in range(nt2):
                    score_scr[t] += w * rel[:, t * LANES:(t + 1) * LANES]
                return 0

            lax.fori_loop(0, n_idx_heads, idx_head, 0, unroll=2)
            diag_vis = _diag_visible(tq)
            for d in range(tq_tiles):
                t = nt2 - tq_tiles + d
                score_scr[t] = jnp.where(diag_vis[:, d * LANES:(d + 1) * LANES], score_scr[t], -jnp.inf)

        def pair_step(j, bases):
            o_scr[j] = pair_out(j, nvis, cur, True)
            for s_ in range(steps_per_pair if search_next else 0):
                bases = _search_step(j * steps_per_pair + s_, bases, score_scr, nt2, topk)
            return bases

        bases = lax.fori_loop(0, n_pair, pair_step, _search_init(tq) if search_next else (),
                              unroll=unroll_small if 2 * i < n_blocks else unroll_large)
        if search_next:
            _select_bias(bases, score_scr, bias_scr, nxt, nt2, tq, topk)

    _per_block_variants(qb, n_blocks, body)

    @pl.when(jnp.logical_not(bounded))
    def _():
        slot = qb % 2

        def mask_tile(t, _):
            bias_scr[slot, t] = jnp.full((tq, LANES), NEG_BIG, F32)
            return 0

        lax.fori_loop((qb + 1) * tq_tiles, n_tiles, mask_tile, 0)

        def pair_step(j, _):
            o_scr[j] = pair_out(j, n_tiles * LANES, slot, False)
            return 0

        lax.fori_loop(0, n_pair, pair_step, 0)

    o = jnp.concatenate([o_scr[j] for j in range(n_pair)], axis=1)
    g = (o * _silu(z_ref[0].astype(F32))).astype(BF16)
    o_ref[0] = x_ref[0] + _dot(g, w_out_ref[...])


def _rope_inv(half):
    return ROPE_THETA ** (-jnp.arange(half, dtype=F32) / half)


def _dsa_layer(x, pos3, g, w_in, q_norm, k_norm, w_out, *, tm=256, tq=256):
    b, s, d = x.shape
    width = w_out.shape[0]
    head_dim = q_norm.shape[0]
    assert head_dim == 64 and width % LANES == 0
    n_pair = width // LANES
    idx_dim = 64
    n_idx_heads = (w_in.shape[1] - 4 * width - idx_dim) // (idx_dim + 1)
    assert 4 * width + n_idx_heads * idx_dim + idx_dim + n_idx_heads == w_in.shape[1]
    n_qi_tile = n_idx_heads * idx_dim // LANES
    c0 = 4 * width
    c1 = c0 + n_idx_heads * idx_dim
    c2 = c1 + idx_dim
    half = head_dim // 2

    def pair_split(t):
        t4 = t.reshape(t.shape[:-1] + (n_pair, 2, 2, half))
        return jnp.swapaxes(t4, -3, -2).reshape(t.shape)

    w_main = jnp.concatenate([pair_split(w_in[:, :width]), pair_split(w_in[:, width:2 * width]),
                              w_in[:, 2 * width:c0]], axis=1).astype(BF16)
    w_qi = w_in[:, c0:c1].astype(BF16)
    w_ki = w_in[:, c1:c2]
    w_wi = jnp.pad(w_in[:, c2:], ((0, 0), (0, LANES - n_idx_heads)))
    w_kiwi = jnp.concatenate([w_ki, w_ki, w_wi], axis=1).astype(BF16)

    inv32 = _rope_inv(32)
    inv16 = _rope_inv(16)
    zeros32 = jnp.zeros((32,), F32)
    inv_a = jnp.tile(inv32, 4)
    inv_b = jnp.tile(jnp.concatenate([inv16, inv16, zeros32]), 2)
    inv = jnp.stack([inv_a, inv_b])
    sgn_a = jnp.concatenate([-jnp.ones((64,), F32), jnp.ones((64,), F32)])
    sgn_b = jnp.tile(jnp.concatenate([-jnp.ones((16,), F32), jnp.ones((48,), F32)]), 2)
    sgn = jnp.stack([sgn_a, sgn_b])

    def pair_gain(gn):
        return jnp.concatenate([gn[:half], gn[:half], gn[half:], gn[half:]]).reshape(1, LANES)

    qg = pair_gain(q_norm)
    kg = pair_gain(k_norm)

    topk = min(TOPK_MAX, s // 4)
    assert s % tq == 0 and tq % (1 << CHUNK_SHIFT) == 0
    n_heads = 2 * n_pair
    row = lambda bi, i: (bi, i, 0)
    head_row = lambda bi, i: (bi, 0, i, 0)
    q, k, v, z, qi, ki, wi = pl.pallas_call(
        functools.partial(_dsa_proj_kernel, width=width, n_pair=n_pair, n_qi_tile=n_qi_tile,
                          scale=head_dim ** -0.5 * LOG2E,
                          wi_scale=n_idx_heads ** -0.5 * idx_dim ** -0.5),
        grid=(b, s // tm),
        in_specs=[
            pl.BlockSpec((1, tm, d), row),
            pl.BlockSpec((1, tm, 1), row),
            _const_spec((1, d)),
            _const_spec((d, c0)),
            _const_spec((d, c1 - c0)),
            _const_spec((d, 2 * LANES)),
            _const_spec((1, LANES)),
            _const_spec((1, LANES)),
            _const_spec((2, LANES)),
            _const_spec((2, LANES)),
        ],
        out_specs=[
            pl.BlockSpec((1, n_heads, tm, LANES), head_row),
            pl.BlockSpec((1, n_pair, tm, LANES), head_row),
            pl.BlockSpec((1, n_pair, tm, LANES), head_row),
            pl.BlockSpec((1, tm, width), row),
            pl.BlockSpec((1, n_idx_heads, tm, LANES), head_row),
            pl.BlockSpec((1, tm, LANES), row),
            pl.BlockSpec((1, tm, LANES), row),
        ],
        out_shape=[
            jax.ShapeDtypeStruct((b, n_heads, s, LANES), BF16),
            jax.ShapeDtypeStruct((b, n_pair, s, LANES), BF16),
            jax.ShapeDtypeStruct((b, n_pair, s, LANES), BF16),
            jax.ShapeDtypeStruct((b, s, width), BF16),
            jax.ShapeDtypeStruct((b, n_idx_heads, s, LANES), BF16),
            jax.ShapeDtypeStruct((b, s, LANES), BF16),
            jax.ShapeDtypeStruct((b, s, LANES), F32),
        ],
        compiler_params=_params(("arbitrary", "arbitrary")),
        name="dsa_proj",
    )(x, pos3, g.reshape(1, d), w_main, w_qi, w_kiwi, qg, kg, inv, sgn)

    qrow = lambda bi, i: (bi, i, 0)
    qhead = lambda bi, i: (bi, 0, i, 0)
    full = lambda bi, i: (bi, 0, 0, 0)
    n_blocks = s // tq
    assert tq <= topk and 32 % n_pair == 0
    nxt_head = lambda bi, i: (bi, 0, jnp.minimum(i + 1, n_blocks - 1), 0)
    nxt_row = lambda bi, i: (bi, jnp.minimum(i + 1, n_blocks - 1), 0)
    return pl.pallas_call(
        functools.partial(_dsa_attn_kernel, tq=tq, n_blocks=n_blocks, topk=topk,
                          n_idx_heads=n_idx_heads, n_pair=n_pair, unroll_small=2, unroll_large=2),
        grid=(b, n_blocks),
        in_specs=[
            pl.BlockSpec(memory_space=pltpu.SMEM),
            pl.BlockSpec((1, n_heads, tq, LANES), qhead),
            pl.BlockSpec((1, n_idx_heads, tq, LANES), nxt_head),
            pl.BlockSpec((1, tq, LANES), nxt_row),
            pl.BlockSpec((1, n_pair, s, LANES), full),
            pl.BlockSpec((1, n_pair, s, LANES), full),
            pl.BlockSpec((1, s, LANES), lambda bi, i: (bi, 0, 0)),
            pl.BlockSpec((1, tq, width), qrow),
            pl.BlockSpec((1, tq, d), qrow),
            _single((width, d), lambda bi, i: (0, 0)),
        ],
        out_specs=pl.BlockSpec((1, tq, d), qrow),
        out_shape=jax.ShapeDtypeStruct((b, s, d), F32),
        scratch_shapes=[
            pltpu.VMEM((s // LANES, tq, LANES), F32),
            pltpu.VMEM((2, s // LANES, tq, LANES), F32),
            pltpu.VMEM((n_pair, tq, LANES), F32),
            pltpu.VMEM((n_idx_heads, tq, LANES), F32),
        ],
        compiler_params=_params(("arbitrary", "arbitrary")),
        name="dsa_attn",
    )(_logits_bounded(q_norm, k_norm, head_dim, head_dim ** -0.5 * LOG2E),
      q, qi, wi, k, v, ki, z, x, w_out.astype(BF16))


def _mla_proj_kernel(x_ref, pos_ref, g_ref, w_in_ref, qlat_g_ref, kvlat_g_ref, w_uq_ref, w_uq_rot_ref,
                     w_uk_ref, w_uv_ref, gains_ref, inv_ref, sgn_ref,
                     q_ref, k_ref, v_ref, z_ref,
                     *, q_lora, kv_lora, n_heads, qk_dim, scale):
    x = x_ref[0]
    d = x.shape[-1]
    xb = (_rms(x, d) * g_ref[...]).astype(BF16)
    pos = pos_ref[0].astype(F32)
    ang = pos * inv_ref[...]
    cos_t, sin_t = jnp.cos(ang), jnp.sin(ang) * sgn_ref[...]
    cos_q = cos_t * (gains_ref[0:1, :] * scale)
    sin_q = sin_t * (gains_ref[1:2, :] * scale)
    kg = gains_ref[2:3, :]

    c1 = q_lora + kv_lora
    cq = _dot(xb, w_in_ref[:, 0:q_lora])
    ckv = _dot(xb, w_in_ref[:, q_lora:c1])
    kr = _dot(xb, w_in_ref[:, c1:c1 + LANES])
    kr_rot = _dot(xb, w_in_ref[:, c1 + LANES:c1 + 2 * LANES])
    z = _dot(xb, w_in_ref[:, c1 + 2 * LANES:])
    z_ref[0] = z.astype(BF16)

    cq_b = (_rms(cq, q_lora) * qlat_g_ref[...]).astype(BF16)
    q = _dot(cq_b, w_uq_ref[...])
    q_rot = _dot(cq_b, w_uq_rot_ref[...])
    ckv_b = (_rms(ckv, kv_lora) * kvlat_g_ref[...]).astype(BF16)
    kn = _dot(ckv_b, w_uk_ref[...])
    v = _dot(ckv_b, w_uv_ref[...])

    k_rope = kr * (kg * cos_t) + kr_rot * (gains_ref[3:4, :] * sin_t)
    kr_ss = jnp.sum(kr * kr, axis=-1, keepdims=True)
    inv_n = 1.0 / qk_dim
    for h in range(n_heads):
        sl = slice(h * LANES, (h + 1) * LANES)
        qh = q[:, sl]
        q_s = lax.rsqrt(jnp.sum(qh * qh, axis=-1, keepdims=True) * inv_n + EPS)
        q_ref[0, h] = (q_s * (qh * cos_q + q_rot[:, sl] * sin_q)).astype(BF16)
        kh = kn[:, sl]
        k_s = lax.rsqrt((jnp.sum(kh * kh, axis=-1, keepdims=True) + kr_ss) * inv_n + EPS)
        k_ref[0, h] = (k_s * (kh * kg + k_rope)).astype(BF16)
    for j in range(n_heads // 2):
        v_ref[0, j] = v[:, j * LANES:(j + 1) * LANES].astype(BF16)


def _mla_attn_kernel(bounded_ref, q_ref, k_ref, v_ref, z_ref, x_ref, w_out_ref, o_ref, o_scr,
                     *, tq, seq, n_blocks, n_pair):
    qb = pl.program_id(1)
    bounded = bounded_ref[0, 0] != 0
    lo = lax.broadcasted_iota(jnp.int32, (tq, LANES), 1) < 64

    def body(i):
        nvis = (i + 1) * tq
        diag_bias = jnp.where(_diag_visible(tq), 0.0, NEG_BIG)

        def head_out(h, vt):
            logit = _dot_nt(q_ref[0, h], k_ref[0, h, 0:nvis, :])
            last = logit[:, nvis - tq:nvis] + diag_bias
            if nvis > tq:
                logit = jnp.concatenate([logit[:, 0:nvis - tq], last], axis=1)
            else:
                logit = last
            return _softmax_pv(logit, vt, bounded=True)

        def pair_step(j, _):
            vt = v_ref[0, j, 0:nvis, :]
            o_scr[j] = jnp.where(lo, head_out(2 * j, vt), head_out(2 * j + 1, vt))
            return 0

        lax.fori_loop(0, n_pair, pair_step, 0, unroll=4)

    @pl.when(bounded)
    def _():
        _per_block_variants(qb, n_blocks, body)

    @pl.when(jnp.logical_not(bounded))
    def _():
        s_idx = lax.broadcasted_iota(jnp.int32, (tq, seq), 1)
        q_chunk = (qb * tq + lax.broadcasted_iota(jnp.int32, (tq, seq), 0)) >> CHUNK_SHIFT
        bias = jnp.where((s_idx >> CHUNK_SHIFT) <= q_chunk, 0.0, NEG_BIG)

        def pair_step(j, _):
            vt = v_ref[0, j]
            outs = [_softmax_pv(_dot_nt(q_ref[0, 2 * j + e], k_ref[0, 2 * j + e]) + bias, vt, bounded=False)
                    for e in range(2)]
            o_scr[j] = jnp.where(lo, outs[0], outs[1])
            return 0

        lax.fori_loop(0, n_pair, pair_step, 0)

    o = jnp.concatenate([o_scr[j] for j in range(n_pair)], axis=1)
    g = (o * _silu(z_ref[0].astype(F32))).astype(BF16)
    o_ref[0] = x_ref[0] + _dot(g, w_out_ref[...])


def _mla_layer(x, pos3, g, w_in, q_lat_norm, kv_lat_norm, w_uq, w_ukv, q_norm, k_norm, w_out,
               *, tm=256, tq=256):
    b, s, d = x.shape
    width = w_out.shape[0]
    q_lora = q_lat_norm.shape[0]
    kv_lora = kv_lat_norm.shape[0]
    qk_dim = q_norm.shape[0]
    v_dim = 64
    nope = 64
    rope_dim = qk_dim - nope
    n_heads = width // v_dim
    assert nope + rope_dim <= LANES and w_ukv.shape[1] == n_heads * (nope + v_dim)
    assert w_in.shape[1] == q_lora + kv_lora + rope_dim + width
    assert s % tq == 0 and tq % (1 << CHUNK_SHIFT) == 0
    pad = LANES - qk_dim

    rope_half = rope_dim // 2

    def rope_lanes(t):
        return jnp.pad(t, [(0, 0)] * (t.ndim - 1) + [(nope, LANES - nope - rope_dim)])

    def partner(t):
        return jnp.concatenate([t[..., rope_half:], t[..., :rope_half]], axis=-1)

    c1 = q_lora + kv_lora
    w_kr = w_in[:, c1:c1 + rope_dim]
    w_in_p = jnp.concatenate([w_in[:, :c1], rope_lanes(w_kr), rope_lanes(partner(w_kr)),
                              w_in[:, c1 + rope_dim:]], axis=1).astype(BF16)
    w_uq3 = w_uq.reshape(q_lora, n_heads, qk_dim)
    w_uq_p = jnp.pad(w_uq3, ((0, 0), (0, 0), (0, pad))).reshape(q_lora, n_heads * LANES).astype(BF16)
    w_uq_rot = rope_lanes(partner(w_uq3[:, :, nope:])).reshape(q_lora, n_heads * LANES).astype(BF16)
    w_ukv3 = w_ukv.reshape(kv_lora, n_heads, nope + v_dim)
    w_uk_p = jnp.pad(w_ukv3[:, :, :nope], ((0, 0), (0, 0), (0, LANES - nope)))
    w_uk_p = w_uk_p.reshape(kv_lora, n_heads * LANES).astype(BF16)
    w_uv = w_ukv3[:, :, nope:].reshape(kv_lora, n_heads * v_dim).astype(BF16)
    gains = jnp.stack([jnp.pad(q_norm, (0, pad)), rope_lanes(partner(q_norm[nope:])),
                       jnp.pad(k_norm, (0, pad)), rope_lanes(partner(k_norm[nope:]))])

    inv_h = _rope_inv(rope_half)
    inv = jnp.concatenate([jnp.zeros((nope,), F32), inv_h, inv_h,
                           jnp.zeros((LANES - nope - rope_dim,), F32)]).reshape(1, LANES)
    sgn = jnp.concatenate([jnp.ones((nope,), F32), -jnp.ones((rope_half,), F32),
                           jnp.ones((LANES - nope - rope_half,), F32)]).reshape(1, LANES)

    n_pair = n_heads // 2
    row = lambda bi, i: (bi, i, 0)
    head_row = lambda bi, i: (bi, 0, i, 0)
    n_in = w_in_p.shape[1]
    q, k, v, z = pl.pallas_call(
        functools.partial(_mla_proj_kernel, q_lora=q_lora, kv_lora=kv_lora, n_heads=n_heads,
                          qk_dim=qk_dim, scale=qk_dim ** -0.5 * LOG2E),
        grid=(b, s // tm),
        in_specs=[
            pl.BlockSpec((1, tm, d), row),
            pl.BlockSpec((1, tm, 1), row),
            _const_spec((1, d)),
            _const_spec((d, n_in)),
            _const_spec((1, q_lora)),
            _const_spec((1, kv_lora)),
            _const_spec((q_lora, n_heads * LANES)),
            _const_spec((q_lora, n_heads * LANES)),
            _const_spec((kv_lora, n_heads * LANES)),
            _const_spec((kv_lora, n_heads * v_dim)),
            _const_spec((4, LANES)),
            _const_spec((1, LANES)),
            _const_spec((1, LANES)),
        ],
        out_specs=[
            pl.BlockSpec((1, n_heads, tm, LANES), head_row),
            pl.BlockSpec((1, n_heads, tm, LANES), head_row),
            pl.BlockSpec((1, n_pair, tm, LANES), head_row),
            pl.BlockSpec((1, tm, width), row),
        ],
        out_shape=[
            jax.ShapeDtypeStruct((b, n_heads, s, LANES), BF16),
            jax.ShapeDtypeStruct((b, n_heads, s, LANES), BF16),
            jax.ShapeDtypeStruct((b, n_pair, s, LANES), BF16),
            jax.ShapeDtypeStruct((b, s, width), BF16),
        ],
        compiler_params=_params(("arbitrary", "arbitrary")),
        name="mla_proj",
    )(x, pos3, g.reshape(1, d), w_in_p, q_lat_norm.reshape(1, q_lora), kv_lat_norm.reshape(1, kv_lora),
      w_uq_p, w_uq_rot, w_uk_p, w_uv, gains, inv, sgn)

    qrow = lambda bi, i: (bi, i, 0)
    qhead = lambda bi, i: (bi, 0, i, 0)
    full = lambda bi, i: (bi, 0, 0, 0)
    return pl.pallas_call(
        functools.partial(_mla_attn_kernel, tq=tq, seq=s, n_blocks=s // tq, n_pair=n_pair),
        grid=(b, s // tq),
        in_specs=[
            pl.BlockSpec(memory_space=pltpu.SMEM),
            pl.BlockSpec((1, n_heads, tq, LANES), qhead),
            pl.BlockSpec((1, n_heads, s, LANES), full),
            pl.BlockSpec((1, n_pair, s, LANES), full),
            pl.BlockSpec((1, tq, width), qrow),
            pl.BlockSpec((1, tq, d), qrow),
            _single((width, d), lambda bi, i: (0, 0)),
        ],
        out_specs=pl.BlockSpec((1, tq, d), qrow),
        out_shape=jax.ShapeDtypeStruct((b, s, d), F32),
        scratch_shapes=[
            pltpu.VMEM((n_pair, tq, LANES), F32),
        ],
        compiler_params=_params(("arbitrary", "arbitrary")),
        name="mla_attn",
    )(_logits_bounded(q_norm, k_norm, qk_dim, qk_dim ** -0.5 * LOG2E), q, k, v, z, x, w_out.astype(BF16))


def kernel(x, positions, a_norm, a_w_in, a_conv_w, a_conv_b, a_w_out, b_norm, b_w_in, b_q_norm, b_k_norm, b_w_out, c_norm, c_w_in, c_q_lat_norm, c_kv_lat_norm, c_w_uq, c_w_ukv, c_q_norm, c_k_norm, c_w_out):
    depth = a_norm.shape[0] + b_norm.shape[0] + c_norm.shape[0]
    pos3 = positions.reshape(positions.shape + (1,))
    for i in range(depth):
        kind, j = i % 3, i // 3
        if kind == 0:
            x = _conv_layer(x, j, a_norm[j], a_w_in, a_conv_w[j], a_conv_b[j], a_w_out)
        elif kind == 1:
            x = _dsa_layer(x, pos3, b_norm[j], b_w_in[j], b_q_norm[j], b_k_norm[j], b_w_out[j])
        else:
            x = _mla_layer(x, pos3, c_norm[j], c_w_in[j], c_q_lat_norm[j], c_kv_lat_norm[j],
                           c_w_uq[j], c_w_ukv[j], c_q_norm[j], c_k_norm[j], c_w_out[j])
    return x
```

```python
import functools
import math

import jax
import jax.numpy as jnp
from jax import lax
from jax.experimental import pallas as pl
from jax.experimental.pallas import tpu as pltpu

EPS = 1e-6
ROPE_THETA = 10000.0
CHUNK_SHIFT = 6
TOPK_MAX = 256
LANES = 128
NEG_BIG = -1e30
INT_MIN = -(2 ** 31)
F32_MAX = 3.4028234663852886e38
LOG2E = math.log2(math.e)
SEARCH_ROW_GROUPS = 4
LOGIT_BOUND = 64.0
VMEM_LIMIT = 56 * 1024 * 1024

BF16 = jnp.bfloat16
F32 = jnp.float32


def _dot(a, b):
    return jnp.dot(a, b, preferred_element_type=F32)


def _dot_nt(a, b):
    return lax.dot_general(a, b, (((1,), (1,)), ((), ())), preferred_element_type=F32)


def _rms(x, n):
    return x * lax.rsqrt(jnp.sum(x * x, axis=-1, keepdims=True) * (1.0 / n) + EPS)


def _silu(z):
    return z * jax.nn.sigmoid(z)


def _rope_tile(t, cos_t, sin_t, first_mask, half):
    partner = jnp.where(first_mask, pltpu.roll(t, LANES - half, 1), pltpu.roll(t, half, 1))
    return t * cos_t + partner * sin_t


def _params(sem):
    return pltpu.CompilerParams(dimension_semantics=sem, vmem_limit_bytes=VMEM_LIMIT)


def _single(shape, index_map):
    return pl.BlockSpec(shape, index_map, pipeline_mode=pl.Buffered(1))


def _const_spec(shape):
    nd = len(shape)
    return _single(shape, lambda *_: (0,) * nd)


def _diag_visible(tq):
    r = lax.broadcasted_iota(jnp.int32, (tq, tq), 0)
    c = lax.broadcasted_iota(jnp.int32, (tq, tq), 1)
    return (c >> CHUNK_SHIFT) <= (r >> CHUNK_SHIFT)


def _softmax_pv(logit, v, *, bounded):
    if not bounded:
        logit = logit - jnp.max(logit, axis=-1, keepdims=True)
    p = jnp.exp2(logit).astype(BF16)
    pv = _dot(p, jnp.concatenate([v, jnp.ones_like(v)], axis=1))
    return pv[:, 0:LANES] / pv[:, LANES:2 * LANES]


def _logits_bounded(q_gain, k_gain, n, scale):
    bound = 1.02 * n * scale * jnp.max(jnp.abs(q_gain)) * jnp.max(jnp.abs(k_gain))
    return (bound <= LOGIT_BOUND).astype(jnp.int32).reshape(1, 1)


def _per_block_variants(qb, n_blocks, body):
    for i in range(n_blocks):
        pl.when(qb == i)(functools.partial(body, i))


def _conv_kernel(x_ref, g_ref, w_in_f32_ref, cw_ref, cb_ref, w_out_f32_ref, o_ref, u_scr, w_in_ref, w_out_ref,
                 *, tm, width):
    j = pl.program_id(1)

    @pl.when(jnp.logical_and(pl.program_id(0) == 0, j == 0))
    def _():
        w_in_ref[...] = w_in_f32_ref[...].astype(BF16)
        w_out_ref[...] = w_out_f32_ref[...].astype(BF16)

    x = x_ref[0]
    d = x.shape[-1]
    xb = (_rms(x, d) * g_ref[...]).astype(BF16)

    @pl.when(j == 0)
    def _():
        u_scr[0:8, :] = jnp.zeros((8, width), F32)

    @pl.when(j > 0)
    def _():
        u_scr[0:8, :] = u_scr[tm:tm + 8, :]

    bg = _dot(xb, w_in_ref[:, 0 * width:1 * width])
    cg = _dot(xb, w_in_ref[:, 1 * width:2 * width])
    hv = _dot(xb, w_in_ref[:, 2 * width:3 * width])
    z = _dot(xb, w_in_ref[:, 3 * width:4 * width])
    u = cg * hv
    u_scr[8:tm + 8, :] = u
    y = (cw_ref[2:3, :] * u + cw_ref[1:2, :] * u_scr[7:7 + tm, :]
         + cw_ref[0:1, :] * u_scr[6:6 + tm, :] + cb_ref[...])
    g = (bg * y * _silu(z)).astype(BF16)
    o_ref[0] = x + _dot(g, w_out_ref[...])


def _conv_layer(x, layer, g, w_in_all, cw, cb, w_out_all, *, tm=256):
    b, s, d = x.shape
    width = w_out_all.shape[1]
    return pl.pallas_call(
        functools.partial(_conv_kernel, tm=tm, width=width),
        grid=(b, s // tm),
        in_specs=[
            pl.BlockSpec((1, tm, d), lambda bi, i: (bi, i, 0)),
            _const_spec((1, d)),
            _single((None, d, 4 * width), lambda bi, i: (layer, 0, 0)),
            _const_spec((3, width)),
            _const_spec((1, width)),
            _single((None, width, d), lambda bi, i: (layer, 0, 0)),
        ],
        out_specs=pl.BlockSpec((1, tm, d), lambda bi, i: (bi, i, 0)),
        out_shape=jax.ShapeDtypeStruct((b, s, d), F32),
        scratch_shapes=[
            pltpu.VMEM((tm + 8, width), F32),
            pltpu.VMEM((d, 4 * width), BF16),
            pltpu.VMEM((width, d), BF16),
        ],
        compiler_params=_params(("arbitrary", "arbitrary")),
        name="conv_mixer",
    )(x, g.reshape(1, d), w_in_all, cw, cb.reshape(1, width), w_out_all)


def _dsa_proj_kernel(x_ref, pos_ref, g_ref, w_main_ref, w_qi_ref, w_kiwi_ref, qg_ref, kg_ref,
                     inv_ref, sgn_ref,
                     q_ref, k_ref, v_ref, z_ref, qi_ref, ki_ref, wi_ref,
                     *, width, n_pair, n_qi_tile, scale, wi_scale):
    x = x_ref[0]
    d = x.shape[-1]
    tm = x.shape[0]
    xb = (_rms(x, d) * g_ref[...]).astype(BF16)
    pos = pos_ref[0].astype(F32)
    ang_a = pos * inv_ref[0:1, :]
    ang_b = pos * inv_ref[1:2, :]
    cos_a, sin_a = jnp.cos(ang_a), jnp.sin(ang_a) * sgn_ref[0:1, :]
    cos_b, sin_b = jnp.cos(ang_b), jnp.sin(ang_b) * sgn_ref[1:2, :]
    lane = lax.broadcasted_iota(jnp.int32, (tm, LANES), 1)
    lo = lane < 64
    first_b = (lane & 63) < 16
    head0 = (lane & 32) == 0

    q = _dot(xb, w_main_ref[:, 0 * width:1 * width])
    k = _dot(xb, w_main_ref[:, 1 * width:2 * width])
    v = _dot(xb, w_main_ref[:, 2 * width:3 * width])
    z = _dot(xb, w_main_ref[:, 3 * width:4 * width])
    z_ref[0] = z.astype(BF16)

    def head_norm_rope(t, gain):
        sq = t * t
        s0 = jnp.sum(jnp.where(head0, sq, 0.0), axis=-1, keepdims=True)
        s1 = jnp.sum(jnp.where(head0, 0.0, sq), axis=-1, keepdims=True)
        ms = jnp.where(head0, s0, s1) * (1.0 / 64)
        t = t * lax.rsqrt(ms + EPS) * gain
        return t * cos_a + pltpu.roll(t, 64, 1) * sin_a

    for j in range(n_pair):
        sl = slice(j * LANES, (j + 1) * LANES)
        qt = head_norm_rope(q[:, sl], qg_ref[...]) * scale
        q_ref[0, 2 * j] = jnp.where(head0, qt, 0.0).astype(BF16)
        q_ref[0, 2 * j + 1] = jnp.where(head0, 0.0, qt).astype(BF16)
        k_ref[0, j] = head_norm_rope(k[:, sl], kg_ref[...]).astype(BF16)
        v_ref[0, j] = v[:, sl].astype(BF16)

    qi = _dot(xb, w_qi_ref[...])
    for j in range(n_qi_tile):
        t = _rope_tile(qi[:, j * LANES:(j + 1) * LANES], cos_b, sin_b, first_b, 16)
        qi_ref[0, 2 * j] = jnp.where(lo, t, 0.0).astype(BF16)
        qi_ref[0, 2 * j + 1] = jnp.where(lo, 0.0, t).astype(BF16)

    kiwi = _dot(xb, w_kiwi_ref[...])
    ki_ref[0] = _rope_tile(kiwi[:, 0:LANES], cos_b, sin_b, first_b, 16).astype(BF16)
    wi_ref[0] = kiwi[:, LANES:2 * LANES] * wi_scale


def _key_to_float(key):
    return pltpu.bitcast(jnp.where(key >= 0, key, key ^ 0x7FFFFFFF), F32)


def _search_init(tq):
    rg = tq // SEARCH_ROW_GROUPS
    return tuple(jnp.full((rg, 1), INT_MIN, jnp.int32) for _ in range(SEARCH_ROW_GROUPS))


def _search_step(step, bases, score_scr, nt, topk):
    rg = bases[0].shape[0]
    bit = jnp.left_shift(jnp.int32(1), 31 - step)
    out = []
    for g, base_g in enumerate(bases):
        cand = base_g + bit
        hits = jnp.where(score_scr[0:nt, g * rg:(g + 1) * rg, :] >= _key_to_float(cand), 1.0, 0.0)
        n_ge = jnp.sum(jnp.sum(hits, axis=0), axis=1, keepdims=True)
        out.append(jnp.where(n_ge >= float(topk), cand, base_g))
    return tuple(out)


def _select_bias(bases, score_scr, bias_scr, slot, nt, tq, topk):
    topk_f = float(topk)

    def count(pred_fn):
        hits = jnp.where(pred_fn(score_scr[0:nt]), 1.0, 0.0)
        return jnp.sum(jnp.sum(hits, axis=0), axis=1, keepdims=True)

    thr = _key_to_float(jnp.concatenate(bases, axis=0))
    lowest = jnp.full((tq, 1), -F32_MAX, F32)
    thr = jnp.where(count(lambda s: s >= lowest) < topk_f, lowest, thr)
    excess = jnp.max(count(lambda s: s >= thr)) > topk_f

    def plain(_):
        bias_scr[slot, 0:nt] = jnp.where(score_scr[0:nt] >= thr, 0.0, NEG_BIG)
        return 0

    def index_ordered_ties(_):
        need = topk_f - count(lambda s: s > thr)
        r = lax.broadcasted_iota(jnp.int32, (LANES, LANES), 0)
        c = lax.broadcasted_iota(jnp.int32, (LANES, LANES), 1)
        upper = jnp.where(r < c, 1.0, 0.0).astype(BF16)

        def tile_step(t, carry):
            s = score_scr[t]
            tie_f = jnp.where(s == thr, 1.0, 0.0)
            before = _dot(tie_f.astype(BF16), upper) + carry
            tie_bias = jnp.where(s == thr, jnp.where(before < need, 0.0, NEG_BIG), NEG_BIG)
            bias_scr[slot, t] = jnp.where(s > thr, 0.0, tie_bias)
            return carry + jnp.sum(tie_f, axis=1, keepdims=True)

        lax.fori_loop(0, nt, tile_step, jnp.zeros((tq, 1), F32))
        return 0

    lax.cond(excess, index_ordered_ties, plain, 0)


def _dsa_attn_kernel(bounded_ref, q_ref, qi_ref, wi_ref, k_ref, v_ref, ki_ref, z_ref, x_ref, w_out_ref,
                     o_ref, score_scr, bias_scr, o_scr, wi_scr, *, tq, n_blocks, topk, n_idx_heads, n_pair,
                     unroll_small, unroll_large):
    qb = pl.program_id(1)
    bounded = bounded_ref[0, 0] != 0
    lo = lax.broadcasted_iota(jnp.int32, (tq, LANES), 1) < 64
    tq_tiles = tq // LANES
    n_tiles = score_scr.shape[0]
    steps_per_pair = 32 // n_pair

    def pair_out(j, nvis, slot, is_bounded):
        kt = k_ref[0, j, 0:nvis, :]
        vt = v_ref[0, j, 0:nvis, :]
        qq = q_ref[0, pl.ds(2 * j, 2)].reshape(2 * tq, LANES)
        logit = _dot_nt(qq, kt)
        bias = jnp.concatenate([bias_scr[slot, t] for t in range(nvis // LANES)], axis=1)
        o_even = _softmax_pv(logit[0:tq] + bias, vt, bounded=is_bounded)
        o_odd = _softmax_pv(logit[tq:2 * tq] + bias, vt, bounded=is_bounded)
        return jnp.where(lo, o_even, o_odd)

    def diag_bias_tiles(slot, nt):
        diag_vis = _diag_visible(tq)
        for d in range(tq_tiles):
            bias_scr[slot, nt - tq_tiles + d] = jnp.where(diag_vis[:, d * LANES:(d + 1) * LANES], 0.0, NEG_BIG)

    wi = wi_ref[0]
    for h in range(n_idx_heads):
        wi_scr[h] = jnp.broadcast_to(wi[:, h:h + 1], (tq, LANES))

    def body(i):
        cur, nxt = i % 2, (i + 1) % 2
        nvis = (i + 1) * tq
        if i == 0:
            diag_bias_tiles(cur, nvis // LANES)
        nvis2 = nvis + tq
        nt2 = nvis2 // LANES
        search_next = i + 1 < n_blocks and nvis2 > topk
        if i + 1 < n_blocks and not search_next:
            diag_bias_tiles(nxt, nt2)
        if search_next:
            score_scr[0:nt2] = jnp.zeros((nt2, tq, LANES), F32)

            def idx_head(h, _):
                rel = jnp.maximum(_dot_nt(qi_ref[0, h], ki_ref[0, 0:nvis2, :]), 0.0)
                w = wi_scr[h]
                for t in range(nt2):
                    score_scr[t] += w * rel[:, t * LANES:(t + 1) * LANES]
                return 0

            lax.fori_loop(0, n_idx_heads, idx_head, 0, unroll=4)
            diag_vis = _diag_visible(tq)
            for d in range(tq_tiles):
                t = nt2 - tq_tiles + d
                score_scr[t] = jnp.where(diag_vis[:, d * LANES:(d + 1) * LANES], score_scr[t], -jnp.inf)

        def pair_step(j, bases):
            o_scr[j] = pair_out(j, nvis, cur, True)
            for s_ in range(steps_per_pair if search_next else 0):
                bases = _search_step(j * steps_per_pair + s_, bases, score_scr, nt2, topk)
            return bases

        bases = lax.fori_loop(0, n_pair, pair_step, _search_init(tq) if search_next else (),
                              unroll=unroll_small if 2 * i < n_blocks else unroll_large)
        if search_next:
            _select_bias(bases, score_scr, bias_scr, nxt, nt2, tq, topk)

    _per_block_variants(qb, n_blocks, body)

    @pl.when(jnp.logical_not(bounded))
    def _():
        slot = qb % 2

        def mask_tile(t, _):
            bias_scr[slot, t] = jnp.full((tq, LANES), NEG_BIG, F32)
            return 0

        lax.fori_loop((qb + 1) * tq_tiles, n_tiles, mask_tile, 0)

        def pair_step(j, _):
            o_scr[j] = pair_out(j, n_tiles * LANES, slot, False)
            return 0

        lax.fori_loop(0, n_pair, pair_step, 0)

    o = jnp.concatenate([o_scr[j] for j in range(n_pair)], axis=1)
    g = (o * _silu(z_ref[0].astype(F32))).astype(BF16)
    o_ref[0] = x_ref[0] + _dot(g, w_out_ref[...])


def _rope_inv(half):
    return ROPE_THETA ** (-jnp.arange(half, dtype=F32) / half)


def _dsa_layer(x, pos3, g, w_in, q_norm, k_norm, w_out, *, tm=256, tq=256):
    b, s, d = x.shape
    width = w_out.shape[0]
    head_dim = q_norm.shape[0]
    assert head_dim == 64 and width % LANES == 0
    n_pair = width // LANES
    idx_dim = 64
    n_idx_heads = (w_in.shape[1] - 4 * width - idx_dim) // (idx_dim + 1)
    assert 4 * width + n_idx_heads * idx_dim + idx_dim + n_idx_heads == w_in.shape[1]
    n_qi_tile = n_idx_heads * idx_dim // LANES
    c0 = 4 * width
    c1 = c0 + n_idx_heads * idx_dim
    c2 = c1 + idx_dim
    half = head_dim // 2

    def pair_split(t):
        t4 = t.reshape(t.shape[:-1] + (n_pair, 2, 2, half))
        return jnp.swapaxes(t4, -3, -2).reshape(t.shape)

    w_main = jnp.concatenate([pair_split(w_in[:, :width]), pair_split(w_in[:, width:2 * width]),
                              w_in[:, 2 * width:c0]], axis=1).astype(BF16)
    w_qi = w_in[:, c0:c1].astype(BF16)
    w_ki = w_in[:, c1:c2]
    w_wi = jnp.pad(w_in[:, c2:], ((0, 0), (0, LANES - n_idx_heads)))
    w_kiwi = jnp.concatenate([w_ki, w_ki, w_wi], axis=1).astype(BF16)

    inv32 = _rope_inv(32)
    inv16 = _rope_inv(16)
    zeros32 = jnp.zeros((32,), F32)
    inv_a = jnp.tile(inv32, 4)
    inv_b = jnp.tile(jnp.concatenate([inv16, inv16, zeros32]), 2)
    inv = jnp.stack([inv_a, inv_b])
    sgn_a = jnp.concatenate([-jnp.ones((64,), F32), jnp.ones((64,), F32)])
    sgn_b = jnp.tile(jnp.concatenate([-jnp.ones((16,), F32), jnp.ones((48,), F32)]), 2)
    sgn = jnp.stack([sgn_a, sgn_b])

    def pair_gain(gn):
        return jnp.concatenate([gn[:half], gn[:half], gn[half:], gn[half:]]).reshape(1, LANES)

    qg = pair_gain(q_norm)
    kg = pair_gain(k_norm)

    topk = min(TOPK_MAX, s // 4)
    assert s % tq == 0 and tq % (1 << CHUNK_SHIFT) == 0
    n_heads = 2 * n_pair
    row = lambda bi, i: (bi, i, 0)
    head_row = lambda bi, i: (bi, 0, i, 0)
    q, k, v, z, qi, ki, wi = pl.pallas_call(
        functools.partial(_dsa_proj_kernel, width=width, n_pair=n_pair, n_qi_tile=n_qi_tile,
                          scale=head_dim ** -0.5 * LOG2E,
                          wi_scale=n_idx_heads ** -0.5 * idx_dim ** -0.5),
        grid=(b, s // tm),
        in_specs=[
            pl.BlockSpec((1, tm, d), row),
            pl.BlockSpec((1, tm, 1), row),
            _const_spec((1, d)),
            _const_spec((d, c0)),
            _const_spec((d, c1 - c0)),
            _const_spec((d, 2 * LANES)),
            _const_spec((1, LANES)),
            _const_spec((1, LANES)),
            _const_spec((2, LANES)),
            _const_spec((2, LANES)),
        ],
        out_specs=[
            pl.BlockSpec((1, n_heads, tm, LANES), head_row),
            pl.BlockSpec((1, n_pair, tm, LANES), head_row),
            pl.BlockSpec((1, n_pair, tm, LANES), head_row),
            pl.BlockSpec((1, tm, width), row),
            pl.BlockSpec((1, n_idx_heads, tm, LANES), head_row),
            pl.BlockSpec((1, tm, LANES), row),
            pl.BlockSpec((1, tm, LANES), row),
        ],
        out_shape=[
            jax.ShapeDtypeStruct((b, n_heads, s, LANES), BF16),
            jax.ShapeDtypeStruct((b, n_pair, s, LANES), BF16),
            jax.ShapeDtypeStruct((b, n_pair, s, LANES), BF16),
            jax.ShapeDtypeStruct((b, s, width), BF16),
            jax.ShapeDtypeStruct((b, n_idx_heads, s, LANES), BF16),
            jax.ShapeDtypeStruct((b, s, LANES), BF16),
            jax.ShapeDtypeStruct((b, s, LANES), F32),
        ],
        compiler_params=_params(("arbitrary", "arbitrary")),
        name="dsa_proj",
    )(x, pos3, g.reshape(1, d), w_main, w_qi, w_kiwi, qg, kg, inv, sgn)

    qrow = lambda bi, i: (bi, i, 0)
    qhead = lambda bi, i: (bi, 0, i, 0)
    full = lambda bi, i: (bi, 0, 0, 0)
    n_blocks = s // tq
    assert tq <= topk and 32 % n_pair == 0
    nxt_head = lambda bi, i: (bi, 0, jnp.minimum(i + 1, n_blocks - 1), 0)
    nxt_row = lambda bi, i: (bi, jnp.minimum(i + 1, n_blocks - 1), 0)
    return pl.pallas_call(
        functools.partial(_dsa_attn_kernel, tq=tq, n_blocks=n_blocks, topk=topk,
                          n_idx_heads=n_idx_heads, n_pair=n_pair, unroll_small=2, unroll_large=2),
        grid=(b, n_blocks),
        in_specs=[
            pl.BlockSpec(memory_space=pltpu.SMEM),
            pl.BlockSpec((1, n_heads, tq, LANES), qhead),
            pl.BlockSpec((1, n_idx_heads, tq, LANES), nxt_head),
            pl.BlockSpec((1, tq, LANES), nxt_row),
            pl.BlockSpec((1, n_pair, s, LANES), full),
            pl.BlockSpec((1, n_pair, s, LANES), full),
            pl.BlockSpec((1, s, LANES), lambda bi, i: (bi, 0, 0)),
            pl.BlockSpec((1, tq, width), qrow),
            pl.BlockSpec((1, tq, d), qrow),
            _single((width, d), lambda bi, i: (0, 0)),
        ],
        out_specs=pl.BlockSpec((1, tq, d), qrow),
        out_shape=jax.ShapeDtypeStruct((b, s, d), F32),
        scratch_shapes=[
            pltpu.VMEM((s // LANES, tq, LANES), F32),
            pltpu.VMEM((2, s // LANES, tq, LANES), F32),
            pltpu.VMEM((n_pair, tq, LANES), F32),
            pltpu.VMEM((n_idx_heads, tq, LANES), F32),
        ],
        compiler_params=_params(("arbitrary", "arbitrary")),
        name="dsa_attn",
    )(_logits_bounded(q_norm, k_norm, head_dim, head_dim ** -0.5 * LOG2E),
      q, qi, wi, k, v, ki, z, x, w_out.astype(BF16))


def _mla_proj_kernel(x_ref, pos_ref, g_ref, w_in_ref, qlat_g_ref, kvlat_g_ref, w_uq_ref, w_uq_rot_ref,
                     w_uk_ref, w_uv_ref, gains_ref, inv_ref, sgn_ref,
                     q_ref, k_ref, v_ref, z_ref,
                     *, q_lora, kv_lora, n_heads, qk_dim, scale):
    x = x_ref[0]
    d = x.shape[-1]
    xb = (_rms(x, d) * g_ref[...]).astype(BF16)
    pos = pos_ref[0].astype(F32)
    ang = pos * inv_ref[...]
    cos_t, sin_t = jnp.cos(ang), jnp.sin(ang) * sgn_ref[...]
    cos_q = cos_t * (gains_ref[0:1, :] * scale)
    sin_q = sin_t * (gains_ref[1:2, :] * scale)
    kg = gains_ref[2:3, :]

    c1 = q_lora + kv_lora
    cq = _dot(xb, w_in_ref[:, 0:q_lora])
    ckv = _dot(xb, w_in_ref[:, q_lora:c1])
    kr = _dot(xb, w_in_ref[:, c1:c1 + LANES])
    kr_rot = _dot(xb, w_in_ref[:, c1 + LANES:c1 + 2 * LANES])
    z = _dot(xb, w_in_ref[:, c1 + 2 * LANES:])
    z_ref[0] = z.astype(BF16)

    cq_b = (_rms(cq, q_lora) * qlat_g_ref[...]).astype(BF16)
    q = _dot(cq_b, w_uq_ref[...])
    q_rot = _dot(cq_b, w_uq_rot_ref[...])
    ckv_b = (_rms(ckv, kv_lora) * kvlat_g_ref[...]).astype(BF16)
    kn = _dot(ckv_b, w_uk_ref[...])
    v = _dot(ckv_b, w_uv_ref[...])

    k_rope = kr * (kg * cos_t) + kr_rot * (gains_ref[3:4, :] * sin_t)
    kr_ss = jnp.sum(kr * kr, axis=-1, keepdims=True)
    inv_n = 1.0 / qk_dim
    for h in range(n_heads):
        sl = slice(h * LANES, (h + 1) * LANES)
        qh = q[:, sl]
        q_s = lax.rsqrt(jnp.sum(qh * qh, axis=-1, keepdims=True) * inv_n + EPS)
        q_ref[0, h] = (q_s * (qh * cos_q + q_rot[:, sl] * sin_q)).astype(BF16)
        kh = kn[:, sl]
        k_s = lax.rsqrt((jnp.sum(kh * kh, axis=-1, keepdims=True) + kr_ss) * inv_n + EPS)
        k_ref[0, h] = (k_s * (kh * kg + k_rope)).astype(BF16)
    for j in range(n_heads // 2):
        v_ref[0, j] = v[:, j * LANES:(j + 1) * LANES].astype(BF16)


def _mla_attn_kernel(bounded_ref, q_ref, k_ref, v_ref, z_ref, x_ref, w_out_ref, o_ref, o_scr,
                     *, tq, seq, n_blocks, n_pair):
    qb = pl.program_id(1)
    bounded = bounded_ref[0, 0] != 0
    lo = lax.broadcasted_iota(jnp.int32, (tq, LANES), 1) < 64

    def body(i):
        nvis = (i + 1) * tq
        diag_bias = jnp.where(_diag_visible(tq), 0.0, NEG_BIG)

        def head_out(h, vt):
            logit = _dot_nt(q_ref[0, h], k_ref[0, h, 0:nvis, :])
            last = logit[:, nvis - tq:nvis] + diag_bias
            if nvis > tq:
                logit = jnp.concatenate([logit[:, 0:nvis - tq], last], axis=1)
            else:
                logit = last
            return _softmax_pv(logit, vt, bounded=True)

        def pair_step(j, _):
            vt = v_ref[0, j, 0:nvis, :]
            o_scr[j] = jnp.where(lo, head_out(2 * j, vt), head_out(2 * j + 1, vt))
            return 0

        lax.fori_loop(0, n_pair, pair_step, 0, unroll=4)

    @pl.when(bounded)
    def _():
        _per_block_variants(qb, n_blocks, body)

    @pl.when(jnp.logical_not(bounded))
    def _():
        s_idx = lax.broadcasted_iota(jnp.int32, (tq, seq), 1)
        q_chunk = (qb * tq + lax.broadcasted_iota(jnp.int32, (tq, seq), 0)) >> CHUNK_SHIFT
        bias = jnp.where((s_idx >> CHUNK_SHIFT) <= q_chunk, 0.0, NEG_BIG)

        def pair_step(j, _):
            vt = v_ref[0, j]
            outs = [_softmax_pv(_dot_nt(q_ref[0, 2 * j + e], k_ref[0, 2 * j + e]) + bias, vt, bounded=False)
                    for e in range(2)]
            o_scr[j] = jnp.where(lo, outs[0], outs[1])
            return 0

        lax.fori_loop(0, n_pair, pair_step, 0)

    o = jnp.concatenate([o_scr[j] for j in range(n_pair)], axis=1)
    g = (o * _silu(z_ref[0].astype(F32))).astype(BF16)
    o_ref[0] = x_ref[0] + _dot(g, w_out_ref[...])


def _mla_layer(x, pos3, g, w_in, q_lat_norm, kv_lat_norm, w_uq, w_ukv, q_norm, k_norm, w_out,
               *, tm=256, tq=256):
    b, s, d = x.shape
    width = w_out.shape[0]
    q_lora = q_lat_norm.shape[0]
    kv_lora = kv_lat_norm.shape[0]
    qk_dim = q_norm.shape[0]
    v_dim = 64
    nope = 64
    rope_dim = qk_dim - nope
    n_heads = width // v_dim
    assert nope + rope_dim <= LANES and w_ukv.shape[1] == n_heads * (nope + v_dim)
    assert w_in.shape[1] == q_lora + kv_lora + rope_dim + width
    assert s % tq == 0 and tq % (1 << CHUNK_SHIFT) == 0
    pad = LANES - qk_dim

    rope_half = rope_dim // 2

    def rope_lanes(t):
        return jnp.pad(t, [(0, 0)] * (t.ndim - 1) + [(nope, LANES - nope - rope_dim)])

    def partner(t):
        return jnp.concatenate([t[..., rope_half:], t[..., :rope_half]], axis=-1)

    c1 = q_lora + kv_lora
    w_kr = w_in[:, c1:c1 + rope_dim]
    w_in_p = jnp.concatenate([w_in[:, :c1], rope_lanes(w_kr), rope_lanes(partner(w_kr)),
                              w_in[:, c1 + rope_dim:]], axis=1).astype(BF16)
    w_uq3 = w_uq.reshape(q_lora, n_heads, qk_dim)
    w_uq_p = jnp.pad(w_uq3, ((0, 0), (0, 0), (0, pad))).reshape(q_lora, n_heads * LANES).astype(BF16)
    w_uq_rot = rope_lanes(partner(w_uq3[:, :, nope:])).reshape(q_lora, n_heads * LANES).astype(BF16)
    w_ukv3 = w_ukv.reshape(kv_lora, n_heads, nope + v_dim)
    w_uk_p = jnp.pad(w_ukv3[:, :, :nope], ((0, 0), (0, 0), (0, LANES - nope)))
    w_uk_p = w_uk_p.reshape(kv_lora, n_heads * LANES).astype(BF16)
    w_uv = w_ukv3[:, :, nope:].reshape(kv_lora, n_heads * v_dim).astype(BF16)
    gains = jnp.stack([jnp.pad(q_norm, (0, pad)), rope_lanes(partner(q_norm[nope:])),
                       jnp.pad(k_norm, (0, pad)), rope_lanes(partner(k_norm[nope:]))])

    inv_h = _rope_inv(rope_half)
    inv = jnp.concatenate([jnp.zeros((nope,), F32), inv_h, inv_h,
                           jnp.zeros((LANES - nope - rope_dim,), F32)]).reshape(1, LANES)
    sgn = jnp.concatenate([jnp.ones((nope,), F32), -jnp.ones((rope_half,), F32),
                           jnp.ones((LANES - nope - rope_half,), F32)]).reshape(1, LANES)

    n_pair = n_heads // 2
    row = lambda bi, i: (bi, i, 0)
    head_row = lambda bi, i: (bi, 0, i, 0)
    n_in = w_in_p.shape[1]
    q, k, v, z = pl.pallas_call(
        functools.partial(_mla_proj_kernel, q_lora=q_lora, kv_lora=kv_lora, n_heads=n_heads,
                          qk_dim=qk_dim, scale=qk_dim ** -0.5 * LOG2E),
        grid=(b, s // tm),
        in_specs=[
            pl.BlockSpec((1, tm, d), row),
            pl.BlockSpec((1, tm, 1), row),
            _const_spec((1, d)),
            _const_spec((d, n_in)),
            _const_spec((1, q_lora)),
            _const_spec((1, kv_lora)),
            _const_spec((q_lora, n_heads * LANES)),
            _const_spec((q_lora, n_heads * LANES)),
            _const_spec((kv_lora, n_heads * LANES)),
            _const_spec((kv_lora, n_heads * v_dim)),
            _const_spec((4, LANES)),
            _const_spec((1, LANES)),
            _const_spec((1, LANES)),
        ],
        out_specs=[
            pl.BlockSpec((1, n_heads, tm, LANES), head_row),
            pl.BlockSpec((1, n_heads, tm, LANES), head_row),
            pl.BlockSpec((1, n_pair, tm, LANES), head_row),
            pl.BlockSpec((1, tm, width), row),
        ],
        out_shape=[
            jax.ShapeDtypeStruct((b, n_heads, s, LANES), BF16),
            jax.ShapeDtypeStruct((b, n_heads, s, LANES), BF16),
            jax.ShapeDtypeStruct((b, n_pair, s, LANES), BF16),
            jax.ShapeDtypeStruct((b, s, width), BF16),
        ],
        compiler_params=_params(("arbitrary", "arbitrary")),
        name="mla_proj",
    )(x, pos3, g.reshape(1, d), w_in_p, q_lat_norm.reshape(1, q_lora), kv_lat_norm.reshape(1, kv_lora),
      w_uq_p, w_uq_rot, w_uk_p, w_uv, gains, inv, sgn)

    qrow = lambda bi, i: (bi, i, 0)
    qhead = lambda bi, i: (bi, 0, i, 0)
    full = lambda bi, i: (bi, 0, 0, 0)
    return pl.pallas_call(
        functools.partial(_mla_attn_kernel, tq=tq, seq=s, n_blocks=s // tq, n_pair=n_pair),
        grid=(b, s // tq),
        in_specs=[
            pl.BlockSpec(memory_space=pltpu.SMEM),
            pl.BlockSpec((1, n_heads, tq, LANES), qhead),
            pl.BlockSpec((1, n_heads, s, LANES), full),
            pl.BlockSpec((1, n_pair, s, LANES), full),
            pl.BlockSpec((1, tq, width), qrow),
            pl.BlockSpec((1, tq, d), qrow),
            _single((width, d), lambda bi, i: (0, 0)),
        ],
        out_specs=pl.BlockSpec((1, tq, d), qrow),
        out_shape=jax.ShapeDtypeStruct((b, s, d), F32),
        scratch_shapes=[
            pltpu.VMEM((n_pair, tq, LANES), F32),
        ],
        compiler_params=_params(("arbitrary", "arbitrary")),
        name="mla_attn",
    )(_logits_bounded(q_norm, k_norm, qk_dim, qk_dim ** -0.5 * LOG2E), q, k, v, z, x, w_out.astype(BF16))


def kernel(x, positions, a_norm, a_w_in, a_conv_w, a_conv_b, a_w_out, b_norm, b_w_in, b_q_norm, b_k_norm, b_w_out, c_norm, c_w_in, c_q_lat_norm, c_kv_lat_norm, c_w_uq, c_w_ukv, c_q_norm, c_k_norm, c_w_out):
    depth = a_norm.shape[0] + b_norm.shape[0] + c_norm.shape[0]
    pos3 = positions.reshape(positions.shape + (1,))
    for i in range(depth):
        kind, j = i % 3, i // 3
        if kind == 0:
            x = _conv_layer(x, j, a_norm[j], a_w_in, a_conv_w[j], a_conv_b[j], a_w_out)
        elif kind == 1:
            x = _dsa_layer(x, pos3, b_norm[j], b_w_in[j], b_q_norm[j], b_k_norm[j], b_w_out[j])
        else:
            x = _mla_layer(x, pos3, c_norm[j], c_w_in[j], c_q_lat_norm[j], c_kv_lat_norm[j],
                           c_w_uq[j], c_w_ukv[j], c_q_norm[j], c_k_norm[j], c_w_out[j])
    return x
```

```python
import functools
import math

import jax
import jax.numpy as jnp
from jax import lax
from jax.experimental import pallas as pl
from jax.experimental.pallas import tpu as pltpu

EPS = 1e-6
ROPE_THETA = 10000.0
CHUNK_SHIFT = 6
TOPK_MAX = 256
LANES = 128
NEG_BIG = -1e30
INT_MIN = -(2 ** 31)
F32_MAX = 3.4028234663852886e38
LOG2E = math.log2(math.e)
SEARCH_ROW_GROUPS = 4
LOGIT_BOUND = 64.0
VMEM_LIMIT = 56 * 1024 * 1024

BF16 = jnp.bfloat16
F32 = jnp.float32


def _dot(a, b):
    return jnp.dot(a, b, preferred_element_type=F32)


def _dot_nt(a, b):
    return lax.dot_general(a, b, (((1,), (1,)), ((), ())), preferred_element_type=F32)


def _rms(x, n):
    return x * lax.rsqrt(jnp.sum(x * x, axis=-1, keepdims=True) * (1.0 / n) + EPS)


def _silu(z):
    return z * jax.nn.sigmoid(z)


def _rope_tile(t, cos_t, sin_t, first_mask, half):
    partner = jnp.where(first_mask, pltpu.roll(t, LANES - half, 1), pltpu.roll(t, half, 1))
    return t * cos_t + partner * sin_t


def _params(sem):
    return pltpu.CompilerParams(dimension_semantics=sem, vmem_limit_bytes=VMEM_LIMIT)


def _single(shape, index_map):
    return pl.BlockSpec(shape, index_map, pipeline_mode=pl.Buffered(1))


def _const_spec(shape):
    nd = len(shape)
    return _single(shape, lambda *_: (0,) * nd)


def _diag_visible(tq):
    r = lax.broadcasted_iota(jnp.int32, (tq, tq), 0)
    c = lax.broadcasted_iota(jnp.int32, (tq, tq), 1)
    return (c >> CHUNK_SHIFT) <= (r >> CHUNK_SHIFT)


def _softmax_pv(logit, v, *, bounded, keep=None):
    if not bounded:
        if keep is not None:
            logit = jnp.where(keep.astype(F32) > 0.0, logit, NEG_BIG)
        logit = logit - jnp.max(logit, axis=-1, keepdims=True)
    p = jnp.exp2(logit).astype(BF16)
    if bounded and keep is not None:
        p = p * keep
    pv = _dot(p, jnp.concatenate([v, jnp.ones_like(v)], axis=1))
    return pv[:, 0:LANES] / pv[:, LANES:2 * LANES]


def _logits_bounded(q_gain, k_gain, n, scale):
    bound = 1.02 * n * scale * jnp.max(jnp.abs(q_gain)) * jnp.max(jnp.abs(k_gain))
    return (bound <= LOGIT_BOUND).astype(jnp.int32).reshape(1, 1)


def _per_block_variants(qb, n_blocks, body):
    for i in range(n_blocks):
        pl.when(qb == i)(functools.partial(body, i))


def _conv_kernel(x_ref, g_ref, w_in_f32_ref, cw_ref, cb_ref, w_out_f32_ref, o_ref, u_scr, w_in_ref, w_out_ref,
                 *, tm, width):
    j = pl.program_id(1)

    @pl.when(jnp.logical_and(pl.program_id(0) == 0, j == 0))
    def _():
        w_in_ref[...] = w_in_f32_ref[...].astype(BF16)
        w_out_ref[...] = w_out_f32_ref[...].astype(BF16)

    x = x_ref[0]
    d = x.shape[-1]
    xb = (_rms(x, d) * g_ref[...]).astype(BF16)

    @pl.when(j == 0)
    def _():
        u_scr[0:8, :] = jnp.zeros((8, width), F32)

    @pl.when(j > 0)
    def _():
        u_scr[0:8, :] = u_scr[tm:tm + 8, :]

    bg = _dot(xb, w_in_ref[:, 0 * width:1 * width])
    cg = _dot(xb, w_in_ref[:, 1 * width:2 * width])
    hv = _dot(xb, w_in_ref[:, 2 * width:3 * width])
    z = _dot(xb, w_in_ref[:, 3 * width:4 * width])
    u = cg * hv
    u_scr[8:tm + 8, :] = u
    y = (cw_ref[2:3, :] * u + cw_ref[1:2, :] * u_scr[7:7 + tm, :]
         + cw_ref[0:1, :] * u_scr[6:6 + tm, :] + cb_ref[...])
    g = (bg * y * _silu(z)).astype(BF16)
    o_ref[0] = x + _dot(g, w_out_ref[...])


def _conv_layer(x, layer, g, w_in_all, cw, cb, w_out_all, *, tm=256):
    b, s, d = x.shape
    width = w_out_all.shape[1]
    return pl.pallas_call(
        functools.partial(_conv_kernel, tm=tm, width=width),
        grid=(b, s // tm),
        in_specs=[
            pl.BlockSpec((1, tm, d), lambda bi, i: (bi, i, 0)),
            _const_spec((1, d)),
            _single((None, d, 4 * width), lambda bi, i: (layer, 0, 0)),
            _const_spec((3, width)),
            _const_spec((1, width)),
            _single((None, width, d), lambda bi, i: (layer, 0, 0)),
        ],
        out_specs=pl.BlockSpec((1, tm, d), lambda bi, i: (bi, i, 0)),
        out_shape=jax.ShapeDtypeStruct((b, s, d), F32),
        scratch_shapes=[
            pltpu.VMEM((tm + 8, width), F32),
            pltpu.VMEM((d, 4 * width), BF16),
            pltpu.VMEM((width, d), BF16),
        ],
        compiler_params=_params(("arbitrary", "arbitrary")),
        name="conv_mixer",
    )(x, g.reshape(1, d), w_in_all, cw, cb.reshape(1, width), w_out_all)


def _dsa_proj_kernel(x_ref, pos_ref, g_ref, w_main_ref, w_qi_ref, w_kiwi_ref, qg_ref, kg_ref,
                     inv_ref, sgn_ref,
                     q_ref, k_ref, v_ref, z_ref, qi_ref, ki_ref, wi_ref,
                     *, width, n_pair, n_qi_tile, scale, wi_scale):
    x = x_ref[0]
    d = x.shape[-1]
    tm = x.shape[0]
    xb = (_rms(x, d) * g_ref[...]).astype(BF16)
    pos = pos_ref[0].astype(F32)
    ang_a = pos * inv_ref[0:1, :]
    ang_b = pos * inv_ref[1:2, :]
    cos_a, sin_a = jnp.cos(ang_a), jnp.sin(ang_a) * sgn_ref[0:1, :]
    cos_b, sin_b = jnp.cos(ang_b), jnp.sin(ang_b) * sgn_ref[1:2, :]
    lane = lax.broadcasted_iota(jnp.int32, (tm, LANES), 1)
    lo = lane < 64
    first_b = (lane & 63) < 16
    head0 = (lane & 32) == 0

    q = _dot(xb, w_main_ref[:, 0 * width:1 * width])
    k = _dot(xb, w_main_ref[:, 1 * width:2 * width])
    v = _dot(xb, w_main_ref[:, 2 * width:3 * width])
    z = _dot(xb, w_main_ref[:, 3 * width:4 * width])
    z_ref[0] = z.astype(BF16)

    def head_norm_rope(t, gain):
        sq = t * t
        s0 = jnp.sum(jnp.where(head0, sq, 0.0), axis=-1, keepdims=True)
        s1 = jnp.sum(jnp.where(head0, 0.0, sq), axis=-1, keepdims=True)
        ms = jnp.where(head0, s0, s1) * (1.0 / 64)
        t = t * lax.rsqrt(ms + EPS) * gain
        return t * cos_a + pltpu.roll(t, 64, 1) * sin_a

    for j in range(n_pair):
        sl = slice(j * LANES, (j + 1) * LANES)
        qt = head_norm_rope(q[:, sl], qg_ref[...]) * scale
        q_ref[0, 2 * j] = jnp.where(head0, qt, 0.0).astype(BF16)
        q_ref[0, 2 * j + 1] = jnp.where(head0, 0.0, qt).astype(BF16)
        k_ref[0, j] = head_norm_rope(k[:, sl], kg_ref[...]).astype(BF16)
        v_ref[0, j] = v[:, sl].astype(BF16)

    qi = _dot(xb, w_qi_ref[...])
    for j in range(n_qi_tile):
        t = _rope_tile(qi[:, j * LANES:(j + 1) * LANES], cos_b, sin_b, first_b, 16)
        qi_ref[0, 2 * j] = jnp.where(lo, t, 0.0).astype(BF16)
        qi_ref[0, 2 * j + 1] = jnp.where(lo, 0.0, t).astype(BF16)

    kiwi = _dot(xb, w_kiwi_ref[...])
    ki_ref[0] = _rope_tile(kiwi[:, 0:LANES], cos_b, sin_b, first_b, 16).astype(BF16)
    wi_ref[0] = kiwi[:, LANES:2 * LANES] * wi_scale


def _key_to_float(key):
    return pltpu.bitcast(jnp.where(key >= 0, key, key ^ 0x7FFFFFFF), F32)


def _search_init(tq):
    rg = tq // SEARCH_ROW_GROUPS
    return tuple(jnp.full((rg, 1), INT_MIN, jnp.int32) for _ in range(SEARCH_ROW_GROUPS))


def _search_step(step, bases, score_scr, nt, topk):
    rg = bases[0].shape[0]
    bit = jnp.left_shift(jnp.int32(1), 31 - step)
    out = []
    for g, base_g in enumerate(bases):
        cand = base_g + bit
        hits = jnp.where(score_scr[0:nt, g * rg:(g + 1) * rg, :] >= _key_to_float(cand), 1.0, 0.0)
        n_ge = jnp.sum(jnp.sum(hits, axis=0), axis=1, keepdims=True)
        out.append(jnp.where(n_ge >= float(topk), cand, base_g))
    return tuple(out)


def _select_bias(bases, score_scr, bias_scr, slot, nt, tq, topk):
    topk_f = float(topk)

    def count(pred_fn):
        hits = jnp.where(pred_fn(score_scr[0:nt]), 1.0, 0.0)
        return jnp.sum(jnp.sum(hits, axis=0), axis=1, keepdims=True)

    thr = _key_to_float(jnp.concatenate(bases, axis=0))
    lowest = jnp.full((tq, 1), -F32_MAX, F32)
    thr = jnp.where(count(lambda s: s >= lowest) < topk_f, lowest, thr)
    excess = jnp.max(count(lambda s: s >= thr)) > topk_f

    def plain(_):
        bias_scr[slot, 0:nt] = jnp.where(score_scr[0:nt] >= thr, 1.0, 0.0).astype(BF16)
        return 0

    def index_ordered_ties(_):
        need = topk_f - count(lambda s: s > thr)
        r = lax.broadcasted_iota(jnp.int32, (LANES, LANES), 0)
        c = lax.broadcasted_iota(jnp.int32, (LANES, LANES), 1)
        upper = jnp.where(r < c, 1.0, 0.0).astype(BF16)

        def tile_step(t, carry):
            s = score_scr[t]
            tie_f = jnp.where(s == thr, 1.0, 0.0)
            before = _dot(tie_f.astype(BF16), upper) + carry
            tie_keep = jnp.where(s == thr, jnp.where(before < need, 1.0, 0.0), 0.0)
            bias_scr[slot, t] = jnp.where(s > thr, 1.0, tie_keep).astype(BF16)
            return carry + jnp.sum(tie_f, axis=1, keepdims=True)

        lax.fori_loop(0, nt, tile_step, jnp.zeros((tq, 1), F32))
        return 0

    lax.cond(excess, index_ordered_ties, plain, 0)


def _dsa_attn_kernel(bounded_ref, q_ref, qi_ref, wi_ref, k_ref, v_ref, ki_ref, z_ref, x_ref, w_out_ref,
                     o_ref, score_scr, bias_scr, o_scr, wi_scr, *, tq, n_blocks, topk, n_idx_heads, n_pair,
                     unroll_small, unroll_large):
    qb = pl.program_id(1)
    bounded = bounded_ref[0, 0] != 0
    lo = lax.broadcasted_iota(jnp.int32, (tq, LANES), 1) < 64
    tq_tiles = tq // LANES
    n_tiles = score_scr.shape[0]
    steps_per_pair = 32 // n_pair

    def pair_out(j, nvis, slot, is_bounded):
        kt = k_ref[0, j, 0:nvis, :]
        vt = v_ref[0, j, 0:nvis, :]
        qq = q_ref[0, pl.ds(2 * j, 2)].reshape(2 * tq, LANES)
        logit = _dot_nt(qq, kt)
        keep = jnp.concatenate([bias_scr[slot, t] for t in range(nvis // LANES)], axis=1)
        o_even = _softmax_pv(logit[0:tq], vt, bounded=is_bounded, keep=keep)
        o_odd = _softmax_pv(logit[tq:2 * tq], vt, bounded=is_bounded, keep=keep)
        return jnp.where(lo, o_even, o_odd)

    def diag_bias_tiles(slot, nt):
        diag_vis = _diag_visible(tq)
        for d in range(tq_tiles):
            bias_scr[slot, nt - tq_tiles + d] = jnp.where(diag_vis[:, d * LANES:(d + 1) * LANES],
                                                          1.0, 0.0).astype(BF16)

    wi = wi_ref[0]
    for h in range(n_idx_heads):
        wi_scr[h] = jnp.broadcast_to(wi[:, h:h + 1], (tq, LANES))

    def body(i):
        cur, nxt = i % 2, (i + 1) % 2
        nvis = (i + 1) * tq
        if i == 0:
            diag_bias_tiles(cur, nvis // LANES)
        nvis2 = nvis + tq
        nt2 = nvis2 // LANES
        search_next = i + 1 < n_blocks and nvis2 > topk
        if i + 1 < n_blocks and not search_next:
            diag_bias_tiles(nxt, nt2)
        if search_next:
            score_scr[0:nt2] = jnp.zeros((nt2, tq, LANES), F32)

            def idx_head(h, _):
                rel = jnp.maximum(_dot_nt(qi_ref[0, h], ki_ref[0, 0:nvis2, :]), 0.0)
                w = wi_scr[h]
                for t in range(nt2):
                    score_scr[t] += w * rel[:, t * LANES:(t + 1) * LANES]
                return 0

            lax.fori_loop(0, n_idx_heads, idx_head, 0, unroll=4)
            diag_vis = _diag_visible(tq)
            for d in range(tq_tiles):
                t = nt2 - tq_tiles + d
                score_scr[t] = jnp.where(diag_vis[:, d * LANES:(d + 1) * LANES], score_scr[t], -jnp.inf)

        def pair_step(j, bases):
            o_scr[j] = pair_out(j, nvis, cur, True)
            for s_ in range(steps_per_pair if search_next else 0):
                bases = _search_step(j * steps_per_pair + s_, bases, score_scr, nt2, topk)
            return bases

        bases = lax.fori_loop(0, n_pair, pair_step, _search_init(tq) if search_next else (),
                              unroll=unroll_small if 2 * i < n_blocks else unroll_large)
        if search_next:
            _select_bias(bases, score_scr, bias_scr, nxt, nt2, tq, topk)

    _per_block_variants(qb, n_blocks, body)

    @pl.when(jnp.logical_not(bounded))
    def _():
        slot = qb % 2

        def mask_tile(t, _):
            bias_scr[slot, t] = jnp.zeros((tq, LANES), BF16)
            return 0

        lax.fori_loop((qb + 1) * tq_tiles, n_tiles, mask_tile, 0)

        def pair_step(j, _):
            o_scr[j] = pair_out(j, n_tiles * LANES, slot, False)
            return 0

        lax.fori_loop(0, n_pair, pair_step, 0)

    o = jnp.concatenate([o_scr[j] for j in range(n_pair)], axis=1)
    g = (o * _silu(z_ref[0].astype(F32))).astype(BF16)
    o_ref[0] = x_ref[0] + _dot(g, w_out_ref[...])


def _rope_inv(half):
    return ROPE_THETA ** (-jnp.arange(half, dtype=F32) / half)


def _dsa_layer(x, pos3, g, w_in, q_norm, k_norm, w_out, *, tm=256, tq=256):
    b, s, d = x.shape
    width = w_out.shape[0]
    head_dim = q_norm.shape[0]
    assert head_dim == 64 and width % LANES == 0
    n_pair = width // LANES
    idx_dim = 64
    n_idx_heads = (w_in.shape[1] - 4 * width - idx_dim) // (idx_dim + 1)
    assert 4 * width + n_idx_heads * idx_dim + idx_dim + n_idx_heads == w_in.shape[1]
    n_qi_tile = n_idx_heads * idx_dim // LANES
    c0 = 4 * width
    c1 = c0 + n_idx_heads * idx_dim
    c2 = c1 + idx_dim
    half = head_dim // 2

    def pair_split(t):
        t4 = t.reshape(t.shape[:-1] + (n_pair, 2, 2, half))
        return jnp.swapaxes(t4, -3, -2).reshape(t.shape)

    w_main = jnp.concatenate([pair_split(w_in[:, :width]), pair_split(w_in[:, width:2 * width]),
                              w_in[:, 2 * width:c0]], axis=1).astype(BF16)
    w_qi = w_in[:, c0:c1].astype(BF16)
    w_ki = w_in[:, c1:c2]
    w_wi = jnp.pad(w_in[:, c2:], ((0, 0), (0, LANES - n_idx_heads)))
    w_kiwi = jnp.concatenate([w_ki, w_ki, w_wi], axis=1).astype(BF16)

    inv32 = _rope_inv(32)
    inv16 = _rope_inv(16)
    zeros32 = jnp.zeros((32,), F32)
    inv_a = jnp.tile(inv32, 4)
    inv_b = jnp.tile(jnp.concatenate([inv16, inv16, zeros32]), 2)
    inv = jnp.stack([inv_a, inv_b])
    sgn_a = jnp.concatenate([-jnp.ones((64,), F32), jnp.ones((64,), F32)])
    sgn_b = jnp.tile(jnp.concatenate([-jnp.ones((16,), F32), jnp.ones((48,), F32)]), 2)
    sgn = jnp.stack([sgn_a, sgn_b])

    def pair_gain(gn):
        return jnp.concatenate([gn[:half], gn[:half], gn[half:], gn[half:]]).reshape(1, LANES)

    qg = pair_gain(q_norm)
    kg = pair_gain(k_norm)

    topk = min(TOPK_MAX, s // 4)
    assert s % tq == 0 and tq % (1 << CHUNK_SHIFT) == 0
    n_heads = 2 * n_pair
    row = lambda bi, i: (bi, i, 0)
    head_row = lambda bi, i: (bi, 0, i, 0)
    q, k, v, z, qi, ki, wi = pl.pallas_call(
        functools.partial(_dsa_proj_kernel, width=width, n_pair=n_pair, n_qi_tile=n_qi_tile,
                          scale=head_dim ** -0.5 * LOG2E,
                          wi_scale=n_idx_heads ** -0.5 * idx_dim ** -0.5),
        grid=(b, s // tm),
        in_specs=[
            pl.BlockSpec((1, tm, d), row),
            pl.BlockSpec((1, tm, 1), row),
            _const_spec((1, d)),
            _const_spec((d, c0)),
            _const_spec((d, c1 - c0)),
            _const_spec((d, 2 * LANES)),
            _const_spec((1, LANES)),
            _const_spec((1, LANES)),
            _const_spec((2, LANES)),
            _const_spec((2, LANES)),
        ],
        out_specs=[
            pl.BlockSpec((1, n_heads, tm, LANES), head_row),
            pl.BlockSpec((1, n_pair, tm, LANES), head_row),
            pl.BlockSpec((1, n_pair, tm, LANES), head_row),
            pl.BlockSpec((1, tm, width), row),
            pl.BlockSpec((1, n_idx_heads, tm, LANES), head_row),
            pl.BlockSpec((1, tm, LANES), row),
            pl.BlockSpec((1, tm, LANES), row),
        ],
        out_shape=[
            jax.ShapeDtypeStruct((b, n_heads, s, LANES), BF16),
            jax.ShapeDtypeStruct((b, n_pair, s, LANES), BF16),
            jax.ShapeDtypeStruct((b, n_pair, s, LANES), BF16),
            jax.ShapeDtypeStruct((b, s, width), BF16),
            jax.ShapeDtypeStruct((b, n_idx_heads, s, LANES), BF16),
            jax.ShapeDtypeStruct((b, s, LANES), BF16),
            jax.ShapeDtypeStruct((b, s, LANES), F32),
        ],
        compiler_params=_params(("arbitrary", "arbitrary")),
        name="dsa_proj",
    )(x, pos3, g.reshape(1, d), w_main, w_qi, w_kiwi, qg, kg, inv, sgn)

    qrow = lambda bi, i: (bi, i, 0)
    qhead = lambda bi, i: (bi, 0, i, 0)
    full = lambda bi, i: (bi, 0, 0, 0)
    n_blocks = s // tq
    assert tq <= topk and 32 % n_pair == 0
    nxt_head = lambda bi, i: (bi, 0, jnp.minimum(i + 1, n_blocks - 1), 0)
    nxt_row = lambda bi, i: (bi, jnp.minimum(i + 1, n_blocks - 1), 0)
    return pl.pallas_call(
        functools.partial(_dsa_attn_kernel, tq=tq, n_blocks=n_blocks, topk=topk,
                          n_idx_heads=n_idx_heads, n_pair=n_pair, unroll_small=2, unroll_large=2),
        grid=(b, n_blocks),
        in_specs=[
            pl.BlockSpec(memory_space=pltpu.SMEM),
            pl.BlockSpec((1, n_heads, tq, LANES), qhead),
            pl.BlockSpec((1, n_idx_heads, tq, LANES), nxt_head),
            pl.BlockSpec((1, tq, LANES), nxt_row),
            pl.BlockSpec((1, n_pair, s, LANES), full),
            pl.BlockSpec((1, n_pair, s, LANES), full),
            pl.BlockSpec((1, s, LANES), lambda bi, i: (bi, 0, 0)),
            pl.BlockSpec((1, tq, width), qrow),
            pl.BlockSpec((1, tq, d), qrow),
            _single((width, d), lambda bi, i: (0, 0)),
        ],
        out_specs=pl.BlockSpec((1, tq, d), qrow),
        out_shape=jax.ShapeDtypeStruct((b, s, d), F32),
        scratch_shapes=[
            pltpu.VMEM((s // LANES, tq, LANES), F32),
            pltpu.VMEM((2, s // LANES, tq, LANES), BF16),
            pltpu.VMEM((n_pair, tq, LANES), F32),
            pltpu.VMEM((n_idx_heads, tq, LANES), F32),
        ],
        compiler_params=_params(("arbitrary", "arbitrary")),
        name="dsa_attn",
    )(_logits_bounded(q_norm, k_norm, head_dim, head_dim ** -0.5 * LOG2E),
      q, qi, wi, k, v, ki, z, x, w_out.astype(BF16))


def _mla_proj_kernel(x_ref, pos_ref, g_ref, w_in_ref, qlat_g_ref, kvlat_g_ref, w_uq_ref, w_uq_rot_ref,
                     w_uk_ref, w_uv_ref, gains_ref, inv_ref, sgn_ref,
                     q_ref, k_ref, v_ref, z_ref,
                     *, q_lora, kv_lora, n_heads, qk_dim, scale):
    x = x_ref[0]
    d = x.shape[-1]
    xb = (_rms(x, d) * g_ref[...]).astype(BF16)
    pos = pos_ref[0].astype(F32)
    ang = pos * inv_ref[...]
    cos_t, sin_t = jnp.cos(ang), jnp.sin(ang) * sgn_ref[...]
    cos_q = cos_t * (gains_ref[0:1, :] * scale)
    sin_q = sin_t * (gains_ref[1:2, :] * scale)
    kg = gains_ref[2:3, :]

    c1 = q_lora + kv_lora
    cq = _dot(xb, w_in_ref[:, 0:q_lora])
    ckv = _dot(xb, w_in_ref[:, q_lora:c1])
    kr = _dot(xb, w_in_ref[:, c1:c1 + LANES])
    kr_rot = _dot(xb, w_in_ref[:, c1 + LANES:c1 + 2 * LANES])
    z = _dot(xb, w_in_ref[:, c1 + 2 * LANES:])
    z_ref[0] = z.astype(BF16)

    cq_b = (_rms(cq, q_lora) * qlat_g_ref[...]).astype(BF16)
    q = _dot(cq_b, w_uq_ref[...])
    q_rot = _dot(cq_b, w_uq_rot_ref[...])
    ckv_b = (_rms(ckv, kv_lora) * kvlat_g_ref[...]).astype(BF16)
    kn = _dot(ckv_b, w_uk_ref[...])
    v = _dot(ckv_b, w_uv_ref[...])

    k_rope = kr * (kg * cos_t) + kr_rot * (gains_ref[3:4, :] * sin_t)
    kr_ss = jnp.sum(kr * kr, axis=-1, keepdims=True)
    inv_n = 1.0 / qk_dim
    for h in range(n_heads):
        sl = slice(h * LANES, (h + 1) * LANES)
        qh = q[:, sl]
        q_s = lax.rsqrt(jnp.sum(qh * qh, axis=-1, keepdims=True) * inv_n + EPS)
        q_ref[0, h] = (q_s * (qh * cos_q + q_rot[:, sl] * sin_q)).astype(BF16)
        kh = kn[:, sl]
        k_s = lax.rsqrt((jnp.sum(kh * kh, axis=-1, keepdims=True) + kr_ss) * inv_n + EPS)
        k_ref[0, h] = (k_s * (kh * kg + k_rope)).astype(BF16)
    for j in range(n_heads // 2):
        v_ref[0, j] = v[:, j * LANES:(j + 1) * LANES].astype(BF16)


def _mla_attn_kernel(bounded_ref, q_ref, k_ref, v_ref, z_ref, x_ref, w_out_ref, o_ref, o_scr,
                     *, tq, seq, n_blocks, n_pair):
    qb = pl.program_id(1)
    bounded = bounded_ref[0, 0] != 0
    lo = lax.broadcasted_iota(jnp.int32, (tq, LANES), 1) < 64

    def body(i):
        nvis = (i + 1) * tq
        diag_bias = jnp.where(_diag_visible(tq), 0.0, NEG_BIG)

        def head_out(h, vt):
            logit = _dot_nt(q_ref[0, h], k_ref[0, h, 0:nvis, :])
            last = logit[:, nvis - tq:nvis] + diag_bias
            if nvis > tq:
                logit = jnp.concatenate([logit[:, 0:nvis - tq], last], axis=1)
            else:
                logit = last
            return _softmax_pv(logit, vt, bounded=True)

        def pair_step(j, _):
            vt = v_ref[0, j, 0:nvis, :]
            o_scr[j] = jnp.where(lo, head_out(2 * j, vt), head_out(2 * j + 1, vt))
            return 0

        lax.fori_loop(0, n_pair, pair_step, 0, unroll=4)

    @pl.when(bounded)
    def _():
        _per_block_variants(qb, n_blocks, body)

    @pl.when(jnp.logical_not(bounded))
    def _():
        s_idx = lax.broadcasted_iota(jnp.int32, (tq, seq), 1)
        q_chunk = (qb * tq + lax.broadcasted_iota(jnp.int32, (tq, seq), 0)) >> CHUNK_SHIFT
        bias = jnp.where((s_idx >> CHUNK_SHIFT) <= q_chunk, 0.0, NEG_BIG)

        def pair_step(j, _):
            vt = v_ref[0, j]
            outs = [_softmax_pv(_dot_nt(q_ref[0, 2 * j + e], k_ref[0, 2 * j + e]) + bias, vt, bounded=False)
                    for e in range(2)]
            o_scr[j] = jnp.where(lo, outs[0], outs[1])
            return 0

        lax.fori_loop(0, n_pair, pair_step, 0)

    o = jnp.concatenate([o_scr[j] for j in range(n_pair)], axis=1)
    g = (o * _silu(z_ref[0].astype(F32))).astype(BF16)
    o_ref[0] = x_ref[0] + _dot(g, w_out_ref[...])


def _mla_layer(x, pos3, g, w_in, q_lat_norm, kv_lat_norm, w_uq, w_ukv, q_norm, k_norm, w_out,
               *, tm=256, tq=256):
    b, s, d = x.shape
    width = w_out.shape[0]
    q_lora = q_lat_norm.shape[0]
    kv_lora = kv_lat_norm.shape[0]
    qk_dim = q_norm.shape[0]
    v_dim = 64
    nope = 64
    rope_dim = qk_dim - nope
    n_heads = width // v_dim
    assert nope + rope_dim <= LANES and w_ukv.shape[1] == n_heads * (nope + v_dim)
    assert w_in.shape[1] == q_lora + kv_lora + rope_dim + width
    assert s % tq == 0 and tq % (1 << CHUNK_SHIFT) == 0
    pad = LANES - qk_dim

    rope_half = rope_dim // 2

    def rope_lanes(t):
        return jnp.pad(t, [(0, 0)] * (t.ndim - 1) + [(nope, LANES - nope - rope_dim)])

    def partner(t):
        return jnp.concatenate([t[..., rope_half:], t[..., :rope_half]], axis=-1)

    c1 = q_lora + kv_lora
    w_kr = w_in[:, c1:c1 + rope_dim]
    w_in_p = jnp.concatenate([w_in[:, :c1], rope_lanes(w_kr), rope_lanes(partner(w_kr)),
                              w_in[:, c1 + rope_dim:]], axis=1).astype(BF16)
    w_uq3 = w_uq.reshape(q_lora, n_heads, qk_dim)
    w_uq_p = jnp.pad(w_uq3, ((0, 0), (0, 0), (0, pad))).reshape(q_lora, n_heads * LANES).astype(BF16)
    w_uq_rot = rope_lanes(partner(w_uq3[:, :, nope:])).reshape(q_lora, n_heads * LANES).astype(BF16)
    w_ukv3 = w_ukv.reshape(kv_lora, n_heads, nope + v_dim)
    w_uk_p = jnp.pad(w_ukv3[:, :, :nope], ((0, 0), (0, 0), (0, LANES - nope)))
    w_uk_p = w_uk_p.reshape(kv_lora, n_heads * LANES).astype(BF16)
    w_uv = w_ukv3[:, :, nope:].reshape(kv_lora, n_heads * v_dim).astype(BF16)
    gains = jnp.stack([jnp.pad(q_norm, (0, pad)), rope_lanes(partner(q_norm[nope:])),
                       jnp.pad(k_norm, (0, pad)), rope_lanes(partner(k_norm[nope:]))])

    inv_h = _rope_inv(rope_half)
    inv = jnp.concatenate([jnp.zeros((nope,), F32), inv_h, inv_h,
                           jnp.zeros((LANES - nope - rope_dim,), F32)]).reshape(1, LANES)
    sgn = jnp.concatenate([jnp.ones((nope,), F32), -jnp.ones((rope_half,), F32),
                           jnp.ones((LANES - nope - rope_half,), F32)]).reshape(1, LANES)

    n_pair = n_heads // 2
    row = lambda bi, i: (bi, i, 0)
    head_row = lambda bi, i: (bi, 0, i, 0)
    n_in = w_in_p.shape[1]
    q, k, v, z = pl.pallas_call(
        functools.partial(_mla_proj_kernel, q_lora=q_lora, kv_lora=kv_lora, n_heads=n_heads,
                          qk_dim=qk_dim, scale=qk_dim ** -0.5 * LOG2E),
        grid=(b, s // tm),
        in_specs=[
            pl.BlockSpec((1, tm, d), row),
            pl.BlockSpec((1, tm, 1), row),
            _const_spec((1, d)),
            _const_spec((d, n_in)),
            _const_spec((1, q_lora)),
            _const_spec((1, kv_lora)),
            _const_spec((q_lora, n_heads * LANES)),
            _const_spec((q_lora, n_heads * LANES)),
            _const_spec((kv_lora, n_heads * LANES)),
            _const_spec((kv_lora, n_heads * v_dim)),
            _const_spec((4, LANES)),
            _const_spec((1, LANES)),
            _const_spec((1, LANES)),
        ],
        out_specs=[
            pl.BlockSpec((1, n_heads, tm, LANES), head_row),
            pl.BlockSpec((1, n_heads, tm, LANES), head_row),
            pl.BlockSpec((1, n_pair, tm, LANES), head_row),
            pl.BlockSpec((1, tm, width), row),
        ],
        out_shape=[
            jax.ShapeDtypeStruct((b, n_heads, s, LANES), BF16),
            jax.ShapeDtypeStruct((b, n_heads, s, LANES), BF16),
            jax.ShapeDtypeStruct((b, n_pair, s, LANES), BF16),
            jax.ShapeDtypeStruct((b, s, width), BF16),
        ],
        compiler_params=_params(("arbitrary", "arbitrary")),
        name="mla_proj",
    )(x, pos3, g.reshape(1, d), w_in_p, q_lat_norm.reshape(1, q_lora), kv_lat_norm.reshape(1, kv_lora),
      w_uq_p, w_uq_rot, w_uk_p, w_uv, gains, inv, sgn)

    qrow = lambda bi, i: (bi, i, 0)
    qhead = lambda bi, i: (bi, 0, i, 0)
    full = lambda bi, i: (bi, 0, 0, 0)
    return pl.pallas_call(
        functools.partial(_mla_attn_kernel, tq=tq, seq=s, n_blocks=s // tq, n_pair=n_pair),
        grid=(b, s // tq),
        in_specs=[
            pl.BlockSpec(memory_space=pltpu.SMEM),
            pl.BlockSpec((1, n_heads, tq, LANES), qhead),
            pl.BlockSpec((1, n_heads, s, LANES), full),
            pl.BlockSpec((1, n_pair, s, LANES), full),
            pl.BlockSpec((1, tq, width), qrow),
            pl.BlockSpec((1, tq, d), qrow),
            _single((width, d), lambda bi, i: (0, 0)),
        ],
        out_specs=pl.BlockSpec((1, tq, d), qrow),
        out_shape=jax.ShapeDtypeStruct((b, s, d), F32),
        scratch_shapes=[
            pltpu.VMEM((n_pair, tq, LANES), F32),
        ],
        compiler_params=_params(("arbitrary", "arbitrary")),
        name="mla_attn",
    )(_logits_bounded(q_norm, k_norm, qk_dim, qk_dim ** -0.5 * LOG2E), q, k, v, z, x, w_out.astype(BF16))


def kernel(x, positions, a_norm, a_w_in, a_conv_w, a_conv_b, a_w_out, b_norm, b_w_in, b_q_norm, b_k_norm, b_w_out, c_norm, c_w_in, c_q_lat_norm, c_kv_lat_norm, c_w_uq, c_w_ukv, c_q_norm, c_k_norm, c_w_out):
    depth = a_norm.shape[0] + b_norm.shape[0] + c_norm.shape[0]
    pos3 = positions.reshape(positions.shape + (1,))
    for i in range(depth):
        kind, j = i % 3, i // 3
        if kind == 0:
            x = _conv_layer(x, j, a_norm[j], a_w_in, a_conv_w[j], a_conv_b[j], a_w_out)
        elif kind == 1:
            x = _dsa_layer(x, pos3, b_norm[j], b_w_in[j], b_q_norm[j], b_k_norm[j], b_w_out[j])
        else:
            x = _mla_layer(x, pos3, c_norm[j], c_w_in[j], c_q_lat_norm[j], c_kv_lat_norm[j],
                           c_w_uq[j], c_w_ukv[j], c_q_norm[j], c_k_norm[j], c_w_out[j])
    return x
```
